```python
import jax, jax.numpy as jnp
from jax import lax
import numpy as np

D_MODEL = 1024
BATCH = 8
SEQ = 2048
DEPTH = 2

HEAD_DIM = 64
H_SB = 8
H_NSA = 8
NSA_KV_HEADS = 2
H_FOX = 8
H_MEM = 4
N_MEM = 256
N_BRANCH = 3
ROPE_THETA = 500000.0
ROPE_DIM = HEAD_DIM // 4
Q_BLOCK = 128
CMP_STRIDE = 16
CMP_LEN = 2 * CMP_STRIDE
CMP_HIDDEN = 256
SEL_BLOCK = 64
SEL_TOPK = 8
WINDOW = 512
D_FF = -(-(8 * D_MODEL) // (3 * 256)) * 256
W_SB = H_SB * HEAD_DIM
W_NSA = H_NSA * HEAD_DIM
W_NSA_KV = NSA_KV_HEADS * HEAD_DIM
W_FOX = H_FOX * HEAD_DIM
W_MEM = H_MEM * HEAD_DIM
IN_SIZES = (W_SB, W_SB, W_SB,
            W_NSA, W_NSA_KV, W_NSA_KV, W_NSA_KV, W_NSA_KV, W_NSA_KV, W_NSA_KV, 3 * H_NSA,
            W_FOX, W_FOX, W_FOX, H_FOX,
            N_BRANCH * D_MODEL)
D_IN = sum(IN_SIZES)

kernel_name = "hybrid_sb_nsa_fox_block"

F32 = jnp.float32


def rmsnorm(x, g, eps=1e-6):
    xf = x.astype(F32)
    y = xf * lax.rsqrt(jnp.mean(xf * xf, axis=-1, keepdims=True) + eps)
    return (y * g.astype(F32)).astype(x.dtype)


def masked_softmax(logits, mask):
    logits = jnp.where(mask, logits.astype(F32), -jnp.inf)
    m = jnp.max(logits, axis=-1, keepdims=True)
    m = jnp.where(jnp.isfinite(m), m, 0.0)
    e = jnp.where(mask, jnp.exp(logits - m), 0.0)
    den = jnp.sum(e, axis=-1, keepdims=True)
    return e / jnp.where(den > 0, den, 1.0)


def partial_rope(x, pos):
    half = ROPE_DIM // 2
    inv_freq = ROPE_THETA ** (-jnp.arange(half, dtype=F32) / half)
    ang = pos.astype(F32)[:, :, None] * inv_freq
    cos = jnp.cos(ang)[:, :, None, :].astype(x.dtype)
    sin = jnp.sin(ang)[:, :, None, :].astype(x.dtype)
    x1, x2, rest = x[..., :half], x[..., half:ROPE_DIM], x[..., ROPE_DIM:]
    return jnp.concatenate([x1 * cos - x2 * sin, x2 * cos + x1 * sin, rest], axis=-1)


def stick_breaking_attention(q, k, v):
    B, S, H, d = q.shape
    scale = d ** -0.5
    outs = []
    for i in range(S // Q_BLOCK):
        q0, q1 = i * Q_BLOCK, (i + 1) * Q_BLOCK
        z = jnp.einsum('bqhd,bkhd->bhqk', q[:, q0:q1], k[:, :q1]).astype(F32) * scale
        strict = jnp.arange(q1)[None, :] < jnp.arange(q0, q1)[:, None]
        log_keep = jnp.where(strict, jax.nn.log_sigmoid(-z), 0.0)
        later = lax.cumsum(log_keep, axis=3, reverse=True) - log_keep
        w = jnp.where(strict, jnp.exp(jax.nn.log_sigmoid(z) + later), 0.0)
        outs.append(jnp.einsum('bhqk,bkhd->bqhd', w.astype(v.dtype), v[:, :q1]))
    return jnp.concatenate(outs, axis=1)


def forgetting_attention(q, k, v, f_logit):
    B, S, H, d = q.shape
    scale = d ** -0.5
    c = jnp.cumsum(jax.nn.log_sigmoid(f_logit.astype(F32)), axis=1).transpose(0, 2, 1)
    outs = []
    for i in range(S // Q_BLOCK):
        q0, q1 = i * Q_BLOCK, (i + 1) * Q_BLOCK
        logits = (jnp.einsum('bqhd,bkhd->bhqk', q[:, q0:q1], k[:, :q1]).astype(F32) * scale
                  + c[:, :, q0:q1, None] - c[:, :, None, :q1])
        causal = jnp.arange(q1)[None, :] <= jnp.arange(q0, q1)[:, None]
        p = masked_softmax(logits, causal)
        outs.append(jnp.einsum('bhqk,bkhd->bqhd', p.astype(v.dtype), v[:, :q1]))
    return jnp.concatenate(outs, axis=1)


def compress_blocks(x, pe, w1, b1, w2):
    B, S, G, d = x.shape
    ch = x.reshape(B, S // CMP_STRIDE, CMP_STRIDE, G, d)
    blk = jnp.concatenate([ch[:, :-1], ch[:, 1:]], axis=2) + pe[None, None, :, None, :]
    n_cmp = blk.shape[1]
    flat = blk.transpose(0, 1, 3, 2, 4).reshape(B, n_cmp, G, CMP_LEN * d)
    return jax.nn.silu(flat @ w1 + b1) @ w2


def native_sparse_attention(q, kc, vc, ks, vs, kw, vw, gates, pos,
                            pe_k, w1_k, b1_k, w2_k, pe_v, w1_v, b1_v, w2_v):
    B, S, H, d = q.shape
    G = kc.shape[2]
    R = H // G
    scale = d ** -0.5
    t_all = jnp.arange(S)
    qg = q.reshape(B, S, G, R, d)
    qg_rope = partial_rope(q, pos).reshape(B, S, G, R, d)
    ks_r = partial_rope(ks, pos)
    kw_r = partial_rope(kw, pos)

    k_cmp = compress_blocks(kc, pe_k, w1_k, b1_k, w2_k)
    v_cmp = compress_blocks(vc, pe_v, w1_v, b1_v, w2_v)
    n_cmp = k_cmp.shape[1]
    cmp_start = jnp.arange(n_cmp) * CMP_STRIDE
    cmp_valid = (cmp_start + CMP_LEN - 1)[None, :] <= t_all[:, None]
    s_cmp = jnp.einsum('bsgrd,bcgd->bgrsc', qg, k_cmp).astype(F32) * scale
    p_cmp = masked_softmax(s_cmp, cmp_valid)
    o_cmp = jnp.einsum('bgrsc,bcgd->bsgrd', p_cmp.astype(v_cmp.dtype), v_cmp)

    n_sel = S // SEL_BLOCK
    sel_start = jnp.arange(n_sel) * SEL_BLOCK
    overlap = ((cmp_start[:, None] < sel_start[None, :] + SEL_BLOCK)
               & (cmp_start[:, None] + CMP_LEN > sel_start[None, :])).astype(F32)
    p_slc = jnp.einsum('bgrsc,cn->bgsn', p_cmp, overlap)
    blk_id = jnp.arange(n_sel)[None, :]
    sel_valid = sel_start[None, :] <= t_all[:, None]
    forced = (blk_id == 0) | (blk_id == (t_all // SEL_BLOCK)[:, None])
    sel_score = jnp.where(forced, 1e4, jnp.where(sel_valid, p_slc, -1.0))
    k_top = min(SEL_TOPK, n_sel)
    sel_idx = lax.top_k(sel_score, k_top)[1]
    ks_blk = ks_r.reshape(B, n_sel, SEL_BLOCK, G, d).transpose(0, 3, 1, 2, 4)
    vs_blk = vs.reshape(B, n_sel, SEL_BLOCK, G, d).transpose(0, 3, 1, 2, 4)
    b_ix = jnp.arange(B)[:, None, None, None]
    g_ix = jnp.arange(G)[None, :, None, None]
    o_sel, o_win = [], []
    for i in range(S // Q_BLOCK):
        q0, q1 = i * Q_BLOCK, (i + 1) * Q_BLOCK
        t_blk = jnp.arange(q0, q1)
        ib = sel_idx[:, :, q0:q1]
        kg = ks_blk[b_ix, g_ix, ib].reshape(B, G, Q_BLOCK, k_top * SEL_BLOCK, d)
        vg = vs_blk[b_ix, g_ix, ib].reshape(B, G, Q_BLOCK, k_top * SEL_BLOCK, d)
        kpos = (ib[..., None] * SEL_BLOCK + jnp.arange(SEL_BLOCK)).reshape(B, G, Q_BLOCK, k_top * SEL_BLOCK)
        logits = jnp.einsum('bqgrd,bgqnd->bgrqn', qg_rope[:, q0:q1], kg) * scale
        mask = kpos[:, :, None] <= t_blk[None, None, None, :, None]
        p = masked_softmax(logits, mask)
        o_sel.append(jnp.einsum('bgrqn,bgqnd->bqgrd', p.astype(vg.dtype), vg))
        k0 = max(0, q0 - WINDOW + 1)
        logits_w = jnp.einsum('bqgrd,bkgd->bgrqk', qg_rope[:, q0:q1], kw_r[:, k0:q1]) * scale
        s_pos = jnp.arange(k0, q1)[None, :]
        band = (s_pos <= t_blk[:, None]) & (t_blk[:, None] - s_pos < WINDOW)
        p_w = masked_softmax(logits_w, band)
        o_win.append(jnp.einsum('bgrqk,bkgd->bqgrd', p_w.astype(vw.dtype), vw[:, k0:q1]))
    o_sel = jnp.concatenate(o_sel, axis=1)
    o_win = jnp.concatenate(o_win, axis=1)
    g = gates.reshape(B, S, G, R, 3).astype(q.dtype)
    o = g[..., 0:1] * o_cmp + g[..., 1:2] * o_sel + g[..., 2:3] * o_win
    return o.reshape(B, S, H * d)


def hybrid_mixer(h, pos, w_in, b_fox_f, pe_k, w1_k, b1_k, w2_k, pe_v, w1_v, b1_v, w2_v,
                 w_up_sb, w_up_nsa, w_up_fox, w_out):
    B, S, _ = h.shape
    offsets = np.cumsum(IN_SIZES)[:-1].tolist()
    (sb_q, sb_k, sb_v, nsa_q, nsa_kc, nsa_vc, nsa_ks, nsa_vs, nsa_kw, nsa_vw, nsa_g,
     fox_q, fox_k, fox_v, fox_f, merge) = jnp.split(h @ w_in, offsets, axis=-1)

    def heads(t, n):
        return t.reshape(B, S, n, HEAD_DIM)

    o_sb = stick_breaking_attention(heads(sb_q, H_SB), heads(sb_k, H_SB), heads(sb_v, H_SB))
    nsa_gates = jax.nn.sigmoid(nsa_g.astype(F32)).reshape(B, S, H_NSA, 3)
    o_nsa = native_sparse_attention(
        heads(nsa_q, H_NSA), heads(nsa_kc, NSA_KV_HEADS), heads(nsa_vc, NSA_KV_HEADS),
        heads(nsa_ks, NSA_KV_HEADS), heads(nsa_vs, NSA_KV_HEADS),
        heads(nsa_kw, NSA_KV_HEADS), heads(nsa_vw, NSA_KV_HEADS), nsa_gates, pos,
        pe_k, w1_k, b1_k, w2_k, pe_v, w1_v, b1_v, w2_v)
    o_fox = forgetting_attention(heads(fox_q, H_FOX), heads(fox_k, H_FOX), heads(fox_v, H_FOX),
                                 fox_f + b_fox_f)
    gate = jax.nn.sigmoid(merge.astype(F32)).astype(h.dtype).reshape(B, S, N_BRANCH, D_MODEL)
    y = (gate[:, :, 0] * (o_sb.reshape(B, S, W_SB) @ w_up_sb)
         + gate[:, :, 1] * (o_nsa @ w_up_nsa)
         + gate[:, :, 2] * (o_fox.reshape(B, S, W_FOX) @ w_up_fox))
    return y @ w_out


def memory_attention(h, mem_n, wq, wk, wv, wo):
    B, S, _ = h.shape
    M = mem_n.shape[1]
    q = (h @ wq).reshape(B, S, H_MEM, HEAD_DIM)
    k = (mem_n @ wk).reshape(B, M, H_MEM, HEAD_DIM)
    v = (mem_n @ wv).reshape(B, M, H_MEM, HEAD_DIM)
    logits = jnp.einsum('bshd,bmhd->bhsm', q, k).astype(F32) * HEAD_DIM ** -0.5
    p = jax.nn.softmax(logits, axis=-1).astype(v.dtype)
    o = jnp.einsum('bhsm,bmhd->bshd', p, v).reshape(B, S, W_MEM)
    return o @ wo


def swiglu(h, w_gate, w_up, w_down):
    return (jax.nn.silu(h @ w_gate) * (h @ w_up)) @ w_down


def setup_inputs(seed: int = 0) -> dict:
    key = jax.random.key(seed)
    ks = jax.random.split(key, 40)
    nrm = jax.random.normal

    def w(k, shape, fan_in):
        return nrm(k, shape, F32) * fan_in ** -0.5

    def gain(k):
        return 1.0 + 0.05 * nrm(k, (DEPTH, D_MODEL), F32)

    start = jax.random.randint(ks[2], (BATCH, 1), 0, 4096, dtype=jnp.int32)
    positions = (start + jnp.arange(SEQ, dtype=jnp.int32)[None, :]).astype(jnp.int32)
    return {
        "x": nrm(ks[0], (BATCH, SEQ, D_MODEL), F32),
        "mem": nrm(ks[1], (BATCH, N_MEM, D_MODEL), F32),
        "positions": positions,
        "g_pre_mix": gain(ks[3]),
        "g_post_mix": gain(ks[4]),
        "g_pre_mem": gain(ks[5]),
        "g_mem": gain(ks[6]),
        "g_post_mem": gain(ks[7]),
        "g_pre_ffn": gain(ks[8]),
        "g_post_ffn": gain(ks[9]),
        "w_in": w(ks[10], (DEPTH, D_MODEL, D_IN), D_MODEL),
        "b_fox_f": 3.0 + 0.5 * nrm(ks[11], (DEPTH, H_FOX), F32),
        "cmp_pe_k": 0.02 * nrm(ks[12], (DEPTH, CMP_LEN, HEAD_DIM), F32),
        "cmp_w1_k": w(ks[13], (DEPTH, CMP_LEN * HEAD_DIM, CMP_HIDDEN), CMP_LEN * HEAD_DIM),
        "cmp_b1_k": 0.02 * nrm(ks[14], (DEPTH, CMP_HIDDEN), F32),
        "cmp_w2_k": w(ks[15], (DEPTH, CMP_HIDDEN, HEAD_DIM), CMP_HIDDEN),
        "cmp_pe_v": 0.02 * nrm(ks[16], (DEPTH, CMP_LEN, HEAD_DIM), F32),
        "cmp_w1_v": w(ks[17], (DEPTH, CMP_LEN * HEAD_DIM, CMP_HIDDEN), CMP_LEN * HEAD_DIM),
        "cmp_b1_v": 0.02 * nrm(ks[18], (DEPTH, CMP_HIDDEN), F32),
        "cmp_w2_v": w(ks[19], (DEPTH, CMP_HIDDEN, HEAD_DIM), CMP_HIDDEN),
        "w_up_sb": w(ks[20], (DEPTH, W_SB, D_MODEL), W_SB),
        "w_up_nsa": w(ks[21], (DEPTH, W_NSA, D_MODEL), W_NSA),
        "w_up_fox": w(ks[22], (DEPTH, W_FOX, D_MODEL), W_FOX),
        "w_out": w(ks[23], (DEPTH, D_MODEL, D_MODEL), D_MODEL),
        "w_mem_q": w(ks[24], (DEPTH, D_MODEL, W_MEM), D_MODEL),
        "w_mem_k": w(ks[25], (DEPTH, D_MODEL, W_MEM), D_MODEL),
        "w_mem_v": w(ks[26], (DEPTH, D_MODEL, W_MEM), D_MODEL),
        "w_mem_o": w(ks[27], (DEPTH, W_MEM, D_MODEL), W_MEM),
        "w_ffn_gate": w(ks[28], (DEPTH, D_MODEL, D_FF), D_MODEL),
        "w_ffn_up": w(ks[29], (DEPTH, D_MODEL, D_FF), D_MODEL),
        "w_ffn_down": w(ks[30], (DEPTH, D_FF, D_MODEL), D_FF),
    }


def reference(x, mem, positions, g_pre_mix, g_post_mix, g_pre_mem, g_mem, g_post_mem,
              g_pre_ffn, g_post_ffn, w_in, b_fox_f, cmp_pe_k, cmp_w1_k, cmp_b1_k, cmp_w2_k,
              cmp_pe_v, cmp_w1_v, cmp_b1_v, cmp_w2_v, w_up_sb, w_up_nsa, w_up_fox, w_out,
              w_mem_q, w_mem_k, w_mem_v, w_mem_o, w_ffn_gate, w_ffn_up, w_ffn_down):
    for l in range(DEPTH):
        h = rmsnorm(x, g_pre_mix[l])
        y = hybrid_mixer(h, positions, w_in[l], b_fox_f[l],
                         cmp_pe_k[l], cmp_w1_k[l], cmp_b1_k[l], cmp_w2_k[l],
                         cmp_pe_v[l], cmp_w1_v[l], cmp_b1_v[l], cmp_w2_v[l],
                         w_up_sb[l], w_up_nsa[l], w_up_fox[l], w_out[l])
        x = x + rmsnorm(y, g_post_mix[l])
        h = rmsnorm(x, g_pre_mem[l])
        y = memory_attention(h, rmsnorm(mem, g_mem[l]), w_mem_q[l], w_mem_k[l], w_mem_v[l], w_mem_o[l])
        x = x + rmsnorm(y, g_post_mem[l])
        h = rmsnorm(x, g_pre_ffn[l])
        y = swiglu(h, w_ffn_gate[l], w_ffn_up[l], w_ffn_down[l])
        x = x + rmsnorm(y, g_post_ffn[l])
    return x
```

```python
import functools

import numpy as np
import jax
import jax.numpy as jnp
from jax import lax
from jax.experimental import pallas as pl
from jax.experimental.pallas import tpu as pltpu

F32 = jnp.float32
BF16 = jnp.bfloat16

D_MODEL = 1024
HEAD_DIM = 64
H_SB = 8
H_NSA = 8
NSA_KV_HEADS = 2
H_FOX = 8
H_MEM = 4
N_BRANCH = 3
ROPE_THETA = 500000.0
ROPE_DIM = HEAD_DIM // 4
CMP_STRIDE = 16
CMP_LEN = 2 * CMP_STRIDE
CMP_HIDDEN = 256
SEL_BLOCK = 64
SEL_SHIFT = SEL_BLOCK.bit_length() - 1
SEL_TOPK = 8
WINDOW = 512
W_HEADS = 8 * HEAD_DIM
W_MEM = H_MEM * HEAD_DIM
EPS = 1e-6
SCALE = HEAD_DIM ** -0.5
NEG = -1e30

LANES = 128
TQ = 128

C_MERGE = 0
C_SBQ, C_SBK, C_SBV = 3072, 3584, 4096
C_NQ = 4608
C_FQ, C_FK, C_FV = 5120, 5632, 6144
C_NKV = 6656
C_MISC = 7424
MISC_W = 256
N_IN = 7680
_O_NSA_G, _O_FOX_Q, _O_FOX_F, _O_MERGE, _O_END = 2816, 2840, 4376, 4384, 7456

VMEM_LIMIT = 56 * 1024 * 1024


def _cparams(sem):
    return pltpu.CompilerParams(dimension_semantics=sem, vmem_limit_bytes=VMEM_LIMIT)


def _dot(a, b):
    return jnp.dot(a, b, preferred_element_type=F32)


def _dot_nt(a, b):
    return lax.dot_general(a, b, (((1,), (1,)), ((), ())), preferred_element_type=F32)


def _rms(x, g):
    ms = jnp.mean(x * x, axis=-1, keepdims=True)
    return x * lax.rsqrt(ms + EPS) * g


def _split_bf16(x):
    hi = x.astype(BF16)
    lo = (x - hi.astype(F32)).astype(BF16)
    return jnp.concatenate([hi, lo], axis=1)


def _norm_matmul_kernel(x_ref, g_ref, w_ref, o_ref, h_ref):
    @pl.when(pl.program_id(1) == 0)
    def _():
        h_ref[...] = _rms(x_ref[...], g_ref[...]).astype(BF16)

    o_ref[...] = _dot(h_ref[...], w_ref[...])


def _norm_matmul(x, g, w, tm, tn):
    T, D = x.shape
    N = w.shape[1]
    return pl.pallas_call(
        _norm_matmul_kernel,
        grid=(T // tm, N // tn),
        in_specs=[pl.BlockSpec((tm, D), lambda i, j: (i, 0)),
                  pl.BlockSpec((1, D), lambda i, j: (0, 0)),
                  pl.BlockSpec((D, tn), lambda i, j: (0, j))],
        out_specs=pl.BlockSpec((tm, tn), lambda i, j: (i, j)),
        out_shape=jax.ShapeDtypeStruct((T, N), F32),
        scratch_shapes=[pltpu.VMEM((tm, D), BF16)],
        compiler_params=_cparams(("parallel", "arbitrary")),
    )(x, g, w)


def _sb_kernel(q_ref, k_ref, v_ref, u_ref, o_ref):
    i = pl.program_id(2)
    lane = lax.broadcasted_iota(jnp.int32, (TQ, LANES), 1)
    row = lax.broadcasted_iota(jnp.int32, (TQ, LANES), 0)
    low = lane < HEAD_DIM
    q = q_ref[...] * SCALE
    zeros = jnp.zeros((TQ, LANES), F32)
    outs = []
    for hh in range(2):
        qh = jnp.where(low if hh == 0 else jnp.logical_not(low), q, 0.0).astype(BF16)

        def body(j, carry, qh=qh):
            cs, acc = carry
            kb = i - j
            off = pl.multiple_of(kb * LANES, LANES)
            k = k_ref[pl.ds(off, LANES), :].astype(BF16)
            v = v_ref[pl.ds(off, LANES), :].astype(BF16)
            z = _dot_nt(qh, k)
            sp = jnp.log(1.0 + jnp.exp(-jnp.abs(z)))
            strict = (lane + off) < (row + i * TQ)
            log_keep = jnp.where(strict, -jnp.maximum(z, 0.0) - sp, 0.0)
            c2 = _dot(_split_bf16(log_keep), u_ref[...])
            later = c2[:, :LANES] + cs
            w = jnp.where(strict, jnp.exp(jnp.minimum(z, 0.0) - sp + later), 0.0)
            acc = acc + _dot(w.astype(BF16), v)
            return cs + c2[:, LANES:], acc

        _, acc = lax.fori_loop(0, i + 1, body, (zeros, zeros))
        outs.append(acc)
    o_ref[...] = jnp.where(low, outs[0], outs[1]).astype(o_ref.dtype)


def _sb_attention(proj, u2, B, S):
    nq = S // TQ
    cq, ck, cv = C_SBQ // LANES, C_SBK // LANES, C_SBV // LANES
    return pl.pallas_call(
        _sb_kernel,
        grid=(B, H_SB // 2, nq),
        in_specs=[pl.BlockSpec((TQ, LANES), lambda b, p, i: (b * nq + i, cq + p)),
                  pl.BlockSpec((S, LANES), lambda b, p, i: (b, ck + p)),
                  pl.BlockSpec((S, LANES), lambda b, p, i: (b, cv + p)),
                  pl.BlockSpec((2 * LANES, 2 * LANES), lambda b, p, i: (0, 0))],
        out_specs=pl.BlockSpec((TQ, LANES), lambda b, p, i: (b * nq + i, p)),
        out_shape=jax.ShapeDtypeStruct((B * S, W_HEADS), BF16),
        compiler_params=_cparams(("parallel", "parallel", "arbitrary")),
    )(proj, proj, proj, u2)


def _fox_kernel(q_ref, k_ref, v_ref, cc_ref, cr_ref, o_ref):
    i = pl.program_id(2)
    lane = lax.broadcasted_iota(jnp.int32, (TQ, LANES), 1)
    row = lax.broadcasted_iota(jnp.int32, (TQ, LANES), 0)
    low = lane < HEAD_DIM
    q = q_ref[...] * SCALE
    outs = []
    for hh in range(2):
        qh = jnp.where(low if hh == 0 else jnp.logical_not(low), q, 0.0).astype(BF16)
        c_col = cc_ref[0, hh]

        def body(j, carry, qh=qh, c_col=c_col, hh=hh):
            m, l, acc = carry
            kb = i - j
            off = pl.multiple_of(kb * LANES, LANES)
            k = k_ref[pl.ds(off, LANES), :].astype(BF16)
            v = v_ref[pl.ds(off, LANES), :].astype(BF16)
            c_row = cr_ref[0, hh, pl.ds(kb, 1), :]
            s = _dot_nt(qh, k) + (c_col - c_row)
            causal = (lane + off) <= (row + i * TQ)
            s = jnp.where(causal, s, NEG)
            m_new = jnp.maximum(m, jnp.max(s, axis=-1, keepdims=True))
            p = jnp.where(causal, jnp.exp(s - m_new), 0.0)
            alpha = jnp.exp(m - m_new)
            l = alpha * l + jnp.sum(p, axis=-1, keepdims=True)
            acc = alpha * acc + _dot(p.astype(BF16), v)
            return m_new, l, acc

        init = (jnp.full((TQ, 1), NEG, F32), jnp.zeros((TQ, 1), F32), jnp.zeros((TQ, LANES), F32))
        _, l, acc = lax.fori_loop(0, i + 1, body, init)
        outs.append(acc / jnp.where(l > 0.0, l, 1.0))
    o_ref[...] = jnp.where(low, outs[0], outs[1]).astype(o_ref.dtype)


def _fox_attention(proj, c_col, c_row, B, S):
    nq = S // TQ
    cq, ck, cv = C_FQ // LANES, C_FK // LANES, C_FV // LANES
    return pl.pallas_call(
        _fox_kernel,
        grid=(B, H_FOX // 2, nq),
        in_specs=[pl.BlockSpec((TQ, LANES), lambda b, p, i: (b * nq + i, cq + p)),
                  pl.BlockSpec((S, LANES), lambda b, p, i: (b, ck + p)),
                  pl.BlockSpec((S, LANES), lambda b, p, i: (b, cv + p)),
                  pl.BlockSpec((1, 2, TQ, 1), lambda b, p, i: (b, p, i, 0)),
                  pl.BlockSpec((1, 2, S // LANES, LANES), lambda b, p, i: (b, p, 0, 0))],
        out_specs=pl.BlockSpec((TQ, LANES), lambda b, p, i: (b * nq + i, p)),
        out_shape=jax.ShapeDtypeStruct((B * S, W_HEADS), BF16),
        compiler_params=_cparams(("parallel", "parallel", "arbitrary")),
    )(proj, proj, proj, c_col, c_row)


def _rope(x, cos, s_up, s_dn):
    return x * cos + pltpu.roll(x, ROPE_DIM // 2, axis=1) * s_up + pltpu.roll(x, LANES - ROPE_DIM // 2, axis=1) * s_dn


def _prep_kernel(misc_ref, bias_ref, pos_ref, rc_ref, q_ref, ks_ref, kw_ref,
                 gates_ref, c_ref, qr_ref, ksr_ref, kwr_ref, carry_ref, *, ts):
    @pl.when(pl.program_id(1) == 0)
    def _():
        carry_ref[...] = jnp.zeros_like(carry_ref)

    logits = misc_ref[...]
    gates_ref[...] = jax.nn.sigmoid(logits)
    zf = logits + bias_ref[...]
    x = jnp.minimum(zf, 0.0) - jnp.log(1.0 + jnp.exp(-jnp.abs(zf)))
    row = lax.broadcasted_iota(jnp.int32, (ts, LANES), 0)
    sh = 1
    while sh < ts:
        x = x + jnp.where(row >= sh, pltpu.roll(x, sh, axis=0), 0.0)
        sh *= 2
    x = x + carry_ref[...]
    c_ref[...] = x
    carry_ref[...] = x[ts - 1:ts, :]

    ang = pos_ref[0].astype(F32) * rc_ref[0:1, :]
    cos = jnp.cos(ang)
    sin = jnp.sin(ang)
    s_up = sin * rc_ref[1:2, :]
    s_dn = sin * rc_ref[2:3, :]
    for a in range(W_HEADS // LANES):
        qa = q_ref[:, a * LANES:(a + 1) * LANES]
        qr_ref[:, a * LANES:(a + 1) * LANES] = _rope(qa, cos, s_up, s_dn) * SCALE
    ksr_ref[...] = _rope(ks_ref[...], cos, s_up, s_dn).astype(BF16)
    kwr_ref[...] = _rope(kw_ref[...], cos, s_up, s_dn).astype(BF16)


def _prep(proj, bias_row, pos3, rope_c, B, S):
    ts = min(512, S)
    nt = S // ts
    T = B * S
    row_blk = lambda w: pl.BlockSpec((ts, w), lambda b, t: (b * nt + t, 0))
    return pl.pallas_call(
        functools.partial(_prep_kernel, ts=ts),
        grid=(B, nt),
        in_specs=[pl.BlockSpec((ts, LANES), lambda b, t: (b * nt + t, C_MISC // LANES)),
                  pl.BlockSpec((1, LANES), lambda b, t: (0, 0)),
                  pl.BlockSpec((1, ts, 1), lambda b, t: (b, t, 0)),
                  pl.BlockSpec((8, LANES), lambda b, t: (0, 0)),
                  pl.BlockSpec((ts, W_HEADS), lambda b, t: (b * nt + t, C_NQ // W_HEADS)),
                  pl.BlockSpec((ts, LANES), lambda b, t: (b * nt + t, C_NKV // LANES + 2)),
                  pl.BlockSpec((ts, LANES), lambda b, t: (b * nt + t, C_NKV // LANES + 4))],
        out_specs=[row_blk(LANES), row_blk(LANES), row_blk(W_HEADS), row_blk(LANES), row_blk(LANES)],
        out_shape=[jax.ShapeDtypeStruct((T, LANES), F32),
                   jax.ShapeDtypeStruct((T, LANES), F32),
                   jax.ShapeDtypeStruct((T, W_HEADS), F32),
                   jax.ShapeDtypeStruct((T, LANES), BF16),
                   jax.ShapeDtypeStruct((T, LANES), BF16)],
        scratch_shapes=[pltpu.VMEM((1, LANES), F32)],
        compiler_params=_cparams(("parallel", "arbitrary")),
    )(proj, bias_row, pos3, rope_c, proj, proj, proj)


def _compress_kernel(xk_ref, xv_ref, pek_ref, w1k_ref, b1k_ref, w2k_ref,
                     pev_ref, w1v_ref, b1v_ref, w2v_ref, ok_ref, ov_ref, *, nc):
    half = CMP_STRIDE * HEAD_DIM
    for x_ref, pe_ref, w1_ref, b1_ref, w2_ref, o_ref in (
            (xk_ref, pek_ref, w1k_ref, b1k_ref, w2k_ref, ok_ref),
            (xv_ref, pev_ref, w1v_ref, b1v_ref, w2v_ref, ov_ref)):
        outs = []
        for g in range(NSA_KV_HEADS):
            x = x_ref[0, g]
            h_first = _dot((x + pe_ref[0:1, :]).astype(BF16), w1_ref[0:half, :])
            h_second = _dot((x + pe_ref[1:2, :]).astype(BF16), w1_ref[half:2 * half, :])
            h = h_first + pltpu.roll(h_second, nc - 1, axis=0) + b1_ref[...]
            a = h * jax.nn.sigmoid(h)
            outs.append(_dot(a.astype(BF16), w2_ref[...]))
        o_ref[0] = jnp.concatenate(outs, axis=1).astype(o_ref.dtype)


def _compress(xk, xv, pek, w1k, b1k, w2k, pev, w1v, b1v, w2v, B, S):
    nc = S // CMP_STRIDE
    G = NSA_KV_HEADS
    x_spec = pl.BlockSpec((1, G, nc, CMP_STRIDE * HEAD_DIM), lambda b: (b, 0, 0, 0))
    full = lambda a: pl.BlockSpec(a.shape, lambda b: (0,) * a.ndim)
    out_spec = pl.BlockSpec((1, nc, LANES), lambda b: (b, 0, 0))
    return pl.pallas_call(
        functools.partial(_compress_kernel, nc=nc),
        grid=(B,),
        in_specs=[x_spec, x_spec, full(pek), full(w1k), full(b1k), full(w2k),
                  full(pev), full(w1v), full(b1v), full(w2v)],
        out_specs=[out_spec, out_spec],
        out_shape=[jax.ShapeDtypeStruct((B, nc, LANES), BF16)] * 2,
        compiler_params=_cparams(("parallel",)),
    )(xk, xv, pek, w1k, b1k, w2k, pev, w1v, b1v, w2v)


def _nsa_kernel(qn_ref, qr_ref, kc_ref, vc_ref, ks_ref, vs_ref, kw_ref, vw_ref, gt_ref, ot_ref,
                o_ref, *, S):
    i = pl.program_id(1)
    nc = S // CMP_STRIDE
    ns = S // SEL_BLOCK
    k_top = min(SEL_TOPK, ns)
    R = H_NSA // NSA_KV_HEADS
    RT = R * TQ
    lane = lax.broadcasted_iota(jnp.int32, (TQ, LANES), 1)
    row = lax.broadcasted_iota(jnp.int32, (TQ, LANES), 0)
    t_q = row + i * TQ
    low = lane < HEAD_DIM
    gates = gt_ref[...]
    qn = qn_ref[...] * SCALE
    qr = qr_ref[...]
    n_sub = lax.broadcasted_iota(jnp.int32, (LANES, LANES), 0)
    s_lane = lax.broadcasted_iota(jnp.int32, (LANES, LANES), 1)

    def rep(x):
        return jnp.concatenate([x] * R, axis=0)

    def online_softmax_step(carry, s, mask, v):
        m, l, acc = carry
        s = jnp.where(mask, s, NEG)
        m_new = jnp.maximum(m, jnp.max(s, axis=-1, keepdims=True))
        p = jnp.where(mask, jnp.exp(s - m_new), 0.0)
        alpha = jnp.exp(m - m_new)
        l = alpha * l + jnp.sum(p, axis=-1, keepdims=True)
        acc = alpha * acc + _dot(p.astype(BF16), v)
        return m_new, l, acc

    init = (jnp.full((RT, 1), NEG, F32), jnp.zeros((RT, 1), F32), jnp.zeros((RT, LANES), F32))
    heads = [None] * H_NSA
    for g in range(NSA_KV_HEADS):
        in_g = low if g == 0 else jnp.logical_not(low)

        def stack(qfull, g=g, in_g=in_g):
            parts = []
            for r in range(R):
                a, b = divmod(R * g + r, 2)
                blk = qfull[:, a * LANES:(a + 1) * LANES]
                if b != g:
                    blk = pltpu.roll(blk, HEAD_DIM, axis=1)
                parts.append(jnp.where(in_g, blk, 0.0))
            return jnp.concatenate(parts, axis=0).astype(BF16)

        sc = _dot_nt(stack(qn), kc_ref[0])
        c_idx = lax.broadcasted_iota(jnp.int32, (RT, nc), 1)
        t_c = (lax.broadcasted_iota(jnp.int32, (RT, nc), 0) & (TQ - 1)) + i * TQ
        c_valid = c_idx * CMP_STRIDE + (CMP_LEN - 1) <= t_c
        sc = jnp.where(c_valid, sc, NEG)
        e = jnp.where(c_valid, jnp.exp(sc - jnp.max(sc, axis=-1, keepdims=True)), 0.0)
        den = jnp.sum(e, axis=-1, keepdims=True)
        p_cmp = e / jnp.where(den > 0.0, den, 1.0)
        o_cmp = _dot(p_cmp.astype(BF16), vc_ref[0])

        p_sum = p_cmp[0:TQ]
        for r in range(1, R):
            p_sum = p_sum + p_cmp[r * TQ:(r + 1) * TQ]
        p_slc = _dot_nt(ot_ref[...], _split_bf16(p_sum))
        n_idx = lax.broadcasted_iota(jnp.int32, (ns, TQ), 0)
        t_s = lax.broadcasted_iota(jnp.int32, (ns, TQ), 1) + i * TQ
        forced = jnp.logical_or(n_idx == 0, n_idx == (t_s >> SEL_SHIFT))
        score = jnp.where(forced, 1e4, jnp.where(n_idx * SEL_BLOCK <= t_s, p_slc, -1.0))
        rank = jnp.zeros((ns, TQ), F32)
        for m_i in range(ns):
            sm = score[m_i:m_i + 1, :]
            ge = jnp.where(sm >= score, 1.0, 0.0)
            gt = jnp.where(sm > score, 1.0, 0.0)
            rank = rank + jnp.where(n_idx > m_i, ge, gt)
        sel_t = jnp.where(rank < k_top, 1.0, 0.0)
        if ns < LANES:
            sel_t = jnp.concatenate([sel_t, jnp.zeros((LANES - ns, TQ), F32)], axis=0)
        sel = sel_t.T.astype(BF16)

        qs = stack(qr)

        def sel_body(j, carry, qs=qs, sel=sel):
            kb = i - j
            off = pl.multiple_of(kb * LANES, LANES)
            expand = jnp.where(((s_lane + off) >> SEL_SHIFT) == n_sub, 1.0, 0.0).astype(BF16)
            chosen = _dot(sel, expand)
            mask = rep(jnp.logical_and(chosen > 0.5, (lane + off) <= t_q))
            s = _dot_nt(qs, ks_ref[pl.ds(off, LANES), :])
            return online_softmax_step(carry, s, mask, vs_ref[pl.ds(off, LANES), :].astype(BF16))

        _, l, acc = lax.fori_loop(0, i + 1, sel_body, init)
        o_sel = acc / jnp.where(l > 0.0, l, 1.0)

        def win_body(j, carry, qs=qs):
            kb = i - j
            off = pl.multiple_of(kb * LANES, LANES)
            s_pos = lane + off
            mask = rep(jnp.logical_and(s_pos <= t_q, t_q - s_pos < WINDOW))
            s = _dot_nt(qs, kw_ref[pl.ds(off, LANES), :])
            return online_softmax_step(carry, s, mask, vw_ref[pl.ds(off, LANES), :].astype(BF16))

        n_win = jnp.minimum(i, (WINDOW + TQ - 2) // LANES) + 1
        _, l, acc = lax.fori_loop(0, n_win, win_body, init)
        o_win = acc / jnp.where(l > 0.0, l, 1.0)

        for r in range(R):
            h = R * g + r
            rows = slice(r * TQ, (r + 1) * TQ)
            o = (gates[:, 3 * h:3 * h + 1] * o_cmp[rows] + gates[:, 3 * h + 1:3 * h + 2] * o_sel[rows]
                 + gates[:, 3 * h + 2:3 * h + 3] * o_win[rows])
            if h % 2 != g:
                o = pltpu.roll(o, HEAD_DIM, axis=1)
            heads[h] = o
    for a in range(H_NSA // 2):
        o_ref[:, a * LANES:(a + 1) * LANES] = jnp.where(low, heads[2 * a], heads[2 * a + 1]).astype(o_ref.dtype)


def _nsa_attention(proj, q_rope, k_cmp, v_cmp, ks_r, kw_r, gates, overlap_t, B, S):
    nq = S // TQ
    nc = S // CMP_STRIDE
    ckv = C_NKV // LANES
    kv_full = lambda c: pl.BlockSpec((S, LANES), lambda b, i: (b, c))
    return pl.pallas_call(
        functools.partial(_nsa_kernel, S=S),
        grid=(B, nq),
        in_specs=[pl.BlockSpec((TQ, W_HEADS), lambda b, i: (b * nq + i, C_NQ // W_HEADS)),
                  pl.BlockSpec((TQ, W_HEADS), lambda b, i: (b * nq + i, 0)),
                  pl.BlockSpec((1, nc, LANES), lambda b, i: (b, 0, 0)),
                  pl.BlockSpec((1, nc, LANES), lambda b, i: (b, 0, 0)),
                  kv_full(0), kv_full(ckv + 3), kv_full(0), kv_full(ckv + 5),
                  pl.BlockSpec((TQ, LANES), lambda b, i: (b * nq + i, 0)),
                  pl.BlockSpec(overlap_t.shape, lambda b, i: (0, 0))],
        out_specs=pl.BlockSpec((TQ, W_HEADS), lambda b, i: (b * nq + i, 0)),
        out_shape=jax.ShapeDtypeStruct((B * S, W_HEADS), BF16),
        compiler_params=_cparams(("parallel", "arbitrary")),
    )(proj, q_rope, k_cmp, v_cmp, ks_r, proj, kw_r, proj, gates, overlap_t)


def _merge_kernel(x_ref, osb_ref, onsa_ref, ofox_ref, m0_ref, m1_ref, m2_ref,
                  wsb_ref, wnsa_ref, wfox_ref, wout_ref, g_ref, o_ref):
    y = jax.nn.sigmoid(m0_ref[...]) * _dot(osb_ref[...], wsb_ref[...])
    y = y + jax.nn.sigmoid(m1_ref[...]) * _dot(onsa_ref[...], wnsa_ref[...])
    y = y + jax.nn.sigmoid(m2_ref[...]) * _dot(ofox_ref[...], wfox_ref[...])
    z = _dot(y.astype(BF16), wout_ref[...])
    o_ref[...] = x_ref[...] + _rms(z, g_ref[...])


def _merge(x, o_sb, o_nsa, o_fox, proj, w_sb, w_nsa, w_fox, w_out, g, tm):
    T, D = x.shape
    row = lambda w: pl.BlockSpec((tm, w), lambda i: (i, 0))
    full = lambda a: pl.BlockSpec(a.shape, lambda i: (0, 0))
    gate = lambda c: pl.BlockSpec((tm, D), lambda i: (i, C_MERGE // D + c))
    return pl.pallas_call(
        _merge_kernel,
        grid=(T // tm,),
        in_specs=[row(D), row(W_HEADS), row(W_HEADS), row(W_HEADS), gate(0), gate(1), gate(2),
                  full(w_sb), full(w_nsa), full(w_fox), full(w_out), full(g)],
        out_specs=row(D),
        out_shape=jax.ShapeDtypeStruct((T, D), F32),
        compiler_params=_cparams(("parallel",)),
    )(x, o_sb, o_nsa, o_fox, proj, proj, proj, w_sb, w_nsa, w_fox, w_out, g)


def _mem_kv_kernel(mem_ref, g_ref, wk_ref, wv_ref, k_ref, v_ref):
    mn = _rms(mem_ref[0], g_ref[...]).astype(BF16)
    k_ref[0] = _dot(mn, wk_ref[...]).astype(BF16)
    v_ref[0] = _dot(mn, wv_ref[...]).astype(BF16)


def _mem_kv(mem, g, wk, wv):
    B, M, D = mem.shape
    full = lambda a: pl.BlockSpec(a.shape, lambda b: (0, 0))
    out_spec = pl.BlockSpec((1, M, W_MEM), lambda b: (b, 0, 0))
    return pl.pallas_call(
        _mem_kv_kernel,
        grid=(B,),
        in_specs=[pl.BlockSpec((1, M, D), lambda b: (b, 0, 0)), full(g), full(wk), full(wv)],
        out_specs=[out_spec, out_spec],
        out_shape=[jax.ShapeDtypeStruct((B, M, W_MEM), BF16)] * 2,
        compiler_params=_cparams(("parallel",)),
    )(mem, g, wk, wv)


def _mem_attn_kernel(x_ref, g_ref, wq_ref, k_ref, v_ref, wo_ref, gp_ref, o_ref):
    x = x_ref[...]
    q = _dot(_rms(x, g_ref[...]).astype(BF16), wq_ref[...]) * SCALE
    k = k_ref[0]
    v = v_ref[0]
    lane = lax.broadcasted_iota(jnp.int32, q.shape, 1)
    o = jnp.zeros(q.shape, F32)
    for h in range(H_MEM):
        in_h = jnp.logical_and(lane >= h * HEAD_DIM, lane < (h + 1) * HEAD_DIM)
        s = _dot_nt(jnp.where(in_h, q, 0.0).astype(BF16), k)
        e = jnp.exp(s - jnp.max(s, axis=-1, keepdims=True))
        p = e / jnp.sum(e, axis=-1, keepdims=True)
        o = o + jnp.where(in_h, _dot(p.astype(BF16), v), 0.0)
    y = _dot(o.astype(BF16), wo_ref[...])
    o_ref[...] = x + _rms(y, gp_ref[...])


def _mem_attn(x, g_pre, wq, k, v, wo, g_post, B, S, tm):
    T, D = x.shape
    M = k.shape[1]
    nt = S // tm
    full = lambda a: pl.BlockSpec(a.shape, lambda b, i: (0, 0))
    kv_spec = pl.BlockSpec((1, M, W_MEM), lambda b, i: (b, 0, 0))
    row = pl.BlockSpec((tm, D), lambda b, i: (b * nt + i, 0))
    return pl.pallas_call(
        _mem_attn_kernel,
        grid=(B, nt),
        in_specs=[row, full(g_pre), full(wq), kv_spec, kv_spec, full(wo), full(g_post)],
        out_specs=row,
        out_shape=jax.ShapeDtypeStruct((T, D), F32),
        compiler_params=_cparams(("parallel", "parallel")),
    )(x, g_pre, wq, k, v, wo, g_post)


def _ffn_kernel(x_ref, g_ref, wg_ref, wu_ref, wd_ref, gp_ref, o_ref, h_ref, acc_ref):
    j = pl.program_id(1)

    @pl.when(j == 0)
    def _():
        h_ref[...] = _rms(x_ref[...], g_ref[...]).astype(BF16)
        acc_ref[...] = jnp.zeros_like(acc_ref)

    h = h_ref[...]
    a = _dot(h, wg_ref[...])
    u = _dot(h, wu_ref[...])
    acc_ref[...] += _dot((a * jax.nn.sigmoid(a) * u).astype(BF16), wd_ref[...])

    @pl.when(j == pl.num_programs(1) - 1)
    def _():
        o_ref[...] = x_ref[...] + _rms(acc_ref[...], gp_ref[...])


def _ffn(x, g_pre, wg, wu, wd, g_post, tm, tf):
    T, D = x.shape
    F = wg.shape[1]
    row = pl.BlockSpec((tm, D), lambda i, j: (i, 0))
    vec = pl.BlockSpec((1, D), lambda i, j: (0, 0))
    return pl.pallas_call(
        _ffn_kernel,
        grid=(T // tm, F // tf),
        in_specs=[row, vec,
                  pl.BlockSpec((D, tf), lambda i, j: (0, j)),
                  pl.BlockSpec((D, tf), lambda i, j: (0, j)),
                  pl.BlockSpec((tf, D), lambda i, j: (j, 0)),
                  vec],
        out_specs=row,
        out_shape=jax.ShapeDtypeStruct((T, D), F32),
        scratch_shapes=[pltpu.VMEM((tm, D), BF16), pltpu.VMEM((tm, D), F32)],
        compiler_params=_cparams(("parallel", "arbitrary")),
    )(x, g_pre, wg, wu, wd, g_post)


def _scan_matrix():
    j = np.arange(LANES)
    later = (j[:, None] > j[None, :]).astype(np.float32)
    u = np.concatenate([later, np.ones((LANES, LANES), np.float32)], axis=1)
    return jnp.asarray(np.concatenate([u, u], axis=0), dtype=BF16)


def _overlap_t(S):
    nc, ns = S // CMP_STRIDE, S // SEL_BLOCK
    c0 = np.arange(nc) * CMP_STRIDE
    n0 = np.arange(ns) * SEL_BLOCK
    ov = (c0[None, :] < n0[:, None] + SEL_BLOCK) & (c0[None, :] + CMP_LEN > n0[:, None])
    ov = ov & (np.arange(nc)[None, :] < nc - 1)
    ov = ov.astype(np.float32)
    return jnp.asarray(np.concatenate([ov, ov], axis=1), dtype=BF16)


def _rope_rows():
    half = ROPE_DIM // 2
    inv_freq = ROPE_THETA ** (-jnp.arange(half, dtype=F32) / half)
    d = np.arange(LANES) % HEAD_DIM
    rot = d < ROPE_DIM
    freq = jnp.where(jnp.asarray(rot), inv_freq[jnp.asarray(d % half)], 0.0)
    s_up = jnp.asarray(((d >= half) & rot).astype(np.float32))
    s_dn = jnp.asarray(-(d < half).astype(np.float32))
    rows = jnp.stack([freq, s_up, s_dn] + [jnp.zeros((LANES,), F32)] * 5)
    return rows.astype(F32)


def _reorder_w_in(w):
    pad = jnp.zeros((w.shape[0], MISC_W - (_O_FOX_Q - _O_NSA_G) - (_O_MERGE - _O_FOX_F)), w.dtype)
    return jnp.concatenate([
        w[:, _O_MERGE:_O_END],
        w[:, :_O_NSA_G - 6 * LANES],
        w[:, _O_FOX_Q:_O_FOX_F],
        w[:, _O_NSA_G - 6 * LANES:_O_NSA_G],
        w[:, _O_NSA_G:_O_FOX_Q],
        w[:, _O_FOX_F:_O_MERGE],
        pad], axis=1).astype(BF16)


def _row_tile(n, target):
    t = min(n, target)
    while n % t:
        t //= 2
    return t


def kernel(x, mem, positions, g_pre_mix, g_post_mix, g_pre_mem, g_mem, g_post_mem, g_pre_ffn, g_post_ffn,
           w_in, b_fox_f, cmp_pe_k, cmp_w1_k, cmp_b1_k, cmp_w2_k, cmp_pe_v, cmp_w1_v, cmp_b1_v, cmp_w2_v,
           w_up_sb, w_up_nsa, w_up_fox, w_out, w_mem_q, w_mem_k, w_mem_v, w_mem_o,
           w_ffn_gate, w_ffn_up, w_ffn_down):
    B, S, D = x.shape
    T = B * S
    depth = w_in.shape[0]
    G = NSA_KV_HEADS
    nc = S // CMP_STRIDE
    u2 = _scan_matrix()
    overlap_t = _overlap_t(S)
    rope_c = _rope_rows()
    pos3 = positions.reshape(B, S, 1)
    vec = lambda g: g.reshape(1, -1)
    pe_rows = lambda pe: pe.reshape(2, CMP_STRIDE * HEAD_DIM)
    tm_big = _row_tile(T, 1024)
    tm_mid = _row_tile(T, 512)
    d_ff = w_ffn_gate.shape[2]
    tf = d_ff // 2 if (d_ff // 2) % LANES == 0 else d_ff

    xf = x.reshape(T, D)
    for l in range(depth):
        proj = _norm_matmul(xf, vec(g_pre_mix[l]), _reorder_w_in(w_in[l]), tm_big, N_IN // 5)

        o_sb = _sb_attention(proj, u2, B, S)

        bias_row = jnp.zeros((1, LANES), F32).at[0, 24:32].set(b_fox_f[l])
        gates, c_all, q_rope, ks_r, kw_r = _prep(proj, bias_row, pos3, rope_c, B, S)
        c = c_all.reshape(B, S, LANES)[:, :, 24:32].transpose(0, 2, 1)
        o_fox = _fox_attention(proj, c.reshape(B, H_FOX, S, 1), c.reshape(B, H_FOX, S // LANES, LANES), B, S)

        kv_c = proj[:, C_NKV:C_NKV + 2 * LANES].reshape(B, nc, CMP_STRIDE, 2, G, HEAD_DIM)
        kv_c = kv_c.transpose(3, 0, 4, 1, 2, 5).reshape(2, B, G, nc, CMP_STRIDE * HEAD_DIM)
        k_cmp, v_cmp = _compress(
            kv_c[0], kv_c[1],
            pe_rows(cmp_pe_k[l]), cmp_w1_k[l].astype(BF16), vec(cmp_b1_k[l]), cmp_w2_k[l].astype(BF16),
            pe_rows(cmp_pe_v[l]), cmp_w1_v[l].astype(BF16), vec(cmp_b1_v[l]), cmp_w2_v[l].astype(BF16), B, S)
        o_nsa = _nsa_attention(proj, q_rope, k_cmp, v_cmp, ks_r, kw_r, gates, overlap_t, B, S)

        xf = _merge(xf, o_sb, o_nsa, o_fox, proj, w_up_sb[l].astype(BF16), w_up_nsa[l].astype(BF16),
                    w_up_fox[l].astype(BF16), w_out[l].astype(BF16), vec(g_post_mix[l]), tm_mid)

        k_mem, v_mem = _mem_kv(mem, vec(g_mem[l]), w_mem_k[l].astype(BF16), w_mem_v[l].astype(BF16))
        xf = _mem_attn(xf, vec(g_pre_mem[l]), w_mem_q[l].astype(BF16), k_mem, v_mem,
                       w_mem_o[l].astype(BF16), vec(g_post_mem[l]), B, S, _row_tile(S, 512))

        xf = _ffn(xf, vec(g_pre_ffn[l]), w_ffn_gate[l].astype(BF16), w_ffn_up[l].astype(BF16),
                  w_ffn_down[l].astype(BF16), vec(g_post_ffn[l]), tm_mid, tf)
    return xf.reshape(B, S, D)
```

```python
import functools

import numpy as np
import jax
import jax.numpy as jnp
from jax import lax
from jax.experimental import pallas as pl
from jax.experimental.pallas import tpu as pltpu

F32 = jnp.float32
BF16 = jnp.bfloat16

D_MODEL = 1024
HEAD_DIM = 64
H_SB = 8
H_NSA = 8
NSA_KV_HEADS = 2
H_FOX = 8
H_MEM = 4
N_BRANCH = 3
ROPE_THETA = 500000.0
ROPE_DIM = HEAD_DIM // 4
CMP_STRIDE = 16
CMP_LEN = 2 * CMP_STRIDE
CMP_HIDDEN = 256
SEL_BLOCK = 64
SEL_SHIFT = SEL_BLOCK.bit_length() - 1
SEL_TOPK = 8
WINDOW = 512
W_HEADS = 8 * HEAD_DIM
W_MEM = H_MEM * HEAD_DIM
EPS = 1e-6
SCALE = HEAD_DIM ** -0.5
NEG = -1e30

LANES = 128
TQ = 128
TQA = 256

C_MERGE = 0
C_SBQ, C_SBK, C_SBV = 3072, 3584, 4096
C_NQ = 4608
C_FQ, C_FK, C_FV = 5120, 5632, 6144
C_NKV = 6656
C_MISC = 7424
MISC_W = 256
N_IN = 7680
_O_NSA_G, _O_FOX_Q, _O_FOX_F, _O_MERGE, _O_END = 2816, 2840, 4376, 4384, 7456

VMEM_LIMIT = 56 * 1024 * 1024


def _cparams(sem):
    return pltpu.CompilerParams(dimension_semantics=sem, vmem_limit_bytes=VMEM_LIMIT)


def _dot(a, b):
    return jnp.dot(a, b, preferred_element_type=F32)


def _dot_nt(a, b):
    return lax.dot_general(a, b, (((1,), (1,)), ((), ())), preferred_element_type=F32)


def _rms(x, g):
    ms = jnp.mean(x * x, axis=-1, keepdims=True)
    return x * lax.rsqrt(ms + EPS) * g


def _split_bf16(x):
    hi = x.astype(BF16)
    lo = (x - hi.astype(F32)).astype(BF16)
    return jnp.concatenate([hi, lo], axis=1)


def _norm_matmul_kernel(x_ref, g_ref, w_ref, o_ref, h_ref):
    @pl.when(pl.program_id(1) == 0)
    def _():
        h_ref[...] = _rms(x_ref[...], g_ref[...]).astype(BF16)

    o_ref[...] = _dot(h_ref[...], w_ref[...])


def _norm_matmul(x, g, w, tm, tn):
    T, D = x.shape
    N = w.shape[1]
    return pl.pallas_call(
        _norm_matmul_kernel,
        grid=(T // tm, N // tn),
        in_specs=[pl.BlockSpec((tm, D), lambda i, j: (i, 0)),
                  pl.BlockSpec((1, D), lambda i, j: (0, 0)),
                  pl.BlockSpec((D, tn), lambda i, j: (0, j))],
        out_specs=pl.BlockSpec((tm, tn), lambda i, j: (i, j)),
        out_shape=jax.ShapeDtypeStruct((T, N), F32),
        scratch_shapes=[pltpu.VMEM((tm, D), BF16)],
        compiler_params=_cparams(("parallel", "arbitrary")),
    )(x, g, w)


def _stack_pair(q, low):
    return jnp.concatenate([jnp.where(low, q, 0.0), jnp.where(low, 0.0, q)], axis=0).astype(BF16)


def _sb_kernel(q_ref, k_ref, v_ref, u_ref, o_ref, acc_ref, cs_ref):
    i = pl.program_id(2)
    rt = 2 * TQA
    lane = lax.broadcasted_iota(jnp.int32, (rt, LANES), 1)
    t_q = (lax.broadcasted_iota(jnp.int32, (rt, LANES), 0) & (TQA - 1)) + i * TQA
    low = lax.broadcasted_iota(jnp.int32, (TQA, LANES), 1) < HEAD_DIM
    qs = _stack_pair(q_ref[...] * SCALE, low)

    def group(kb_first, on_diagonal, cs, acc):
        offs = [pl.multiple_of((kb_first - d) * LANES, LANES) for d in range(n_diag)]
        zs = [_dot_nt(qs, k_ref[pl.ds(off, LANES), :].astype(BF16)) for off in offs]
        log_keeps, log_betas, stricts = [], [], []
        for off, z in zip(offs, zs):
            log_keep = -(jnp.maximum(z, 0.0) + jnp.log(1.0 + jnp.exp(-jnp.abs(z))))
            log_betas.append(z + log_keep)
            if on_diagonal:
                stricts.append((lane + off) < t_q)
                log_keep = jnp.where(stricts[-1], log_keep, 0.0)
            log_keeps.append(log_keep)
        c2s = [_dot(_split_bf16(log_keep), u_ref[...]) for log_keep in log_keeps]
        for d, off in enumerate(offs):
            w = jnp.exp(log_betas[d] + c2s[d][:, :LANES] + cs)
            if on_diagonal:
                w = jnp.where(stricts[d], w, 0.0)
            acc = acc + _dot(w.astype(BF16), v_ref[pl.ds(off, LANES), :].astype(BF16))
            cs = cs + c2s[d][:, LANES:]
        return cs, acc

    n_diag = TQA // LANES
    zeros = jnp.zeros((rt, LANES), F32)
    cs, acc = group(n_diag * i + n_diag - 1, True, zeros, zeros)
    acc_ref[...] = acc
    cs_ref[...] = cs

    def body(j, carry):
        cs, acc = group(n_diag * (i - j) - 1, False, cs_ref[...], acc_ref[...])
        acc_ref[...] = acc
        cs_ref[...] = cs
        return carry

    lax.fori_loop(0, i, body, 0)
    acc = acc_ref[...]
    o_ref[...] = jnp.where(low, acc[:TQA], acc[TQA:]).astype(o_ref.dtype)


def _sb_attention(proj, u2, B, S):
    nq = S // TQA
    cq, ck, cv = C_SBQ // LANES, C_SBK // LANES, C_SBV // LANES
    return pl.pallas_call(
        _sb_kernel,
        grid=(B, H_SB // 2, nq),
        in_specs=[pl.BlockSpec((TQA, LANES), lambda b, p, i: (b * nq + i, cq + p)),
                  pl.BlockSpec((S, LANES), lambda b, p, i: (b, ck + p)),
                  pl.BlockSpec((S, LANES), lambda b, p, i: (b, cv + p)),
                  pl.BlockSpec((2 * LANES, 2 * LANES), lambda b, p, i: (0, 0))],
        out_specs=pl.BlockSpec((TQA, LANES), lambda b, p, i: (b * nq + i, p)),
        out_shape=jax.ShapeDtypeStruct((B * S, W_HEADS), BF16),
        scratch_shapes=[pltpu.VMEM((2 * TQA, LANES), F32), pltpu.VMEM((2 * TQA, LANES), F32)],
        compiler_params=_cparams(("parallel", "parallel", "arbitrary")),
    )(proj, proj, proj, u2)


def _fox_kernel(q_ref, k_ref, v_ref, fa_ref, fb_ref, o_ref, acc_ref, m_ref, l_ref):
    i = pl.program_id(2)
    rt = 2 * TQA
    low = lax.broadcasted_iota(jnp.int32, (TQA, LANES), 1) < HEAD_DIM
    qs = _stack_pair(q_ref[...] * SCALE, low)
    qa =jnp.concatenate([qs, jnp.concatenate([fa_ref[:, :LANES], fa_ref[:, LANES:]], axis=0)], axis=1)
    ones = jnp.ones((TQA, LANES), BF16)
    acc_ref[...] = jnp.zeros_like(acc_ref)
    l_ref[...] = jnp.zeros_like(l_ref)
    m_ref[...] = jnp.full(m_ref.shape, NEG, F32)

    def tile(kb, on_diagonal):
        off = pl.multiple_of(kb * TQA, TQA)
        kbias = jnp.concatenate([k_ref[pl.ds(off, TQA), :].astype(BF16), fb_ref[pl.ds(off, TQA), :]], axis=1)
        v = v_ref[pl.ds(off, TQA), :].astype(BF16)
        s = _dot_nt(qa, kbias)
        if on_diagonal:
            causal = lax.broadcasted_iota(jnp.int32, (rt, TQA), 1) <= (
                lax.broadcasted_iota(jnp.int32, (rt, TQA), 0) & (TQA - 1))
            s = jnp.where(causal, s, NEG)
        m_old = m_ref[...]
        m_new = jnp.maximum(m_old, jnp.max(s, axis=-1, keepdims=True))
        p = jnp.exp(s - m_new)
        if on_diagonal:
            p = jnp.where(causal, p, 0.0)
        p = p.astype(BF16)
        alpha = jnp.exp(m_old - m_new)
        m_ref[...] = m_new
        l_ref[...] = alpha * l_ref[...] + _dot(p, ones)
        acc_ref[...] = alpha * acc_ref[...] + _dot(p, v)

    tile(i, True)

    def body(j, carry):
        tile(i - 1 - j, False)
        return carry

    lax.fori_loop(0, i, body, 0)
    l = l_ref[...]
    o = acc_ref[...] / jnp.where(l > 0.0, l, 1.0)
    o_ref[...] = jnp.where(low, o[:TQA], o[TQA:]).astype(o_ref.dtype)


def _fox_attention(proj, fox_a, fox_b, B, S):
    nq = S // TQA
    cq, ck, cv = C_FQ // LANES, C_FK // LANES, C_FV // LANES
    return pl.pallas_call(
        _fox_kernel,
        grid=(B, H_FOX // 2, nq),
        in_specs=[pl.BlockSpec((TQA, LANES), lambda b, p, i: (b * nq + i, cq + p)),
                  pl.BlockSpec((S, LANES), lambda b, p, i: (b, ck + p)),
                  pl.BlockSpec((S, LANES), lambda b, p, i: (b, cv + p)),
                  pl.BlockSpec((TQA, 2 * LANES), lambda b, p, i: (b * nq + i, p)),
                  pl.BlockSpec((S, LANES), lambda b, p, i: (b, p))],
        out_specs=pl.BlockSpec((TQA, LANES), lambda b, p, i: (b * nq + i, p)),
        out_shape=jax.ShapeDtypeStruct((B * S, W_HEADS), BF16),
        scratch_shapes=[pltpu.VMEM((2 * TQA, LANES), F32), pltpu.VMEM((2 * TQA, 1), F32),
                        pltpu.VMEM((2 * TQA, LANES), F32)],
        compiler_params=_cparams(("parallel", "parallel", "arbitrary")),
    )(proj, proj, proj, fox_a, fox_b)


def _rope(x, cos, s_up, s_dn):
    return x * cos + pltpu.roll(x, ROPE_DIM // 2, axis=1) * s_up + pltpu.roll(x, LANES - ROPE_DIM // 2, axis=1) * s_dn


def _prep_kernel(misc_ref, bias_ref, pos_ref, rc_ref, q_ref, ks_ref, kw_ref, pa_ref, pb_ref, oa_ref, ob_ref,
                 gates_ref, fa_ref, fb_ref, qr_ref, ksr_ref, kwr_ref, carry_ref, *, ts):
    @pl.when(pl.program_id(1) == 0)
    def _():
        carry_ref[...] = jnp.zeros_like(carry_ref)

    logits = misc_ref[...]
    gates_ref[...] = jax.nn.sigmoid(logits)
    zf = logits + bias_ref[...]
    x = jnp.minimum(zf, 0.0) - jnp.log(1.0 + jnp.exp(-jnp.abs(zf)))
    row = lax.broadcasted_iota(jnp.int32, (ts, LANES), 0)
    sh = 1
    while sh < ts:
        x = x + jnp.where(row >= sh, pltpu.roll(x, sh, axis=0), 0.0)
        sh *= 2
    x = x + carry_ref[...]
    carry_ref[...] = x[ts - 1:ts, :]
    hi = x.astype(BF16)
    r1 = x - hi.astype(F32)
    mid = r1.astype(BF16)
    lo = (r1 - mid.astype(F32)).astype(BF16)
    parts = jnp.concatenate([hi, mid, lo], axis=1)
    fa_ref[...] = (_dot(parts, pa_ref[...]) + oa_ref[...]).astype(BF16)
    fb_ref[...] = (_dot(parts, pb_ref[...]) + ob_ref[...]).astype(BF16)

    ang = pos_ref[0].astype(F32) * rc_ref[0:1, :]
    cos = jnp.cos(ang)
    sin = jnp.sin(ang)
    s_up = sin * rc_ref[1:2, :]
    s_dn = sin * rc_ref[2:3, :]
    for a in range(W_HEADS // LANES):
        qa = q_ref[:, a * LANES:(a + 1) * LANES]
        qr_ref[:, a * LANES:(a + 1) * LANES] = _rope(qa, cos, s_up, s_dn) * SCALE
    ksr_ref[...] = _rope(ks_ref[...], cos, s_up, s_dn).astype(BF16)
    kwr_ref[...] = _rope(kw_ref[...], cos, s_up, s_dn).astype(BF16)


def _prep(proj, bias_row, pos3, rope_c, place, B, S):
    ts = min(512, S)
    nt = S // ts
    T = B * S
    row_blk = lambda w: pl.BlockSpec((ts, w), lambda b, t: (b * nt + t, 0))
    full = lambda a: pl.BlockSpec(a.shape, lambda b, t: (0, 0))
    wa, wb = place[0].shape[1], place[1].shape[1]
    return pl.pallas_call(
        functools.partial(_prep_kernel, ts=ts),
        grid=(B, nt),
        in_specs=[pl.BlockSpec((ts, LANES), lambda b, t: (b * nt + t, C_MISC // LANES)),
                  pl.BlockSpec((1, LANES), lambda b, t: (0, 0)),
                  pl.BlockSpec((1, ts, 1), lambda b, t: (b, t, 0)),
                  pl.BlockSpec((8, LANES), lambda b, t: (0, 0)),
                  pl.BlockSpec((ts, W_HEADS), lambda b, t: (b * nt + t, C_NQ // W_HEADS)),
                  pl.BlockSpec((ts, LANES), lambda b, t: (b * nt + t, C_NKV // LANES + 2)),
                  pl.BlockSpec((ts, LANES), lambda b, t: (b * nt + t, C_NKV // LANES + 4)),
                  full(place[0]), full(place[1]), full(place[2]), full(place[3])],
        out_specs=[row_blk(LANES), row_blk(wa), row_blk(wb), row_blk(W_HEADS), row_blk(LANES), row_blk(LANES)],
        out_shape=[jax.ShapeDtypeStruct((T, LANES), F32),
                   jax.ShapeDtypeStruct((T, wa), BF16),
                   jax.ShapeDtypeStruct((T, wb), BF16),
                   jax.ShapeDtypeStruct((T, W_HEADS), F32),
                   jax.ShapeDtypeStruct((T, LANES), BF16),
                   jax.ShapeDtypeStruct((T, LANES), BF16)],
        scratch_shapes=[pltpu.VMEM((1, LANES), F32)],
        compiler_params=_cparams(("parallel", "arbitrary")),
    )(proj, bias_row, pos3, rope_c, proj, proj, proj, *place)


def _compress_kernel(xk_ref, xv_ref, pek_ref, w1k_ref, b1k_ref, w2k_ref,
                     pev_ref, w1v_ref, b1v_ref, w2v_ref, ok_ref, ov_ref, *, nc):
    half = CMP_STRIDE * HEAD_DIM
    for x_ref, pe_ref, w1_ref, b1_ref, w2_ref, o_ref in (
            (xk_ref, pek_ref, w1k_ref, b1k_ref, w2k_ref, ok_ref),
            (xv_ref, pev_ref, w1v_ref, b1v_ref, w2v_ref, ov_ref)):
        outs = []
        for g in range(NSA_KV_HEADS):
            x = x_ref[0, g]
            h_first = _dot((x + pe_ref[0:1, :]).astype(BF16), w1_ref[0:half, :])
            h_second = _dot((x + pe_ref[1:2, :]).astype(BF16), w1_ref[half:2 * half, :])
            h = h_first + pltpu.roll(h_second, nc - 1, axis=0) + b1_ref[...]
            a = h * jax.nn.sigmoid(h)
            outs.append(_dot(a.astype(BF16), w2_ref[...]))
        o_ref[0] = jnp.concatenate(outs, axis=1).astype(o_ref.dtype)


def _compress(xk, xv, pek, w1k, b1k, w2k, pev, w1v, b1v, w2v, B, S):
    nc = S // CMP_STRIDE
    G = NSA_KV_HEADS
    x_spec = pl.BlockSpec((1, G, nc, CMP_STRIDE * HEAD_DIM), lambda b: (b, 0, 0, 0))
    full = lambda a: pl.BlockSpec(a.shape, lambda b: (0,) * a.ndim)
    out_spec = pl.BlockSpec((1, nc, LANES), lambda b: (b, 0, 0))
    return pl.pallas_call(
        functools.partial(_compress_kernel, nc=nc),
        grid=(B,),
        in_specs=[x_spec, x_spec, full(pek), full(w1k), full(b1k), full(w2k),
                  full(pev), full(w1v), full(b1v), full(w2v)],
        out_specs=[out_spec, out_spec],
        out_shape=[jax.ShapeDtypeStruct((B, nc, LANES), BF16)] * 2,
        compiler_params=_cparams(("parallel",)),
    )(xk, xv, pek, w1k, b1k, w2k, pev, w1v, b1v, w2v)


def _nsa_kernel(qn_ref, qr_ref, kc_ref, vc_ref, ks_ref, vs_ref, kw_ref, vw_ref, gt_ref, ot_ref,
                o_ref, *, S):
    i = pl.program_id(1)
    nc = S // CMP_STRIDE
    ns = S // SEL_BLOCK
    k_top = min(SEL_TOPK, ns)
    R = H_NSA // NSA_KV_HEADS
    RT = R * TQ
    lane = lax.broadcasted_iota(jnp.int32, (TQ, LANES), 1)
    row = lax.broadcasted_iota(jnp.int32, (TQ, LANES), 0)
    t_q = row + i * TQ
    low = lane < HEAD_DIM
    gates = gt_ref[...]
    qn = qn_ref[...] * SCALE
    qr = qr_ref[...]
    n_sub = lax.broadcasted_iota(jnp.int32, (LANES, LANES), 0)
    s_lane = lax.broadcasted_iota(jnp.int32, (LANES, LANES), 1)

    def rep(x):
        return jnp.concatenate([x] * R, axis=0)

    def online_softmax_step(carry, s, mask, v):
        m, l, acc = carry
        s = jnp.where(mask, s, NEG)
        m_new = jnp.maximum(m, jnp.max(s, axis=-1, keepdims=True))
        p = jnp.where(mask, jnp.exp(s - m_new), 0.0)
        alpha = jnp.exp(m - m_new)
        l = alpha * l + jnp.sum(p, axis=-1, keepdims=True)
        acc = alpha * acc + _dot(p.astype(BF16), v)
        return m_new, l, acc

    init = (jnp.full((RT, 1), NEG, F32), jnp.zeros((RT, 1), F32), jnp.zeros((RT, LANES), F32))
    heads = [None] * H_NSA
    for g in range(NSA_KV_HEADS):
        in_g = low if g == 0 else jnp.logical_not(low)

        def stack(qfull, g=g, in_g=in_g):
            parts = []
            for r in range(R):
                a, b = divmod(R * g + r, 2)
                blk = qfull[:, a * LANES:(a + 1) * LANES]
                if b != g:
                    blk = pltpu.roll(blk, HEAD_DIM, axis=1)
                parts.append(jnp.where(in_g, blk, 0.0))
            return jnp.concatenate(parts, axis=0).astype(BF16)

        sc = _dot_nt(stack(qn), kc_ref[0])
        c_idx = lax.broadcasted_iota(jnp.int32, (RT, nc), 1)
        t_c = (lax.broadcasted_iota(jnp.int32, (RT, nc), 0) & (TQ - 1)) + i * TQ
        c_valid = c_idx * CMP_STRIDE + (CMP_LEN - 1) <= t_c
        sc = jnp.where(c_valid, sc, NEG)
        e = jnp.where(c_valid, jnp.exp(sc - jnp.max(sc, axis=-1, keepdims=True)), 0.0)
        den = jnp.sum(e, axis=-1, keepdims=True)
        p_cmp = e / jnp.where(den > 0.0, den, 1.0)
        o_cmp = _dot(p_cmp.astype(BF16), vc_ref[0])

        p_sum = p_cmp[0:TQ]
        for r in range(1, R):
            p_sum = p_sum + p_cmp[r * TQ:(r + 1) * TQ]
        p_slc = _dot_nt(ot_ref[...], _split_bf16(p_sum))
        n_idx = lax.broadcasted_iota(jnp.int32, (ns, TQ), 0)
        t_s = lax.broadcasted_iota(jnp.int32, (ns, TQ), 1) + i * TQ
        forced = jnp.logical_or(n_idx == 0, n_idx == (t_s >> SEL_SHIFT))
        score = jnp.where(forced, 1e4, jnp.where(n_idx * SEL_BLOCK <= t_s, p_slc, -1.0))
        rank = jnp.zeros((ns, TQ), F32)
        for m_i in range(ns):
            sm = score[m_i:m_i + 1, :]
            ge = jnp.where(sm >= score, 1.0, 0.0)
            gt = jnp.where(sm > score, 1.0, 0.0)
            rank = rank + jnp.where(n_idx > m_i, ge, gt)
        sel_t = jnp.where(rank < k_top, 1.0, 0.0)
        if ns < LANES:
            sel_t = jnp.concatenate([sel_t, jnp.zeros((LANES - ns, TQ), F32)], axis=0)
        sel = sel_t.T.astype(BF16)

        qs = stack(qr)

        def sel_body(j, carry, qs=qs, sel=sel):
            kb = i - j
            off = pl.multiple_of(kb * LANES, LANES)
            expand = jnp.where(((s_lane + off) >> SEL_SHIFT) == n_sub, 1.0, 0.0).astype(BF16)
            chosen = _dot(sel, expand)
            mask = rep(jnp.logical_and(chosen > 0.5, (lane + off) <= t_q))
            s = _dot_nt(qs, ks_ref[pl.ds(off, LANES), :])
            return online_softmax_step(carry, s, mask, vs_ref[pl.ds(off, LANES), :].astype(BF16))

        _, l, acc = lax.fori_loop(0, i + 1, sel_body, init)
        o_sel = acc / jnp.where(l > 0.0, l, 1.0)

        def win_body(j, carry, qs=qs):
            kb = i - j
            off = pl.multiple_of(kb * LANES, LANES)
            s_pos = lane + off
            mask = rep(jnp.logical_and(s_pos <= t_q, t_q - s_pos < WINDOW))
            s = _dot_nt(qs, kw_ref[pl.ds(off, LANES), :])
            return online_softmax_step(carry, s, mask, vw_ref[pl.ds(off, LANES), :].astype(BF16))

        n_win = jnp.minimum(i, (WINDOW + TQ - 2) // LANES) + 1
        _, l, acc = lax.fori_loop(0, n_win, win_body, init)
        o_win = acc / jnp.where(l > 0.0, l, 1.0)

        for r in range(R):
            h = R * g + r
            rows = slice(r * TQ, (r + 1) * TQ)
            o = (gates[:, 3 * h:3 * h + 1] * o_cmp[rows] + gates[:, 3 * h + 1:3 * h + 2] * o_sel[rows]
                 + gates[:, 3 * h + 2:3 * h + 3] * o_win[rows])
            if h % 2 != g:
                o = pltpu.roll(o, HEAD_DIM, axis=1)
            heads[h] = o
    for a in range(H_NSA // 2):
        o_ref[:, a * LANES:(a + 1) * LANES] = jnp.where(low, heads[2 * a], heads[2 * a + 1]).astype(o_ref.dtype)


def _nsa_attention(proj, q_rope, k_cmp, v_cmp, ks_r, kw_r, gates, overlap_t, B, S):
    nq = S // TQ
    nc = S // CMP_STRIDE
    ckv = C_NKV // LANES
    kv_full = lambda c: pl.BlockSpec((S, LANES), lambda b, i: (b, c))
    return pl.pallas_call(
        functools.partial(_nsa_kernel, S=S),
        grid=(B, nq),
        in_specs=[pl.BlockSpec((TQ, W_HEADS), lambda b, i: (b * nq + i, C_NQ // W_HEADS)),
                  pl.BlockSpec((TQ, W_HEADS), lambda b, i: (b * nq + i, 0)),
                  pl.BlockSpec((1, nc, LANES), lambda b, i: (b, 0, 0)),
                  pl.BlockSpec((1, nc, LANES), lambda b, i: (b, 0, 0)),
                  kv_full(0), kv_full(ckv + 3), kv_full(0), kv_full(ckv + 5),
                  pl.BlockSpec((TQ, LANES), lambda b, i: (b * nq + i, 0)),
                  pl.BlockSpec(overlap_t.shape, lambda b, i: (0, 0))],
        out_specs=pl.BlockSpec((TQ, W_HEADS), lambda b, i: (b * nq + i, 0)),
        out_shape=jax.ShapeDtypeStruct((B * S, W_HEADS), BF16),
        compiler_params=_cparams(("parallel", "arbitrary")),
    )(proj, q_rope, k_cmp, v_cmp, ks_r, proj, kw_r, proj, gates, overlap_t)


def _merge_kernel(x_ref, osb_ref, onsa_ref, ofox_ref, m0_ref, m1_ref, m2_ref,
                  wsb_ref, wnsa_ref, wfox_ref, wout_ref, g_ref, o_ref):
    y = jax.nn.sigmoid(m0_ref[...]) * _dot(osb_ref[...], wsb_ref[...])
    y = y + jax.nn.sigmoid(m1_ref[...]) * _dot(onsa_ref[...], wnsa_ref[...])
    y = y + jax.nn.sigmoid(m2_ref[...]) * _dot(ofox_ref[...], wfox_ref[...])
    z = _dot(y.astype(BF16), wout_ref[...])
    o_ref[...] = x_ref[...] + _rms(z, g_ref[...])


def _merge(x, o_sb, o_nsa, o_fox, proj, w_sb, w_nsa, w_fox, w_out, g, tm):
    T, D = x.shape
    row = lambda w: pl.BlockSpec((tm, w), lambda i: (i, 0))
    full = lambda a: pl.BlockSpec(a.shape, lambda i: (0, 0))
    gate = lambda c: pl.BlockSpec((tm, D), lambda i: (i, C_MERGE // D + c))
    return pl.pallas_call(
        _merge_kernel,
        grid=(T // tm,),
        in_specs=[row(D), row(W_HEADS), row(W_HEADS), row(W_HEADS), gate(0), gate(1), gate(2),
                  full(w_sb), full(w_nsa), full(w_fox), full(w_out), full(g)],
        out_specs=row(D),
        out_shape=jax.ShapeDtypeStruct((T, D), F32),
        compiler_params=_cparams(("parallel",)),
    )(x, o_sb, o_nsa, o_fox, proj, proj, proj, w_sb, w_nsa, w_fox, w_out, g)


def _mem_kv_kernel(mem_ref, g_ref, wk_ref, wv_ref, k_ref, v_ref):
    mn = _rms(mem_ref[0], g_ref[...]).astype(BF16)
    k_ref[0] = _dot(mn, wk_ref[...]).astype(BF16)
    v_ref[0] = _dot(mn, wv_ref[...]).astype(BF16)


def _mem_kv(mem, g, wk, wv):
    B, M, D = mem.shape
    full = lambda a: pl.BlockSpec(a.shape, lambda b: (0, 0))
    out_spec = pl.BlockSpec((1, M, W_MEM), lambda b: (b, 0, 0))
    return pl.pallas_call(
        _mem_kv_kernel,
        grid=(B,),
        in_specs=[pl.BlockSpec((1, M, D), lambda b: (b, 0, 0)), full(g), full(wk), full(wv)],
        out_specs=[out_spec, out_spec],
        out_shape=[jax.ShapeDtypeStruct((B, M, W_MEM), BF16)] * 2,
        compiler_params=_cparams(("parallel",)),
    )(mem, g, wk, wv)


def _mem_attn_kernel(x_ref, g_ref, wq_ref, k_ref, v_ref, wo_ref, gp_ref, o_ref):
    x = x_ref[...]
    q = _dot(_rms(x, g_ref[...]).astype(BF16), wq_ref[...]) * SCALE
    k = k_ref[0]
    v = v_ref[0]
    lane = lax.broadcasted_iota(jnp.int32, q.shape, 1)
    o = jnp.zeros(q.shape, F32)
    for h in range(H_MEM):
        in_h = jnp.logical_and(lane >= h * HEAD_DIM, lane < (h + 1) * HEAD_DIM)
        s = _dot_nt(jnp.where(in_h, q, 0.0).astype(BF16), k)
        e = jnp.exp(s - jnp.max(s, axis=-1, keepdims=True))
        p = e / jnp.sum(e, axis=-1, keepdims=True)
        o = o + jnp.where(in_h, _dot(p.astype(BF16), v), 0.0)
    y = _dot(o.astype(BF16), wo_ref[...])
    o_ref[...] = x + _rms(y, gp_ref[...])


def _mem_attn(x, g_pre, wq, k, v, wo, g_post, B, S, tm):
    T, D = x.shape
    M = k.shape[1]
    nt = S // tm
    full = lambda a: pl.BlockSpec(a.shape, lambda b, i: (0, 0))
    kv_spec = pl.BlockSpec((1, M, W_MEM), lambda b, i: (b, 0, 0))
    row = pl.BlockSpec((tm, D), lambda b, i: (b * nt + i, 0))
    return pl.pallas_call(
        _mem_attn_kernel,
        grid=(B, nt),
        in_specs=[row, full(g_pre), full(wq), kv_spec, kv_spec, full(wo), full(g_post)],
        out_specs=row,
        out_shape=jax.ShapeDtypeStruct((T, D), F32),
        compiler_params=_cparams(("parallel", "parallel")),
    )(x, g_pre, wq, k, v, wo, g_post)


def _ffn_kernel(x_ref, g_ref, wg_ref, wu_ref, wd_ref, gp_ref, o_ref, h_ref, acc_ref):
    j = pl.program_id(1)

    @pl.when(j == 0)
    def _():
        h_ref[...] = _rms(x_ref[...], g_ref[...]).astype(BF16)
        acc_ref[...] = jnp.zeros_like(acc_ref)

    h = h_ref[...]
    a = _dot(h, wg_ref[...])
    u = _dot(h, wu_ref[...])
    acc_ref[...] += _dot((a * jax.nn.sigmoid(a) * u).astype(BF16), wd_ref[...])

    @pl.when(j == pl.num_programs(1) - 1)
    def _():
        o_ref[...] = x_ref[...] + _rms(acc_ref[...], gp_ref[...])


def _ffn(x, g_pre, wg, wu, wd, g_post, tm, tf):
    T, D = x.shape
    F = wg.shape[1]
    row = pl.BlockSpec((tm, D), lambda i, j: (i, 0))
    vec = pl.BlockSpec((1, D), lambda i, j: (0, 0))
    return pl.pallas_call(
        _ffn_kernel,
        grid=(T // tm, F // tf),
        in_specs=[row, vec,
                  pl.BlockSpec((D, tf), lambda i, j: (0, j)),
                  pl.BlockSpec((D, tf), lambda i, j: (0, j)),
                  pl.BlockSpec((tf, D), lambda i, j: (j, 0)),
                  vec],
        out_specs=row,
        out_shape=jax.ShapeDtypeStruct((T, D), F32),
        scratch_shapes=[pltpu.VMEM((tm, D), BF16), pltpu.VMEM((tm, D), F32)],
        compiler_params=_cparams(("parallel", "arbitrary")),
    )(x, g_pre, wg, wu, wd, g_post)


def _scan_matrix():
    j = np.arange(LANES)
    later = (j[:, None] > j[None, :]).astype(np.float32)
    u = np.concatenate([later, np.ones((LANES, LANES), np.float32)], axis=1)
    return jnp.asarray(np.concatenate([u, u], axis=0), dtype=BF16)


def _overlap_t(S):
    nc, ns = S // CMP_STRIDE, S // SEL_BLOCK
    c0 = np.arange(nc) * CMP_STRIDE
    n0 = np.arange(ns) * SEL_BLOCK
    ov = (c0[None, :] < n0[:, None] + SEL_BLOCK) & (c0[None, :] + CMP_LEN > n0[:, None])
    ov = ov & (np.arange(nc)[None, :] < nc - 1)
    ov = ov.astype(np.float32)
    return jnp.asarray(np.concatenate([ov, ov], axis=1), dtype=BF16)


FOX_F_LANE = 24
N_PARTS = 3


def _fox_bias_placement():
    n_pair = H_FOX // 2
    pa = np.zeros((N_PARTS * LANES, n_pair * 2 * LANES), np.float32)
    pb = np.zeros((N_PARTS * LANES, n_pair * LANES), np.float32)
    oa = np.zeros((1, n_pair * 2 * LANES), np.float32)
    ob = np.zeros((1, n_pair * LANES), np.float32)
    for p in range(n_pair):
        for e in range(2):
            src = FOX_F_LANE + 2 * p + e
            for x in range(N_PARTS):
                pa[x * LANES + src, (2 * p + e) * LANES + 8 * e + x] = 1.0
                oa[0, (2 * p + e) * LANES + 8 * e + N_PARTS + x] = 1.0
                pb[x * LANES + src, p * LANES + 8 * e + N_PARTS + x] = -1.0
                ob[0, p * LANES + 8 * e + x] = 1.0
    return (jnp.asarray(pa, dtype=BF16), jnp.asarray(pb, dtype=BF16), jnp.asarray(oa), jnp.asarray(ob))


def _rope_rows():
    half = ROPE_DIM // 2
    inv_freq = ROPE_THETA ** (-jnp.arange(half, dtype=F32) / half)
    d = np.arange(LANES) % HEAD_DIM
    rot = d < ROPE_DIM
    freq = jnp.where(jnp.asarray(rot), inv_freq[jnp.asarray(d % half)], 0.0)
    s_up = jnp.asarray(((d >= half) & rot).astype(np.float32))
    s_dn = jnp.asarray(-(d < half).astype(np.float32))
    rows = jnp.stack([freq, s_up, s_dn] + [jnp.zeros((LANES,), F32)] * 5)
    return rows.astype(F32)


def _reorder_w_in(w):
    pad = jnp.zeros((w.shape[0], MISC_W - (_O_FOX_Q - _O_NSA_G) - (_O_MERGE - _O_FOX_F)), w.dtype)
    return jnp.concatenate([
        w[:, _O_MERGE:_O_END],
        w[:, :_O_NSA_G - 6 * LANES],
        w[:, _O_FOX_Q:_O_FOX_F],
        w[:, _O_NSA_G - 6 * LANES:_O_NSA_G],
        w[:, _O_NSA_G:_O_FOX_Q],
        w[:, _O_FOX_F:_O_MERGE],
        pad], axis=1).astype(BF16)


def _row_tile(n, target):
    t = min(n, target)
    while n % t:
        t //= 2
    return t


def kernel(x, mem, positions, g_pre_mix, g_post_mix, g_pre_mem, g_mem, g_post_mem, g_pre_ffn, g_post_ffn,
           w_in, b_fox_f, cmp_pe_k, cmp_w1_k, cmp_b1_k, cmp_w2_k, cmp_pe_v, cmp_w1_v, cmp_b1_v, cmp_w2_v,
           w_up_sb, w_up_nsa, w_up_fox, w_out, w_mem_q, w_mem_k, w_mem_v, w_mem_o,
           w_ffn_gate, w_ffn_up, w_ffn_down):
    B, S, D = x.shape
    T = B * S
    depth = w_in.shape[0]
    G = NSA_KV_HEADS
    nc = S // CMP_STRIDE
    u2 = _scan_matrix()
    overlap_t = _overlap_t(S)
    rope_c = _rope_rows()
    place = _fox_bias_placement()
    pos3 =positions.reshape(B, S, 1)
    vec = lambda g: g.reshape(1, -1)
    pe_rows = lambda pe: pe.reshape(2, CMP_STRIDE * HEAD_DIM)
    tm_big = _row_tile(T, 1024)
    tm_mid = _row_tile(T, 512)
    d_ff = w_ffn_gate.shape[2]
    tf = d_ff // 2 if (d_ff // 2) % LANES == 0 else d_ff

    xf = x.reshape(T, D)
    for l in range(depth):
        proj = _norm_matmul(xf, vec(g_pre_mix[l]), _reorder_w_in(w_in[l]), tm_big, N_IN // 5)

        o_sb = _sb_attention(proj, u2, B, S)

        bias_row = jnp.zeros((1, LANES), F32).at[0, FOX_F_LANE:FOX_F_LANE + H_FOX].set(b_fox_f[l])
        gates, fox_a, fox_b, q_rope, ks_r, kw_r = _prep(proj, bias_row, pos3, rope_c, place, B, S)
        o_fox = _fox_attention(proj, fox_a, fox_b, B, S)

        kv_c = proj[:, C_NKV:C_NKV + 2 * LANES].reshape(B, nc, CMP_STRIDE, 2, G, HEAD_DIM)
        kv_c = kv_c.transpose(3, 0, 4, 1, 2, 5).reshape(2, B, G, nc, CMP_STRIDE * HEAD_DIM)
        k_cmp, v_cmp = _compress(
            kv_c[0], kv_c[1],
            pe_rows(cmp_pe_k[l]), cmp_w1_k[l].astype(BF16), vec(cmp_b1_k[l]), cmp_w2_k[l].astype(BF16),
            pe_rows(cmp_pe_v[l]), cmp_w1_v[l].astype(BF16), vec(cmp_b1_v[l]), cmp_w2_v[l].astype(BF16), B, S)
        o_nsa = _nsa_attention(proj, q_rope, k_cmp, v_cmp, ks_r, kw_r, gates, overlap_t, B, S)

        xf = _merge(xf, o_sb, o_nsa, o_fox, proj, w_up_sb[l].astype(BF16), w_up_nsa[l].astype(BF16),
                    w_up_fox[l].astype(BF16), w_out[l].astype(BF16), vec(g_post_mix[l]), tm_mid)

        k_mem, v_mem = _mem_kv(mem, vec(g_mem[l]), w_mem_k[l].astype(BF16), w_mem_v[l].astype(BF16))
        xf = _mem_attn(xf, vec(g_pre_mem[l]), w_mem_q[l].astype(BF16), k_mem, v_mem,
                       w_mem_o[l].astype(BF16), vec(g_post_mem[l]), B, S, _row_tile(S, 512))

        xf = _ffn(xf, vec(g_pre_ffn[l]), w_ffn_gate[l].astype(BF16), w_ffn_up[l].astype(BF16),
                  w_ffn_down[l].astype(BF16), vec(g_post_ffn[l]), tm_mid, tf)
    return xf.reshape(B, S, D)
```

```python
import functools

import numpy as np
import jax
import jax.numpy as jnp
from jax import lax
from jax.experimental import pallas as pl
from jax.experimental.pallas import tpu as pltpu

F32 = jnp.float32
BF16 = jnp.bfloat16

D_MODEL = 1024
HEAD_DIM = 64
H_SB = 8
H_NSA = 8
NSA_KV_HEADS = 2
H_FOX = 8
H_MEM = 4
N_BRANCH = 3
ROPE_THETA = 500000.0
ROPE_DIM = HEAD_DIM // 4
CMP_STRIDE = 16
CMP_LEN = 2 * CMP_STRIDE
CMP_HIDDEN = 256
SEL_BLOCK = 64
SEL_SHIFT = SEL_BLOCK.bit_length() - 1
SEL_TOPK = 8
WINDOW = 512
W_HEADS = 8 * HEAD_DIM
W_MEM = H_MEM * HEAD_DIM
EPS = 1e-6
SCALE = HEAD_DIM ** -0.5
NEG = -1e30

LANES = 128
TQ = 128
TQA = 256

C_MERGE = 0
C_SBQ, C_SBK, C_SBV = 3072, 3584, 4096
C_NQ = 4608
C_FQ, C_FK, C_FV = 5120, 5632, 6144
C_NKV = 6656
C_MISC = 7424
MISC_W = 256
N_IN = 7680
_O_NSA_G, _O_FOX_Q, _O_FOX_F, _O_MERGE, _O_END = 2816, 2840, 4376, 4384, 7456

VMEM_LIMIT = 56 * 1024 * 1024


def _cparams(sem):
    return pltpu.CompilerParams(dimension_semantics=sem, vmem_limit_bytes=VMEM_LIMIT)


def _dot(a, b):
    return jnp.dot(a, b, preferred_element_type=F32)


def _dot_nt(a, b):
    return lax.dot_general(a, b, (((1,), (1,)), ((), ())), preferred_element_type=F32)


def _rms(x, g):
    ms = jnp.mean(x * x, axis=-1, keepdims=True)
    return x * lax.rsqrt(ms + EPS) * g


def _split_bf16(x, axis=1):
    hi = x.astype(BF16)
    lo = (x - hi.astype(F32)).astype(BF16)
    return jnp.concatenate([hi, lo], axis=axis)


def _norm_matmul_kernel(x_ref, g_ref, w_ref, o_ref, h_ref):
    @pl.when(pl.program_id(1) == 0)
    def _():
        h_ref[...] = _rms(x_ref[...], g_ref[...]).astype(BF16)

    o_ref[...] = _dot(h_ref[...], w_ref[...])


def _norm_matmul(x, g, w, tm, tn):
    T, D = x.shape
    N = w.shape[1]
    return pl.pallas_call(
        _norm_matmul_kernel,
        grid=(T // tm, N // tn),
        in_specs=[pl.BlockSpec((tm, D), lambda i, j: (i, 0)),
                  pl.BlockSpec((1, D), lambda i, j: (0, 0)),
                  pl.BlockSpec((D, tn), lambda i, j: (0, j))],
        out_specs=pl.BlockSpec((tm, tn), lambda i, j: (i, j)),
        out_shape=jax.ShapeDtypeStruct((T, N), F32),
        scratch_shapes=[pltpu.VMEM((tm, D), BF16)],
        compiler_params=_cparams(("parallel", "arbitrary")),
    )(x, g, w)


def _stack_pair(q, low):
    return jnp.concatenate([jnp.where(low, q, 0.0), jnp.where(low, 0.0, q)], axis=0).astype(BF16)


def _sb_kernel(q_ref, k_ref, v_ref, u_ref, o_ref, acc_ref, cs_ref, arg_ref, rs_ref):
    i = pl.program_id(2)
    rt = 2 * TQA
    n_grp = TQA // LANES
    lane = lax.broadcasted_iota(jnp.int32, (rt, LANES), 1)
    t_q = (lax.broadcasted_iota(jnp.int32, (rt, LANES), 0) & (TQA - 1)) + i * TQA
    low = lax.broadcasted_iota(jnp.int32, (TQA, LANES), 1) < HEAD_DIM
    qs = _stack_pair(q_ref[...] * SCALE, low)

    def offsets(kb_first):
        return [pl.multiple_of((kb_first - d) * LANES, LANES) for d in range(n_grp)]

    def logits(offs):
        return [_dot_nt(qs, k_ref[pl.ds(off, LANES), :].astype(BF16)) for off in offs]

    def scan(offs, zs, on_diagonal):
        log_keeps, log_betas, stricts = [], [], []
        for off, z in zip(offs, zs):
            log_keep = -(jnp.maximum(z, 0.0) + jnp.log(1.0 + jnp.exp(-jnp.abs(z))))
            log_betas.append(z + log_keep)
            if on_diagonal:
                stricts.append((lane + off) < t_q)
                log_keep = jnp.where(stricts[-1], log_keep, 0.0)
            log_keeps.append(log_keep)
        c2s = [_dot(_split_bf16(log_keep), u_ref[...]) for log_keep in log_keeps]
        args = [log_beta + c2[:, :LANES] for log_beta, c2 in zip(log_betas, c2s)]
        if on_diagonal:
            args = [jnp.where(strict, arg, NEG) for strict, arg in zip(stricts, args)]
        return args, [c2[:, LANES:] for c2 in c2s]

    def stash(args, row_sums):
        for d in range(n_grp):
            arg_ref[d] = args[d]
            rs_ref[d] = row_sums[d]

    def weights(offs, args, row_sums):
        cs = cs_ref[...]
        pvs = []
        for d, off in enumerate(offs):
            w = jnp.exp(args[d] + cs)
            pvs.append(_dot(w.astype(BF16), v_ref[pl.ds(off, LANES), :].astype(BF16)))
            cs = cs + row_sums[d]
        return cs, pvs

    def accumulate(cs, pvs):
        acc = acc_ref[...]
        for pv in pvs:
            acc = acc + pv
        acc_ref[...] = acc
        cs_ref[...] = cs

    acc_ref[...] = jnp.zeros_like(acc_ref)
    cs_ref[...] = jnp.zeros_like(cs_ref)
    first = offsets(n_grp * i + n_grp - 1)
    stash(*scan(first, logits(first), True))

    def body(j, carry):
        cur = offsets(n_grp * (i - j) + n_grp - 1)
        nxt = offsets(n_grp * (i - j) - 1)
        args = [arg_ref[d] for d in range(n_grp)]
        row_sums = [rs_ref[d] for d in range(n_grp)]
        zs = logits(nxt)
        cs, pvs = weights(cur, args, row_sums)
        nxt_args, nxt_sums = scan(nxt, zs, False)
        accumulate(cs, pvs)
        stash(nxt_args, nxt_sums)
        return carry

    lax.fori_loop(0, i, body, 0)
    accumulate(*weights(offsets(n_grp - 1), [arg_ref[d] for d in range(n_grp)],
                        [rs_ref[d] for d in range(n_grp)]))
    acc = acc_ref[...]
    o_ref[...] = jnp.where(low, acc[:TQA], acc[TQA:]).astype(o_ref.dtype)


def _sb_attention(proj, u2, B, S):
    nq = S // TQA
    cq, ck, cv = C_SBQ // LANES, C_SBK // LANES, C_SBV // LANES
    return pl.pallas_call(
        _sb_kernel,
        grid=(B, H_SB // 2, nq),
        in_specs=[pl.BlockSpec((TQA, LANES), lambda b, p, i: (b * nq + i, cq + p)),
                  pl.BlockSpec((S, LANES), lambda b, p, i: (b, ck + p)),
                  pl.BlockSpec((S, LANES), lambda b, p, i: (b, cv + p)),
                  pl.BlockSpec((2 * LANES, 2 * LANES), lambda b, p, i: (0, 0))],
        out_specs=pl.BlockSpec((TQA, LANES), lambda b, p, i: (b * nq + i, p)),
        out_shape=jax.ShapeDtypeStruct((B * S, W_HEADS), BF16),
        scratch_shapes=[pltpu.VMEM((2 * TQA, LANES), F32), pltpu.VMEM((2 * TQA, LANES), F32),
                        pltpu.VMEM((TQA // LANES, 2 * TQA, LANES), F32),
                        pltpu.VMEM((TQA // LANES, 2 * TQA, LANES), F32)],
        compiler_params=_cparams(("parallel", "parallel", "arbitrary")),
    )(proj, proj, proj, u2)


def _fox_kernel(q_ref, k_ref, vt_ref, fa_ref, fb_ref, o_ref, acc_ref, m_ref, l_ref, s_ref, cmax_ref):
    i = pl.program_id(2)
    rt = 2 * TQA
    low = lax.broadcasted_iota(jnp.int32, (TQA, LANES), 1) < HEAD_DIM
    qs = _stack_pair(q_ref[...] * SCALE, low)
    qa = jnp.concatenate([qs, jnp.concatenate([fa_ref[:, :LANES], fa_ref[:, LANES:]], axis=0)], axis=1)
    acc_ref[...] = jnp.zeros_like(acc_ref)
    l_ref[...] = jnp.zeros_like(l_ref)
    m_ref[...] = jnp.full(m_ref.shape, NEG, F32)

    def scores(kb, on_diagonal):
        off = pl.multiple_of(kb * TQA, TQA)
        kbias = jnp.concatenate([k_ref[pl.ds(off, TQA), :].astype(BF16), fb_ref[pl.ds(off, TQA), :]], axis=1)
        s = _dot_nt(kbias, qa)
        if on_diagonal:
            causal = lax.broadcasted_iota(jnp.int32, (TQA, rt), 0) <= (
                lax.broadcasted_iota(jnp.int32, (TQA, rt), 1) & (TQA - 1))
            s = jnp.where(causal, s, NEG)
        return s

    def stash(s):
        s_ref[...] = s
        cmax_ref[...] = jnp.max(s, axis=0, keepdims=True)

    def absorb(kb, s, cmax):
        m_old = m_ref[...]
        m_new = jnp.maximum(m_old, cmax)
        p = jnp.exp(s - m_new)
        alpha = jnp.exp(m_old - m_new)
        m_ref[...] = m_new
        l_ref[...] = alpha * l_ref[...] + jnp.sum(p, axis=0, keepdims=True)
        p = p.astype(BF16)
        vt = vt_ref[0, 0, kb]
        return alpha, [_dot(vt[e * HEAD_DIM:(e + 1) * HEAD_DIM, :], p[:, e * TQA:(e + 1) * TQA]) for e in range(2)]

    def accumulate(alpha, pvs):
        for e in range(2):
            acc_ref[e] = alpha[:, e * TQA:(e + 1) * TQA] * acc_ref[e] + pvs[e]

    stash(scores(i, True))

    def body(j, carry):
        kb = i - j
        s_cur, cmax = s_ref[...], cmax_ref[...]
        s_next = scores(kb - 1, False)
        alpha, pvs = absorb(kb, s_cur, cmax)
        stash(s_next)
        accumulate(alpha, pvs)
        return carry

    lax.fori_loop(0, i, body, 0)
    accumulate(*absorb(0, s_ref[...], cmax_ref[...]))
    l = l_ref[...]
    den = jnp.where(l > 0.0, l, 1.0)
    o_t = jnp.concatenate([acc_ref[0] / den[:, :TQA], acc_ref[1] / den[:, TQA:]], axis=0)
    o_ref[...] = o_t.T.astype(o_ref.dtype)


def _fox_attention(proj, v_t, fox_a, fox_b, B, S):
    nq = S // TQA
    cq, ck = C_FQ // LANES, C_FK // LANES
    return pl.pallas_call(
        _fox_kernel,
        grid=(B, H_FOX // 2, nq),
        in_specs=[pl.BlockSpec((TQA, LANES), lambda b, p, i: (b * nq + i, cq + p)),
                  pl.BlockSpec((S, LANES), lambda b, p, i: (b, ck + p)),
                  pl.BlockSpec((1, 1, nq, LANES, TQA), lambda b, p, i: (b, p, 0, 0, 0)),
                  pl.BlockSpec((TQA, 2 * LANES), lambda b, p, i: (b * nq + i, p)),
                  pl.BlockSpec((S, LANES), lambda b, p, i: (b, p))],
        out_specs=pl.BlockSpec((TQA, LANES), lambda b, p, i: (b * nq + i, p)),
        out_shape=jax.ShapeDtypeStruct((B * S, W_HEADS), BF16),
        scratch_shapes=[pltpu.VMEM((2, HEAD_DIM, TQA), F32), pltpu.VMEM((1, 2 * TQA), F32),
                        pltpu.VMEM((1, 2 * TQA), F32), pltpu.VMEM((TQA, 2 * TQA), F32),
                        pltpu.VMEM((1, 2 * TQA), F32)],
        compiler_params=_cparams(("parallel", "parallel", "arbitrary")),
    )(proj, proj, v_t, fox_a, fox_b)


def _rope(x, cos, s_up, s_dn):
    return x * cos + pltpu.roll(x, ROPE_DIM // 2, axis=1) * s_up + pltpu.roll(x, LANES - ROPE_DIM // 2, axis=1) * s_dn


def _prep_kernel(misc_ref, bias_ref, pos_ref, rc_ref, q_ref, ks_ref, kw_ref, pa_ref, pb_ref, oa_ref, ob_ref,
                 gates_ref, fa_ref, fb_ref, qr_ref, ksr_ref, kwr_ref, carry_ref, *, ts):
    @pl.when(pl.program_id(1) == 0)
    def _():
        carry_ref[...] = jnp.zeros_like(carry_ref)

    logits = misc_ref[...]
    gates_ref[...] = jax.nn.sigmoid(logits)
    zf = logits + bias_ref[...]
    x = jnp.minimum(zf, 0.0) - jnp.log(1.0 + jnp.exp(-jnp.abs(zf)))
    row = lax.broadcasted_iota(jnp.int32, (ts, LANES), 0)
    sh = 1
    while sh < ts:
        x = x + jnp.where(row >= sh, pltpu.roll(x, sh, axis=0), 0.0)
        sh *= 2
    x = x + carry_ref[...]
    carry_ref[...] = x[ts - 1:ts, :]
    hi = x.astype(BF16)
    r1 = x - hi.astype(F32)
    mid = r1.astype(BF16)
    lo = (r1 - mid.astype(F32)).astype(BF16)
    parts = jnp.concatenate([hi, mid, lo], axis=1)
    fa_ref[...] = (_dot(parts, pa_ref[...]) + oa_ref[...]).astype(BF16)
    fb_ref[...] = (_dot(parts, pb_ref[...]) + ob_ref[...]).astype(BF16)

    ang = pos_ref[0].astype(F32) * rc_ref[0:1, :]
    cos = jnp.cos(ang)
    sin = jnp.sin(ang)
    s_up = sin * rc_ref[1:2, :]
    s_dn = sin * rc_ref[2:3, :]
    for a in range(W_HEADS // LANES):
        qa = q_ref[:, a * LANES:(a + 1) * LANES]
        qr_ref[:, a * LANES:(a + 1) * LANES] = _rope(qa, cos, s_up, s_dn) * SCALE
    ksr_ref[...] = _rope(ks_ref[...], cos, s_up, s_dn).astype(BF16)
    kwr_ref[...] = _rope(kw_ref[...], cos, s_up, s_dn).astype(BF16)


def _prep(proj, bias_row, pos3, rope_c, place, B, S):
    ts = min(512, S)
    nt = S // ts
    T = B * S
    row_blk = lambda w: pl.BlockSpec((ts, w), lambda b, t: (b * nt + t, 0))
    full = lambda a: pl.BlockSpec(a.shape, lambda b, t: (0, 0))
    wa, wb = place[0].shape[1], place[1].shape[1]
    return pl.pallas_call(
        functools.partial(_prep_kernel, ts=ts),
        grid=(B, nt),
        in_specs=[pl.BlockSpec((ts, LANES), lambda b, t: (b * nt + t, C_MISC // LANES)),
                  pl.BlockSpec((1, LANES), lambda b, t: (0, 0)),
                  pl.BlockSpec((1, ts, 1), lambda b, t: (b, t, 0)),
                  pl.BlockSpec((8, LANES), lambda b, t: (0, 0)),
                  pl.BlockSpec((ts, W_HEADS), lambda b, t: (b * nt + t, C_NQ // W_HEADS)),
                  pl.BlockSpec((ts, LANES), lambda b, t: (b * nt + t, C_NKV // LANES + 2)),
                  pl.BlockSpec((ts, LANES), lambda b, t: (b * nt + t, C_NKV // LANES + 4)),
                  full(place[0]), full(place[1]), full(place[2]), full(place[3])],
        out_specs=[row_blk(LANES), row_blk(wa), row_blk(wb), row_blk(W_HEADS), row_blk(LANES), row_blk(LANES)],
        out_shape=[jax.ShapeDtypeStruct((T, LANES), F32),
                   jax.ShapeDtypeStruct((T, wa), BF16),
                   jax.ShapeDtypeStruct((T, wb), BF16),
                   jax.ShapeDtypeStruct((T, W_HEADS), F32),
                   jax.ShapeDtypeStruct((T, LANES), BF16),
                   jax.ShapeDtypeStruct((T, LANES), BF16)],
        scratch_shapes=[pltpu.VMEM((1, LANES), F32)],
        compiler_params=_cparams(("parallel", "arbitrary")),
    )(proj, bias_row, pos3, rope_c, proj, proj, proj, *place)


def _compress_kernel(xk_ref, xv_ref, pek_ref, w1k_ref, b1k_ref, w2k_ref,
                     pev_ref, w1v_ref, b1v_ref, w2v_ref, ok_ref, ov_ref, *, nc):
    half = CMP_STRIDE * HEAD_DIM
    for x_ref, pe_ref, w1_ref, b1_ref, w2_ref, o_ref in (
            (xk_ref, pek_ref, w1k_ref, b1k_ref, w2k_ref, ok_ref),
            (xv_ref, pev_ref, w1v_ref, b1v_ref, w2v_ref, ov_ref)):
        outs = []
        for g in range(NSA_KV_HEADS):
            x = x_ref[0, g]
            h_first = _dot((x + pe_ref[0:1, :]).astype(BF16), w1_ref[0:half, :])
            h_second = _dot((x + pe_ref[1:2, :]).astype(BF16), w1_ref[half:2 * half, :])
            h = h_first + pltpu.roll(h_second, nc - 1, axis=0) + b1_ref[...]
            a = h * jax.nn.sigmoid(h)
            outs.append(_dot(a.astype(BF16), w2_ref[...]))
        out = jnp.concatenate(outs, axis=1)
        o_ref[0] = (out.T if o_ref is ov_ref else out).astype(o_ref.dtype)


def _compress(xk, xv, pek, w1k, b1k, w2k, pev, w1v, b1v, w2v, B, S):
    nc = S // CMP_STRIDE
    G = NSA_KV_HEADS
    x_spec = pl.BlockSpec((1, G, nc, CMP_STRIDE * HEAD_DIM), lambda b: (b, 0, 0, 0))
    full = lambda a: pl.BlockSpec(a.shape, lambda b: (0,) * a.ndim)
    out_spec = pl.BlockSpec((1, nc, LANES), lambda b: (b, 0, 0))
    out_spec_t = pl.BlockSpec((1, LANES, nc), lambda b: (b, 0, 0))
    return pl.pallas_call(
        functools.partial(_compress_kernel, nc=nc),
        grid=(B,),
        in_specs=[x_spec, x_spec, full(pek), full(w1k), full(b1k), full(w2k),
                  full(pev), full(w1v), full(b1v), full(w2v)],
        out_specs=[out_spec, out_spec_t],
        out_shape=[jax.ShapeDtypeStruct((B, nc, LANES), BF16), jax.ShapeDtypeStruct((B, LANES, nc), BF16)],
        compiler_params=_cparams(("parallel",)),
    )(xk, xv, pek, w1k, b1k, w2k, pev, w1v, b1v, w2v)


def _nsa_kernel(qn_ref, qr_ref, kc_ref, vct_ref, ks_ref, vst_ref, kw_ref, vwt_ref, gt_ref, ot_ref,
                o_ref, acc_ref, m_ref, l_ref, *, S):
    i = pl.program_id(1)
    nc = S // CMP_STRIDE
    ns = S // SEL_BLOCK
    k_top = min(SEL_TOPK, ns)
    G = NSA_KV_HEADS
    R = H_NSA // G
    RT = R * TQ
    HD = HEAD_DIM
    low = lax.broadcasted_iota(jnp.int32, (TQ, LANES), 1) < HD
    gates_t = gt_ref[...].T
    qn = qn_ref[...] * SCALE
    qr = qr_ref[...]

    def tiled(mask, s, fill):
        return jnp.concatenate([jnp.where(mask, s[:, r * TQ:(r + 1) * TQ], fill) for r in range(R)], axis=1)

    def stack(qfull, g):
        in_g = low if g == 0 else jnp.logical_not(low)
        parts = []
        for r in range(R):
            a, b = divmod(R * g + r, 2)
            blk = qfull[:, a * LANES:(a + 1) * LANES]
            if b != g:
                blk = pltpu.roll(blk, HD, axis=1)
            parts.append(jnp.where(in_g, blk, 0.0))
        return jnp.concatenate(parts, axis=0).astype(BF16)

    def rows_of(g, x):
        return x[g * HD:(g + 1) * HD, :]


    qn_s = [stack(qn, g) for g in range(G)]
    c_valid = (lax.broadcasted_iota(jnp.int32, (nc, TQ), 0) * CMP_STRIDE + (CMP_LEN - 1)
               <= lax.broadcasted_iota(jnp.int32, (nc, TQ), 1) + i * TQ)
    scs = [tiled(c_valid, _dot_nt(kc_ref[0], qn_s[g]), NEG) for g in range(G)]
    es = [tiled(c_valid, jnp.exp(sc - jnp.max(sc, axis=0, keepdims=True)), 0.0) for sc in scs]
    dens = [jnp.sum(e, axis=0, keepdims=True) for e in es]
    p_cmps = [e / jnp.where(den > 0.0, den, 1.0) for e, den in zip(es, dens)]
    vct = vct_ref[0]
    o_cmps = [_dot(rows_of(g, vct), p_cmps[g].astype(BF16)) for g in range(G)]

    p_slcs = []
    for p in p_cmps:
        p_sum = p[:, 0:TQ]
        for r in range(1, R):
            p_sum = p_sum + p[:, r * TQ:(r + 1) * TQ]
        p_slcs.append(_dot(ot_ref[...], _split_bf16(p_sum, 0)))
    n_idx = lax.broadcasted_iota(jnp.int32, (ns, TQ), 0)
    t_s = lax.broadcasted_iota(jnp.int32, (ns, TQ), 1) + i * TQ
    forced = jnp.logical_or(n_idx == 0, n_idx == (t_s >> SEL_SHIFT))
    s_valid = n_idx * SEL_BLOCK <= t_s
    scores = [jnp.where(forced, 1e4, jnp.where(s_valid, p_slc, -1.0)) for p_slc in p_slcs]
    ranks = [jnp.zeros((ns, TQ), F32) for _ in range(G)]
    for m_i in range(ns):
        for g in range(G):
            sm = scores[g][m_i:m_i + 1, :]
            ge = jnp.where(sm >= scores[g], 1.0, 0.0)
            gt = jnp.where(sm > scores[g], 1.0, 0.0)
            ranks[g] = ranks[g] + jnp.where(n_idx > m_i, ge, gt)
    sels = []
    for g in range(G):
        sel = jnp.where(ranks[g] < k_top, 1.0, 0.0)
        if ns < LANES:
            sel = jnp.concatenate([sel, jnp.zeros((LANES - ns, TQ), F32)], axis=0)
        sels.append(sel.astype(BF16))

    qr_s = [stack(qr, g) for g in range(G)]
    m_ref[...] = jnp.full(m_ref.shape, NEG, F32)
    l_ref[...] = jnp.zeros_like(l_ref)
    acc_ref[...] = jnp.zeros_like(acc_ref)

    def attend(items):
        stats = []
        for c, s, _ in items:
            m_old = m_ref[c]
            stats.append((m_old, jnp.maximum(m_old, jnp.max(s, axis=0, keepdims=True))))
        ps = [jnp.exp(s - m_new) for (_, s, _), (_, m_new) in zip(items, stats)]
        for (c, _, vt), (m_old, m_new), p in zip(items, stats, ps):
            alpha = jnp.exp(m_old - m_new)
            m_ref[c] = m_new
            l_ref[c] = alpha * l_ref[c] + jnp.sum(p, axis=0, keepdims=True)
            acc_ref[c] = alpha * acc_ref[c] + _dot(vt, p.astype(BF16))

    def chosen(g, off, width):
        blk = (lax.broadcasted_iota(jnp.int32, (width, LANES), 0) + off) >> SEL_SHIFT
        expand = jnp.where(blk == lax.broadcasted_iota(jnp.int32, (width, LANES), 1), 1.0, 0.0).astype(BF16)
        return _dot(expand, sels[g]) > 0.5

    n_win = jnp.minimum(i, (WINDOW + TQ - 2) // LANES) + 1
    n_w = n_win + ((i + 1 - n_win) & 1)

    def win_body(j, carry):
        kb = i - j
        off = pl.multiple_of(kb * LANES, LANES)
        s_pos = lax.broadcasted_iota(jnp.int32, (LANES, TQ), 0) + off
        t_q = lax.broadcasted_iota(jnp.int32, (LANES, TQ), 1) + i * TQ
        causal = s_pos <= t_q
        band = jnp.logical_and(causal, t_q - s_pos < WINDOW)
        ks_t, kw_t = ks_ref[pl.ds(off, LANES), :], kw_ref[pl.ds(off, LANES), :]
        vs_t, vw_t = vst_ref[0, kb], vwt_ref[0, kb]
        items = []
        for g in range(G):
            pick = jnp.logical_and(chosen(g, off, LANES), causal)
            items.append((2 * g, tiled(pick, _dot_nt(ks_t, qr_s[g]), NEG), rows_of(g, vs_t)))
            items.append((2 * g + 1, tiled(band, _dot_nt(kw_t, qr_s[g]), NEG), rows_of(g, vw_t)))
        attend(items)
        return carry

    lax.fori_loop(0, n_w, win_body, 0)

    def sel_body(j, carry):
        kb = i - n_w - 2 * j - 1
        off = pl.multiple_of(kb * LANES, LANES)
        ks_t = ks_ref[pl.ds(off, 2 * LANES), :]
        vs_t = jnp.concatenate([vst_ref[0, kb], vst_ref[0, kb + 1]], axis=1)
        attend([(2 * g, tiled(chosen(g, off, 2 * LANES), _dot_nt(ks_t, qr_s[g]), NEG), rows_of(g, vs_t))
                for g in range(G)])
        return carry

    lax.fori_loop(0, (i + 1 - n_w) // 2, sel_body, 0)

    heads = [None] * H_NSA
    for g in range(G):
        l_sel, l_win = l_ref[2 * g], l_ref[2 * g + 1]
        o_sel = acc_ref[2 * g] / jnp.where(l_sel > 0.0, l_sel, 1.0)
        o_win = acc_ref[2 * g + 1] / jnp.where(l_win > 0.0, l_win, 1.0)
        for r in range(R):
            h = R * g + r
            cols = slice(r * TQ, (r + 1) * TQ)
            heads[h] = (gates_t[3 * h:3 * h + 1, :] * o_cmps[g][:, cols]
                        + gates_t[3 * h + 1:3 * h + 2, :] * o_sel[:, cols]
                        + gates_t[3 * h + 2:3 * h + 3, :] * o_win[:, cols])
    for a in range(H_NSA // 2):
        pair = jnp.concatenate([heads[2 * a], heads[2 * a + 1]], axis=0)
        o_ref[:, a * LANES:(a + 1) * LANES] = pair.T.astype(o_ref.dtype)


def _nsa_attention(proj, q_rope, k_cmp, v_cmp_t, ks_r, vs_t, kw_r, vw_t, gates, overlap_t, B, S):
    nq = S // TQ
    nc = S // CMP_STRIDE
    nk = S // LANES
    G = NSA_KV_HEADS
    rt = H_NSA // G * TQ
    k_full = pl.BlockSpec((S, LANES), lambda b, i: (b, 0))
    vt_full = pl.BlockSpec((1, nk, LANES, LANES), lambda b, i: (b, 0, 0, 0))
    return pl.pallas_call(
        functools.partial(_nsa_kernel, S=S),
        grid=(B, nq),
        in_specs=[pl.BlockSpec((TQ, W_HEADS), lambda b, i: (b * nq + i, C_NQ // W_HEADS)),
                  pl.BlockSpec((TQ, W_HEADS), lambda b, i: (b * nq + i, 0)),
                  pl.BlockSpec((1, nc, LANES), lambda b, i: (b, 0, 0)),
                  pl.BlockSpec((1, LANES, nc), lambda b, i: (b, 0, 0)),
                  k_full, vt_full, k_full, vt_full,
                  pl.BlockSpec((TQ, LANES), lambda b, i: (b * nq + i, 0)),
                  pl.BlockSpec(overlap_t.shape, lambda b, i: (0, 0))],
        out_specs=pl.BlockSpec((TQ, W_HEADS), lambda b, i: (b * nq + i, 0)),
        out_shape=jax.ShapeDtypeStruct((B * S, W_HEADS), BF16),
        scratch_shapes=[pltpu.VMEM((2 * G, HEAD_DIM, rt), F32), pltpu.VMEM((2 * G, 1, rt), F32),
                        pltpu.VMEM((2 * G, 1, rt), F32)],
        compiler_params=_cparams(("parallel", "arbitrary")),
    )(proj, q_rope, k_cmp, v_cmp_t, ks_r, vs_t, kw_r, vw_t, gates, overlap_t)


def _merge_kernel(x_ref, osb_ref, onsa_ref, ofox_ref, m0_ref, m1_ref, m2_ref,
                  wsb_ref, wnsa_ref, wfox_ref, wout_ref, g_ref, o_ref):
    y = jax.nn.sigmoid(m0_ref[...]) * _dot(osb_ref[...], wsb_ref[...])
    y = y + jax.nn.sigmoid(m1_ref[...]) * _dot(onsa_ref[...], wnsa_ref[...])
    y = y + jax.nn.sigmoid(m2_ref[...]) * _dot(ofox_ref[...], wfox_ref[...])
    z = _dot(y.astype(BF16), wout_ref[...])
    o_ref[...] = x_ref[...] + _rms(z, g_ref[...])


def _merge(x, o_sb, o_nsa, o_fox, proj, w_sb, w_nsa, w_fox, w_out, g, tm):
    T, D = x.shape
    row = lambda w: pl.BlockSpec((tm, w), lambda i: (i, 0))
    full = lambda a: pl.BlockSpec(a.shape, lambda i: (0, 0))
    gate = lambda c: pl.BlockSpec((tm, D), lambda i: (i, C_MERGE // D + c))
    return pl.pallas_call(
        _merge_kernel,
        grid=(T // tm,),
        in_specs=[row(D), row(W_HEADS), row(W_HEADS), row(W_HEADS), gate(0), gate(1), gate(2),
                  full(w_sb), full(w_nsa), full(w_fox), full(w_out), full(g)],
        out_specs=row(D),
        out_shape=jax.ShapeDtypeStruct((T, D), F32),
        compiler_params=_cparams(("parallel",)),
    )(x, o_sb, o_nsa, o_fox, proj, proj, proj, w_sb, w_nsa, w_fox, w_out, g)


def _mem_kv_kernel(mem_ref, g_ref, wk_ref, wv_ref, k_ref, v_ref):
    mn = _rms(mem_ref[0], g_ref[...]).astype(BF16)
    k_ref[0] = _dot(mn, wk_ref[...]).astype(BF16)
    v_ref[0] = _dot(mn, wv_ref[...]).astype(BF16)


def _mem_kv(mem, g, wk, wv):
    B, M, D = mem.shape
    full = lambda a: pl.BlockSpec(a.shape, lambda b: (0, 0))
    out_spec = pl.BlockSpec((1, M, W_MEM), lambda b: (b, 0, 0))
    return pl.pallas_call(
        _mem_kv_kernel,
        grid=(B,),
        in_specs=[pl.BlockSpec((1, M, D), lambda b: (b, 0, 0)), full(g), full(wk), full(wv)],
        out_specs=[out_spec, out_spec],
        out_shape=[jax.ShapeDtypeStruct((B, M, W_MEM), BF16)] * 2,
        compiler_params=_cparams(("parallel",)),
    )(mem, g, wk, wv)


def _mem_attn_kernel(x_ref, g_ref, wq_ref, k_ref, v_ref, wo_ref, gp_ref, o_ref):
    x = x_ref[...]
    q = _dot(_rms(x, g_ref[...]).astype(BF16), wq_ref[...]) * SCALE
    k = k_ref[0]
    v = v_ref[0]
    lane = lax.broadcasted_iota(jnp.int32, q.shape, 1)
    o = jnp.zeros(q.shape, F32)
    for h in range(H_MEM):
        in_h = jnp.logical_and(lane >= h * HEAD_DIM, lane < (h + 1) * HEAD_DIM)
        s = _dot_nt(jnp.where(in_h, q, 0.0).astype(BF16), k)
        e = jnp.exp(s - jnp.max(s, axis=-1, keepdims=True))
        p = e / jnp.sum(e, axis=-1, keepdims=True)
        o = o + jnp.where(in_h, _dot(p.astype(BF16), v), 0.0)
    y = _dot(o.astype(BF16), wo_ref[...])
    o_ref[...] = x + _rms(y, gp_ref[...])


def _mem_attn(x, g_pre, wq, k, v, wo, g_post, B, S, tm):
    T, D = x.shape
    M = k.shape[1]
    nt = S // tm
    full = lambda a: pl.BlockSpec(a.shape, lambda b, i: (0, 0))
    kv_spec = pl.BlockSpec((1, M, W_MEM), lambda b, i: (b, 0, 0))
    row = pl.BlockSpec((tm, D), lambda b, i: (b * nt + i, 0))
    return pl.pallas_call(
        _mem_attn_kernel,
        grid=(B, nt),
        in_specs=[row, full(g_pre), full(wq), kv_spec, kv_spec, full(wo), full(g_post)],
        out_specs=row,
        out_shape=jax.ShapeDtypeStruct((T, D), F32),
        compiler_params=_cparams(("parallel", "parallel")),
    )(x, g_pre, wq, k, v, wo, g_post)


def _ffn_kernel(x_ref, g_ref, wg_ref, wu_ref, wd_ref, gp_ref, o_ref, h_ref, acc_ref):
    j = pl.program_id(1)

    @pl.when(j == 0)
    def _():
        h_ref[...] = _rms(x_ref[...], g_ref[...]).astype(BF16)
        acc_ref[...] = jnp.zeros_like(acc_ref)

    h = h_ref[...]
    a = _dot(h, wg_ref[...])
    u = _dot(h, wu_ref[...])
    acc_ref[...] += _dot((a * jax.nn.sigmoid(a) * u).astype(BF16), wd_ref[...])

    @pl.when(j == pl.num_programs(1) - 1)
    def _():
        o_ref[...] = x_ref[...] + _rms(acc_ref[...], gp_ref[...])


def _ffn(x, g_pre, wg, wu, wd, g_post, tm, tf):
    T, D = x.shape
    F = wg.shape[1]
    row = pl.BlockSpec((tm, D), lambda i, j: (i, 0))
    vec = pl.BlockSpec((1, D), lambda i, j: (0, 0))
    return pl.pallas_call(
        _ffn_kernel,
        grid=(T // tm, F // tf),
        in_specs=[row, vec,
                  pl.BlockSpec((D, tf), lambda i, j: (0, j)),
                  pl.BlockSpec((D, tf), lambda i, j: (0, j)),
                  pl.BlockSpec((tf, D), lambda i, j: (j, 0)),
                  vec],
        out_specs=row,
        out_shape=jax.ShapeDtypeStruct((T, D), F32),
        scratch_shapes=[pltpu.VMEM((tm, D), BF16), pltpu.VMEM((tm, D), F32)],
        compiler_params=_cparams(("parallel", "arbitrary")),
    )(x, g_pre, wg, wu, wd, g_post)


def _scan_matrix():
    j = np.arange(LANES)
    later = (j[:, None] > j[None, :]).astype(np.float32)
    u = np.concatenate([later, np.ones((LANES, LANES), np.float32)], axis=1)
    return jnp.asarray(np.concatenate([u, u], axis=0), dtype=BF16)


def _overlap_t(S):
    nc, ns = S // CMP_STRIDE, S // SEL_BLOCK
    c0 = np.arange(nc) * CMP_STRIDE
    n0 = np.arange(ns) * SEL_BLOCK
    ov = (c0[None, :] < n0[:, None] + SEL_BLOCK) & (c0[None, :] + CMP_LEN > n0[:, None])
    ov = ov & (np.arange(nc)[None, :] < nc - 1)
    ov = ov.astype(np.float32)
    return jnp.asarray(np.concatenate([ov, ov], axis=1), dtype=BF16)


FOX_F_LANE = 24
N_PARTS = 3


def _fox_bias_placement():
    n_pair = H_FOX // 2
    pa = np.zeros((N_PARTS * LANES, n_pair * 2 * LANES), np.float32)
    pb = np.zeros((N_PARTS * LANES, n_pair * LANES), np.float32)
    oa = np.zeros((1, n_pair * 2 * LANES), np.float32)
    ob = np.zeros((1, n_pair * LANES), np.float32)
    for p in range(n_pair):
        for e in range(2):
            src = FOX_F_LANE + 2 * p + e
            for x in range(N_PARTS):
                pa[x * LANES + src, (2 * p + e) * LANES + 8 * e + x] = 1.0
                oa[0, (2 * p + e) * LANES + 8 * e + N_PARTS + x] = 1.0
                pb[x * LANES + src, p * LANES + 8 * e + N_PARTS + x] = -1.0
                ob[0, p * LANES + 8 * e + x] = 1.0
    return (jnp.asarray(pa, dtype=BF16), jnp.asarray(pb, dtype=BF16), jnp.asarray(oa), jnp.asarray(ob))


def _rope_rows():
    half = ROPE_DIM // 2
    inv_freq = ROPE_THETA ** (-jnp.arange(half, dtype=F32) / half)
    d = np.arange(LANES) % HEAD_DIM
    rot = d < ROPE_DIM
    freq = jnp.where(jnp.asarray(rot), inv_freq[jnp.asarray(d % half)], 0.0)
    s_up = jnp.asarray(((d >= half) & rot).astype(np.float32))
    s_dn = jnp.asarray(-(d < half).astype(np.float32))
    rows = jnp.stack([freq, s_up, s_dn] + [jnp.zeros((LANES,), F32)] * 5)
    return rows.astype(F32)


def _reorder_w_in(w):
    pad = jnp.zeros((w.shape[0], MISC_W - (_O_FOX_Q - _O_NSA_G) - (_O_MERGE - _O_FOX_F)), w.dtype)
    return jnp.concatenate([
        w[:, _O_MERGE:_O_END],
        w[:, :_O_NSA_G - 6 * LANES],
        w[:, _O_FOX_Q:_O_FOX_F],
        w[:, _O_NSA_G - 6 * LANES:_O_NSA_G],
        w[:, _O_NSA_G:_O_FOX_Q],
        w[:, _O_FOX_F:_O_MERGE],
        pad], axis=1).astype(BF16)


def _row_tile(n, target):
    t = min(n, target)
    while n % t:
        t //= 2
    return t


def kernel(x, mem, positions, g_pre_mix, g_post_mix, g_pre_mem, g_mem, g_post_mem, g_pre_ffn, g_post_ffn,
           w_in, b_fox_f, cmp_pe_k, cmp_w1_k, cmp_b1_k, cmp_w2_k, cmp_pe_v, cmp_w1_v, cmp_b1_v, cmp_w2_v,
           w_up_sb, w_up_nsa, w_up_fox, w_out, w_mem_q, w_mem_k, w_mem_v, w_mem_o,
           w_ffn_gate, w_ffn_up, w_ffn_down):
    B, S, D = x.shape
    T = B * S
    depth = w_in.shape[0]
    G = NSA_KV_HEADS
    nc = S // CMP_STRIDE
    u2 = _scan_matrix()
    overlap_t = _overlap_t(S)
    rope_c = _rope_rows()
    place = _fox_bias_placement()
    pos3 = positions.reshape(B, S, 1)
    vec = lambda g: g.reshape(1, -1)
    pe_rows = lambda pe: pe.reshape(2, CMP_STRIDE * HEAD_DIM)
    tm_big = _row_tile(T, 1024)
    tm_mid = _row_tile(T, 512)
    d_ff = w_ffn_gate.shape[2]
    tf = d_ff // 2 if (d_ff // 2) % LANES == 0 else d_ff

    xf = x.reshape(T, D)
    for l in range(depth):
        proj = _norm_matmul(xf, vec(g_pre_mix[l]), _reorder_w_in(w_in[l]), tm_big, N_IN // 5)

        o_sb = _sb_attention(proj, u2, B, S)

        bias_row = jnp.zeros((1, LANES), F32).at[0, FOX_F_LANE:FOX_F_LANE + H_FOX].set(b_fox_f[l])
        gates, fox_a, fox_b, q_rope, ks_r, kw_r = _prep(proj, bias_row, pos3, rope_c, place, B, S)
        v_fox_t = proj[:, C_FV:C_FV + W_HEADS].astype(BF16).reshape(B, S // TQA, TQA, H_FOX // 2, LANES)
        v_fox_t = v_fox_t.transpose(0, 3, 1, 4, 2)
        o_fox = _fox_attention(proj, v_fox_t, fox_a, fox_b, B, S)

        kv_c = proj[:, C_NKV:C_NKV + 2 * LANES].reshape(B, nc, CMP_STRIDE, 2, G, HEAD_DIM)
        kv_c = kv_c.transpose(3, 0, 4, 1, 2, 5).reshape(2, B, G, nc, CMP_STRIDE * HEAD_DIM)
        k_cmp, v_cmp_t = _compress(
            kv_c[0], kv_c[1],
            pe_rows(cmp_pe_k[l]), cmp_w1_k[l].astype(BF16), vec(cmp_b1_k[l]), cmp_w2_k[l].astype(BF16),
            pe_rows(cmp_pe_v[l]), cmp_w1_v[l].astype(BF16), vec(cmp_b1_v[l]), cmp_w2_v[l].astype(BF16), B, S)
        v_nsa_t = proj[:, C_NKV + 3 * LANES:C_NKV + 6 * LANES].astype(BF16).reshape(B, S // LANES, LANES, 3, LANES)
        v_nsa_t = v_nsa_t.transpose(3, 0, 1, 4, 2)
        o_nsa = _nsa_attention(proj, q_rope, k_cmp, v_cmp_t, ks_r, v_nsa_t[0], kw_r, v_nsa_t[2], gates,
                               overlap_t, B, S)

        xf = _merge(xf, o_sb, o_nsa, o_fox, proj, w_up_sb[l].astype(BF16), w_up_nsa[l].astype(BF16),
                    w_up_fox[l].astype(BF16), w_out[l].astype(BF16), vec(g_post_mix[l]), tm_mid)

        k_mem, v_mem = _mem_kv(mem, vec(g_mem[l]), w_mem_k[l].astype(BF16), w_mem_v[l].astype(BF16))
        xf = _mem_attn(xf, vec(g_pre_mem[l]), w_mem_q[l].astype(BF16), k_mem, v_mem,
                       w_mem_o[l].astype(BF16), vec(g_post_mem[l]), B, S, _row_tile(S, 512))

        xf = _ffn(xf, vec(g_pre_ffn[l]), w_ffn_gate[l].astype(BF16), w_ffn_up[l].astype(BF16),
                  w_ffn_down[l].astype(BF16), vec(g_post_ffn[l]), tm_mid, tf)
    return xf.reshape(B, S, D)
```

```python
import functools

import numpy as np
import jax
import jax.numpy as jnp
from jax import lax
from jax.experimental import pallas as pl
from jax.experimental.pallas import tpu as pltpu

F32 = jnp.float32
BF16 = jnp.bfloat16

D_MODEL = 1024
HEAD_DIM = 64
H_SB = 8
H_NSA = 8
NSA_KV_HEADS = 2
H_FOX = 8
H_MEM = 4
N_BRANCH = 3
ROPE_THETA = 500000.0
ROPE_DIM = HEAD_DIM // 4
CMP_STRIDE = 16
CMP_LEN = 2 * CMP_STRIDE
CMP_HIDDEN = 256
SEL_BLOCK = 64
SEL_SHIFT = SEL_BLOCK.bit_length() - 1
SEL_TOPK = 8
WINDOW = 512
W_HEADS = 8 * HEAD_DIM
W_MEM = H_MEM * HEAD_DIM
EPS = 1e-6
SCALE = HEAD_DIM ** -0.5
NEG = -1e30

LANES = 128
TQ = 128
TQA = 256

C_MERGE = 0
C_SBQ, C_SBK, C_SBV = 3072, 3584, 4096
C_NQ = 4608
C_FQ, C_FK, C_FV = 5120, 5632, 6144
C_NKV = 6656
C_MISC = 7424
MISC_W = 256
N_IN = 7680
_O_NSA_G, _O_FOX_Q, _O_FOX_F, _O_MERGE, _O_END = 2816, 2840, 4376, 4384, 7456

VMEM_LIMIT = 56 * 1024 * 1024


def _cparams(sem):
    return pltpu.CompilerParams(dimension_semantics=sem, vmem_limit_bytes=VMEM_LIMIT)


def _dot(a, b):
    return jnp.dot(a, b, preferred_element_type=F32)


def _dot_nt(a, b):
    return lax.dot_general(a, b, (((1,), (1,)), ((), ())), preferred_element_type=F32)


def _rms(x, g):
    ms = jnp.mean(x * x, axis=-1, keepdims=True)
    return x * lax.rsqrt(ms + EPS) * g


def _split_bf16(x, axis=1):
    hi = x.astype(BF16)
    lo = (x - hi.astype(F32)).astype(BF16)
    return jnp.concatenate([hi, lo], axis=axis)


def _norm_matmul_kernel(x_ref, g_ref, w_ref, o_ref, h_ref):
    @pl.when(pl.program_id(1) == 0)
    def _():
        h_ref[...] = _rms(x_ref[...], g_ref[...]).astype(BF16)

    o_ref[...] = _dot(h_ref[...], w_ref[...])


def _norm_matmul(x, g, w, tm, tn):
    T, D = x.shape
    N = w.shape[1]
    return pl.pallas_call(
        _norm_matmul_kernel,
        grid=(T // tm, N // tn),
        in_specs=[pl.BlockSpec((tm, D), lambda i, j: (i, 0)),
                  pl.BlockSpec((1, D), lambda i, j: (0, 0)),
                  pl.BlockSpec((D, tn), lambda i, j: (0, j))],
        out_specs=pl.BlockSpec((tm, tn), lambda i, j: (i, j)),
        out_shape=jax.ShapeDtypeStruct((T, N), F32),
        scratch_shapes=[pltpu.VMEM((tm, D), BF16)],
        compiler_params=_cparams(("parallel", "arbitrary")),
    )(x, g, w)


def _stack_pair(q, low):
    return jnp.concatenate([jnp.where(low, q, 0.0), jnp.where(low, 0.0, q)], axis=0).astype(BF16)


def _sb_kernel(q_ref, k_ref, v_ref, u_ref, o_ref, acc_ref, cs_ref, arg_ref, rs_ref):
    i = pl.program_id(2)
    rt = 2 * TQA
    n_grp = TQA // LANES
    lane = lax.broadcasted_iota(jnp.int32, (rt, LANES), 1)
    t_q = (lax.broadcasted_iota(jnp.int32, (rt, LANES), 0) & (TQA - 1)) + i * TQA
    low = lax.broadcasted_iota(jnp.int32, (TQA, LANES), 1) < HEAD_DIM
    qs = _stack_pair(q_ref[...] * SCALE, low)

    def offsets(kb_first):
        return [pl.multiple_of((kb_first - d) * LANES, LANES) for d in range(n_grp)]

    def logits(offs):
        return [_dot_nt(qs, k_ref[pl.ds(off, LANES), :].astype(BF16)) for off in offs]

    def scan(offs, zs, on_diagonal):
        log_keeps, log_betas, stricts = [], [], []
        for off, z in zip(offs, zs):
            log_keep = -(jnp.maximum(z, 0.0) + jnp.log(1.0 + jnp.exp(-jnp.abs(z))))
            log_betas.append(z + log_keep)
            if on_diagonal:
                stricts.append((lane + off) < t_q)
                log_keep = jnp.where(stricts[-1], log_keep, 0.0)
            log_keeps.append(log_keep)
        c2s = [_dot(_split_bf16(log_keep), u_ref[...]) for log_keep in log_keeps]
        args = [log_beta + c2[:, :LANES] for log_beta, c2 in zip(log_betas, c2s)]
        if on_diagonal:
            args = [jnp.where(strict, arg, NEG) for strict, arg in zip(stricts, args)]
        return args, [c2[:, LANES:] for c2 in c2s]

    def stash(args, row_sums):
        for d in range(n_grp):
            arg_ref[d] = args[d]
            rs_ref[d] = row_sums[d]

    def weights(offs, args, row_sums):
        cs = cs_ref[...]
        pvs = []
        for d, off in enumerate(offs):
            w = jnp.exp(args[d] + cs)
            pvs.append(_dot(w.astype(BF16), v_ref[pl.ds(off, LANES), :].astype(BF16)))
            cs = cs + row_sums[d]
        return cs, pvs

    def accumulate(cs, pvs):
        acc = acc_ref[...]
        for pv in pvs:
            acc = acc + pv
        acc_ref[...] = acc
        cs_ref[...] = cs

    acc_ref[...] = jnp.zeros_like(acc_ref)
    cs_ref[...] = jnp.zeros_like(cs_ref)
    first = offsets(n_grp * i + n_grp - 1)
    stash(*scan(first, logits(first), True))

    def body(j, carry):
        cur = offsets(n_grp * (i - j) + n_grp - 1)
        nxt = offsets(n_grp * (i - j) - 1)
        args = [arg_ref[d] for d in range(n_grp)]
        row_sums = [rs_ref[d] for d in range(n_grp)]
        zs = logits(nxt)
        cs, pvs = weights(cur, args, row_sums)
        nxt_args, nxt_sums = scan(nxt, zs, False)
        accumulate(cs, pvs)
        stash(nxt_args, nxt_sums)
        return carry

    lax.fori_loop(0, i, body, 0)
    accumulate(*weights(offsets(n_grp - 1), [arg_ref[d] for d in range(n_grp)],
                        [rs_ref[d] for d in range(n_grp)]))
    acc = acc_ref[...]
    o_ref[...] = jnp.where(low, acc[:TQA], acc[TQA:]).astype(o_ref.dtype)


def _sb_attention(proj, u2, B, S):
    nq = S // TQA
    cq, ck, cv = C_SBQ // LANES, C_SBK // LANES, C_SBV // LANES
    return pl.pallas_call(
        _sb_kernel,
        grid=(B, H_SB // 2, nq),
        in_specs=[pl.BlockSpec((TQA, LANES), lambda b, p, i: (b * nq + i, cq + p)),
                  pl.BlockSpec((S, LANES), lambda b, p, i: (b, ck + p)),
                  pl.BlockSpec((S, LANES), lambda b, p, i: (b, cv + p)),
                  pl.BlockSpec((2 * LANES, 2 * LANES), lambda b, p, i: (0, 0))],
        out_specs=pl.BlockSpec((TQA, LANES), lambda b, p, i: (b * nq + i, p)),
        out_shape=jax.ShapeDtypeStruct((B * S, W_HEADS), BF16),
        scratch_shapes=[pltpu.VMEM((2 * TQA, LANES), F32), pltpu.VMEM((2 * TQA, LANES), F32),
                        pltpu.VMEM((TQA // LANES, 2 * TQA, LANES), F32),
                        pltpu.VMEM((TQA // LANES, 2 * TQA, LANES), F32)],
        compiler_params=_cparams(("parallel", "parallel", "arbitrary")),
    )(proj, proj, proj, u2)


def _fox_kernel(q_ref, k_ref, vt_ref, fa_ref, fb_ref, o_ref, acc_ref, m_ref, l_ref, s_ref, cmax_ref):
    i = pl.program_id(2)
    rt = 2 * TQA
    low = lax.broadcasted_iota(jnp.int32, (TQA, LANES), 1) < HEAD_DIM
    qs = _stack_pair(q_ref[...] * SCALE, low)
    qa = jnp.concatenate([qs, jnp.concatenate([fa_ref[:, :LANES], fa_ref[:, LANES:]], axis=0)], axis=1)
    acc_ref[...] = jnp.zeros_like(acc_ref)
    l_ref[...] = jnp.zeros_like(l_ref)
    m_ref[...] = jnp.full(m_ref.shape, NEG, F32)

    def scores(kb, on_diagonal):
        off = pl.multiple_of(kb * TQA, TQA)
        kbias = jnp.concatenate([k_ref[pl.ds(off, TQA), :].astype(BF16), fb_ref[pl.ds(off, TQA), :]], axis=1)
        s = _dot_nt(kbias, qa)
        if on_diagonal:
            causal = lax.broadcasted_iota(jnp.int32, (TQA, rt), 0) <= (
                lax.broadcasted_iota(jnp.int32, (TQA, rt), 1) & (TQA - 1))
            s = jnp.where(causal, s, NEG)
        return s

    def stash(s):
        s_ref[...] = s
        cmax_ref[...] = jnp.max(s, axis=0, keepdims=True)

    def absorb(kb, s, cmax):
        m_old = m_ref[...]
        m_new = jnp.maximum(m_old, cmax)
        p = jnp.exp(s - m_new)
        alpha = jnp.exp(m_old - m_new)
        m_ref[...] = m_new
        l_ref[...] = alpha * l_ref[...] + jnp.sum(p, axis=0, keepdims=True)
        p = p.astype(BF16)
        vt = vt_ref[0, 0, kb]
        return alpha, [_dot(vt[e * HEAD_DIM:(e + 1) * HEAD_DIM, :], p[:, e * TQA:(e + 1) * TQA]) for e in range(2)]

    def accumulate(alpha, pvs):
        for e in range(2):
            acc_ref[e] = alpha[:, e * TQA:(e + 1) * TQA] * acc_ref[e] + pvs[e]

    stash(scores(i, True))

    def body(j, carry):
        kb = i - j
        s_cur, cmax = s_ref[...], cmax_ref[...]
        s_next = scores(kb - 1, False)
        alpha, pvs = absorb(kb, s_cur, cmax)
        stash(s_next)
        accumulate(alpha, pvs)
        return carry

    lax.fori_loop(0, i, body, 0)
    accumulate(*absorb(0, s_ref[...], cmax_ref[...]))
    l = l_ref[...]
    den = jnp.where(l > 0.0, l, 1.0)
    o_t = jnp.concatenate([acc_ref[0] / den[:, :TQA], acc_ref[1] / den[:, TQA:]], axis=0)
    o_ref[...] = o_t.T.astype(o_ref.dtype)


def _fox_attention(proj, v_t, fox_a, fox_b, B, S):
    nq = S // TQA
    cq, ck = C_FQ // LANES, C_FK // LANES
    return pl.pallas_call(
        _fox_kernel,
        grid=(B, H_FOX // 2, nq),
        in_specs=[pl.BlockSpec((TQA, LANES), lambda b, p, i: (b * nq + i, cq + p)),
                  pl.BlockSpec((S, LANES), lambda b, p, i: (b, ck + p)),
                  pl.BlockSpec((1, 1, nq, LANES, TQA), lambda b, p, i: (b, p, 0, 0, 0)),
                  pl.BlockSpec((TQA, 2 * LANES), lambda b, p, i: (b * nq + i, p)),
                  pl.BlockSpec((S, LANES), lambda b, p, i: (b, p))],
        out_specs=pl.BlockSpec((TQA, LANES), lambda b, p, i: (b * nq + i, p)),
        out_shape=jax.ShapeDtypeStruct((B * S, W_HEADS), BF16),
        scratch_shapes=[pltpu.VMEM((2, HEAD_DIM, TQA), F32), pltpu.VMEM((1, 2 * TQA), F32),
                        pltpu.VMEM((1, 2 * TQA), F32), pltpu.VMEM((TQA, 2 * TQA), F32),
                        pltpu.VMEM((1, 2 * TQA), F32)],
        compiler_params=_cparams(("parallel", "parallel", "arbitrary")),
    )(proj, proj, v_t, fox_a, fox_b)


def _rope(x, cos, s_up, s_dn):
    return x * cos + pltpu.roll(x, ROPE_DIM // 2, axis=1) * s_up + pltpu.roll(x, LANES - ROPE_DIM // 2, axis=1) * s_dn


def _prep_kernel(misc_ref, bias_ref, pos_ref, rc_ref, q_ref, ks_ref, kw_ref, pa_ref, pb_ref, oa_ref, ob_ref,
                 vf_ref, vs_ref, vw_ref,
                 gates_ref, fa_ref, fb_ref, qr_ref, ksr_ref, kwr_ref, vft_ref, vst_ref, vwt_ref, carry_ref, *, ts):
    @pl.when(pl.program_id(1) == 0)
    def _():
        carry_ref[...] = jnp.zeros_like(carry_ref)

    logits = misc_ref[...]
    gates_ref[...] = jax.nn.sigmoid(logits)
    zf = logits + bias_ref[...]
    x = jnp.minimum(zf, 0.0) - jnp.log(1.0 + jnp.exp(-jnp.abs(zf)))
    row = lax.broadcasted_iota(jnp.int32, (ts, LANES), 0)
    sh = 1
    while sh < ts:
        x = x + jnp.where(row >= sh, pltpu.roll(x, sh, axis=0), 0.0)
        sh *= 2
    x = x + carry_ref[...]
    carry_ref[...] = x[ts - 1:ts, :]
    hi = x.astype(BF16)
    r1 = x - hi.astype(F32)
    mid = r1.astype(BF16)
    lo = (r1 - mid.astype(F32)).astype(BF16)
    parts = jnp.concatenate([hi, mid, lo], axis=1)
    fa_ref[...] = (_dot(parts, pa_ref[...]) + oa_ref[...]).astype(BF16)
    fb_ref[...] = (_dot(parts, pb_ref[...]) + ob_ref[...]).astype(BF16)

    ang = pos_ref[0].astype(F32) * rc_ref[0:1, :]
    cos = jnp.cos(ang)
    sin = jnp.sin(ang)
    s_up = sin * rc_ref[1:2, :]
    s_dn = sin * rc_ref[2:3, :]
    for a in range(W_HEADS // LANES):
        qa = q_ref[:, a * LANES:(a + 1) * LANES]
        qr_ref[:, a * LANES:(a + 1) * LANES] = _rope(qa, cos, s_up, s_dn) * SCALE
    ksr_ref[...] = _rope(ks_ref[...], cos, s_up, s_dn).astype(BF16)
    kwr_ref[...] = _rope(kw_ref[...], cos, s_up, s_dn).astype(BF16)

    for p in range(H_FOX // 2):
        for u in range(ts // TQA):
            vft_ref[0, p, u] = vf_ref[u * TQA:(u + 1) * TQA, p * LANES:(p + 1) * LANES].T.astype(BF16)
    for u in range(ts // LANES):
        vst_ref[0, u] = vs_ref[u * LANES:(u + 1) * LANES, :].T.astype(BF16)
        vwt_ref[0, u] = vw_ref[u * LANES:(u + 1) * LANES, :].T.astype(BF16)


def _prep(proj, bias_row, pos3, rope_c, place, B, S):
    ts = min(512, S)
    nt = S // ts
    T = B * S
    row_blk = lambda w: pl.BlockSpec((ts, w), lambda b, t: (b * nt + t, 0))
    full = lambda a: pl.BlockSpec(a.shape, lambda b, t: (0, 0))
    wa, wb = place[0].shape[1], place[1].shape[1]
    return pl.pallas_call(
        functools.partial(_prep_kernel, ts=ts),
        grid=(B, nt),
        in_specs=[pl.BlockSpec((ts, LANES), lambda b, t: (b * nt + t, C_MISC // LANES)),
                  pl.BlockSpec((1, LANES), lambda b, t: (0, 0)),
                  pl.BlockSpec((1, ts, 1), lambda b, t: (b, t, 0)),
                  pl.BlockSpec((8, LANES), lambda b, t: (0, 0)),
                  pl.BlockSpec((ts, W_HEADS), lambda b, t: (b * nt + t, C_NQ // W_HEADS)),
                  pl.BlockSpec((ts, LANES), lambda b, t: (b * nt + t, C_NKV // LANES + 2)),
                  pl.BlockSpec((ts, LANES), lambda b, t: (b * nt + t, C_NKV // LANES + 4)),
                  full(place[0]), full(place[1]), full(place[2]), full(place[3]),
                  pl.BlockSpec((ts, W_HEADS), lambda b, t: (b * nt + t, C_FV // W_HEADS)),
                  pl.BlockSpec((ts, LANES), lambda b, t: (b * nt + t, C_NKV // LANES + 3)),
                  pl.BlockSpec((ts, LANES), lambda b, t: (b * nt + t, C_NKV // LANES + 5))],
        out_specs=[row_blk(LANES), row_blk(wa), row_blk(wb), row_blk(W_HEADS), row_blk(LANES), row_blk(LANES),
                   pl.BlockSpec((1, H_FOX // 2, ts // TQA, LANES, TQA), lambda b, t: (b, 0, t, 0, 0)),
                   pl.BlockSpec((1, ts // LANES, LANES, LANES), lambda b, t: (b, t, 0, 0)),
                   pl.BlockSpec((1, ts // LANES, LANES, LANES), lambda b, t: (b, t, 0, 0))],
        out_shape=[jax.ShapeDtypeStruct((T, LANES), F32),
                   jax.ShapeDtypeStruct((T, wa), BF16),
                   jax.ShapeDtypeStruct((T, wb), BF16),
                   jax.ShapeDtypeStruct((T, W_HEADS), F32),
                   jax.ShapeDtypeStruct((T, LANES), BF16),
                   jax.ShapeDtypeStruct((T, LANES), BF16),
                   jax.ShapeDtypeStruct((B, H_FOX // 2, S // TQA, LANES, TQA), BF16),
                   jax.ShapeDtypeStruct((B, S // LANES, LANES, LANES), BF16),
                   jax.ShapeDtypeStruct((B, S // LANES, LANES, LANES), BF16)],
        scratch_shapes=[pltpu.VMEM((1, LANES), F32)],
        compiler_params=_cparams(("parallel", "arbitrary")),
    )(proj, bias_row, pos3, rope_c, proj, proj, proj, *place, proj, proj, proj)


def _compress_kernel(kc_ref, vc_ref, pek_ref, w1k_ref, b1k_ref, w2k_ref,
                     pev_ref, w1v_ref, b1v_ref, w2v_ref, ok_ref, ov_ref, *, nc):
    for x_ref, pe_ref, w1_ref, b1_ref, w2_ref, o_ref in (
            (kc_ref, pek_ref, w1k_ref, b1k_ref, w2k_ref, ok_ref),
            (vc_ref, pev_ref, w1v_ref, b1v_ref, w2v_ref, ov_ref)):
        h_first = jnp.zeros((nc, NSA_KV_HEADS * CMP_HIDDEN), F32)
        h_second = jnp.zeros((nc, NSA_KV_HEADS * CMP_HIDDEN), F32)
        for l in range(CMP_STRIDE):
            x = x_ref[pl.ds(l, nc, stride=CMP_STRIDE), :]
            h_first = h_first + _dot((x + pe_ref[l:l + 1, :]).astype(BF16), w1_ref[l])
            h_second = h_second + _dot((x + pe_ref[CMP_STRIDE + l:CMP_STRIDE + l + 1, :]).astype(BF16),
                                       w1_ref[CMP_STRIDE + l])
        h = h_first + pltpu.roll(h_second, nc - 1, axis=0) + b1_ref[...]
        a = h * jax.nn.sigmoid(h)
        out = _dot(a.astype(BF16), w2_ref[...])
        o_ref[0] = (out.T if o_ref is ov_ref else out).astype(o_ref.dtype)


def _compress(proj, pek, w1k, b1k, w2k, pev, w1v, b1v, w2v, B, S):
    nc = S // CMP_STRIDE
    full = lambda a: pl.BlockSpec(a.shape, lambda b: (0,) * a.ndim)
    out_spec = pl.BlockSpec((1, nc, LANES), lambda b: (b, 0, 0))
    out_spec_t = pl.BlockSpec((1, LANES, nc), lambda b: (b, 0, 0))
    return pl.pallas_call(
        functools.partial(_compress_kernel, nc=nc),
        grid=(B,),
        in_specs=[pl.BlockSpec((S, LANES), lambda b: (b, C_NKV // LANES)),
                  pl.BlockSpec((S, LANES), lambda b: (b, C_NKV // LANES + 1)),
                  full(pek), full(w1k), full(b1k), full(w2k),
                  full(pev), full(w1v), full(b1v), full(w2v)],
        out_specs=[out_spec, out_spec_t],
        out_shape=[jax.ShapeDtypeStruct((B, nc, LANES), BF16), jax.ShapeDtypeStruct((B, LANES, nc), BF16)],
        compiler_params=_cparams(("parallel",)),
    )(proj, proj, pek, w1k, b1k, w2k, pev, w1v, b1v, w2v)


def _compress_weights(pe, w1, b1, w2):
    G = NSA_KV_HEADS
    pe2 = jnp.tile(pe, (1, G))
    w1l = w1.astype(BF16).reshape(CMP_LEN, HEAD_DIM, CMP_HIDDEN)
    w2b = w2.astype(BF16)
    z1 = jnp.zeros_like(w1l)
    z2 = jnp.zeros_like(w2b)
    w1_bd = jnp.concatenate([jnp.concatenate([w1l, z1], axis=2), jnp.concatenate([z1, w1l], axis=2)], axis=1)
    w2_bd = jnp.concatenate([jnp.concatenate([w2b, z2], axis=1), jnp.concatenate([z2, w2b], axis=1)], axis=0)
    return pe2, w1_bd, jnp.tile(b1.reshape(1, -1), (1, G)), w2_bd


def _nsa_kernel(qn_ref, qr_ref, kc_ref, vct_ref, ks_ref, vst_ref, kw_ref, vwt_ref, gt_ref, ot_ref,
                o_ref, acc_ref, m_ref, l_ref, *, S):
    i = pl.program_id(1)
    nc = S // CMP_STRIDE
    ns = S // SEL_BLOCK
    k_top = min(SEL_TOPK, ns)
    G = NSA_KV_HEADS
    R = H_NSA // G
    RT = R * TQ
    HD = HEAD_DIM
    low = lax.broadcasted_iota(jnp.int32, (TQ, LANES), 1) < HD
    gates_t = gt_ref[...].T
    qn = qn_ref[...] * SCALE
    qr = qr_ref[...]

    def tiled(mask, s, fill):
        return jnp.concatenate([jnp.where(mask, s[:, r * TQ:(r + 1) * TQ], fill) for r in range(R)], axis=1)

    def stack(qfull, g):
        in_g = low if g == 0 else jnp.logical_not(low)
        parts = []
        for r in range(R):
            a, b = divmod(R * g + r, 2)
            blk = qfull[:, a * LANES:(a + 1) * LANES]
            if b != g:
                blk = pltpu.roll(blk, HD, axis=1)
            parts.append(jnp.where(in_g, blk, 0.0))
        return jnp.concatenate(parts, axis=0).astype(BF16)

    def rows_of(g, x):
        return x[g * HD:(g + 1) * HD, :]


    qn_s = [stack(qn, g) for g in range(G)]
    c_valid = (lax.broadcasted_iota(jnp.int32, (nc, TQ), 0) * CMP_STRIDE + (CMP_LEN - 1)
               <= lax.broadcasted_iota(jnp.int32, (nc, TQ), 1) + i * TQ)
    scs = [tiled(c_valid, _dot_nt(kc_ref[0], qn_s[g]), NEG) for g in range(G)]
    es = [tiled(c_valid, jnp.exp(sc - jnp.max(sc, axis=0, keepdims=True)), 0.0) for sc in scs]
    dens = [jnp.sum(e, axis=0, keepdims=True) for e in es]
    p_cmps = [e / jnp.where(den > 0.0, den, 1.0) for e, den in zip(es, dens)]
    vct = vct_ref[0]
    o_cmps = [_dot(rows_of(g, vct), p_cmps[g].astype(BF16)) for g in range(G)]

    p_slcs = []
    for p in p_cmps:
        p_sum = p[:, 0:TQ]
        for r in range(1, R):
            p_sum = p_sum + p[:, r * TQ:(r + 1) * TQ]
        p_slcs.append(_dot(ot_ref[...], _split_bf16(p_sum, 0)))
    n_idx = lax.broadcasted_iota(jnp.int32, (ns, TQ), 0)
    t_s = lax.broadcasted_iota(jnp.int32, (ns, TQ), 1) + i * TQ
    forced = jnp.logical_or(n_idx == 0, n_idx == (t_s >> SEL_SHIFT))
    s_valid = n_idx * SEL_BLOCK <= t_s
    scores = [jnp.where(forced, 1e4, jnp.where(s_valid, p_slc, -1.0)) for p_slc in p_slcs]
    ranks = [jnp.zeros((ns, TQ), F32) for _ in range(G)]
    for m_i in range(ns):
        for g in range(G):
            sm = scores[g][m_i:m_i + 1, :]
            ge = jnp.where(sm >= scores[g], 1.0, 0.0)
            gt = jnp.where(sm > scores[g], 1.0, 0.0)
            ranks[g] = ranks[g] + jnp.where(n_idx > m_i, ge, gt)
    sels = []
    for g in range(G):
        sel = jnp.where(ranks[g] < k_top, 1.0, 0.0)
        if ns < LANES:
            sel = jnp.concatenate([sel, jnp.zeros((LANES - ns, TQ), F32)], axis=0)
        sels.append(sel.astype(BF16))

    qr_s = [stack(qr, g) for g in range(G)]
    m_ref[...] = jnp.full(m_ref.shape, NEG, F32)
    l_ref[...] = jnp.zeros_like(l_ref)
    acc_ref[...] = jnp.zeros_like(acc_ref)

    def attend(items):
        stats = []
        for c, s, _ in items:
            m_old = m_ref[c]
            stats.append((m_old, jnp.maximum(m_old, jnp.max(s, axis=0, keepdims=True))))
        ps = [jnp.exp(s - m_new) for (_, s, _), (_, m_new) in zip(items, stats)]
        for (c, _, vt), (m_old, m_new), p in zip(items, stats, ps):
            alpha = jnp.exp(m_old - m_new)
            m_ref[c] = m_new
            l_ref[c] = alpha * l_ref[c] + jnp.sum(p, axis=0, keepdims=True)
            acc_ref[c] = alpha * acc_ref[c] + _dot(vt, p.astype(BF16))

    def chosen(g, off, width):
        blk = (lax.broadcasted_iota(jnp.int32, (width, LANES), 0) + off) >> SEL_SHIFT
        expand = jnp.where(blk == lax.broadcasted_iota(jnp.int32, (width, LANES), 1), 1.0, 0.0).astype(BF16)
        return _dot(expand, sels[g]) > 0.5

    n_win = jnp.minimum(i, (WINDOW + TQ - 2) // LANES) + 1
    n_w = n_win + ((i + 1 - n_win) & 1)

    def win_body(j, carry):
        kb = i - j
        off = pl.multiple_of(kb * LANES, LANES)
        s_pos = lax.broadcasted_iota(jnp.int32, (LANES, TQ), 0) + off
        t_q = lax.broadcasted_iota(jnp.int32, (LANES, TQ), 1) + i * TQ
        causal = s_pos <= t_q
        band = jnp.logical_and(causal, t_q - s_pos < WINDOW)
        ks_t, kw_t = ks_ref[pl.ds(off, LANES), :], kw_ref[pl.ds(off, LANES), :]
        vs_t, vw_t = vst_ref[0, kb], vwt_ref[0, kb]
        items = []
        for g in range(G):
            pick = jnp.logical_and(chosen(g, off, LANES), causal)
            items.append((2 * g, tiled(pick, _dot_nt(ks_t, qr_s[g]), NEG), rows_of(g, vs_t)))
            items.append((2 * g + 1, tiled(band, _dot_nt(kw_t, qr_s[g]), NEG), rows_of(g, vw_t)))
        attend(items)
        return carry

    lax.fori_loop(0, n_w, win_body, 0)

    def sel_body(j, carry):
        kb = i - n_w - 2 * j - 1
        off = pl.multiple_of(kb * LANES, LANES)
        ks_t = ks_ref[pl.ds(off, 2 * LANES), :]
        vs_t = jnp.concatenate([vst_ref[0, kb], vst_ref[0, kb + 1]], axis=1)
        attend([(2 * g, tiled(chosen(g, off, 2 * LANES), _dot_nt(ks_t, qr_s[g]), NEG), rows_of(g, vs_t))
                for g in range(G)])
        return carry

    lax.fori_loop(0, (i + 1 - n_w) // 2, sel_body, 0)

    heads = [None] * H_NSA
    for g in range(G):
        l_sel, l_win = l_ref[2 * g], l_ref[2 * g + 1]
        o_sel = acc_ref[2 * g] / jnp.where(l_sel > 0.0, l_sel, 1.0)
        o_win = acc_ref[2 * g + 1] / jnp.where(l_win > 0.0, l_win, 1.0)
        for r in range(R):
            h = R * g + r
            cols = slice(r * TQ, (r + 1) * TQ)
            heads[h] = (gates_t[3 * h:3 * h + 1, :] * o_cmps[g][:, cols]
                        + gates_t[3 * h + 1:3 * h + 2, :] * o_sel[:, cols]
                        + gates_t[3 * h + 2:3 * h + 3, :] * o_win[:, cols])
    for a in range(H_NSA // 2):
        pair = jnp.concatenate([heads[2 * a], heads[2 * a + 1]], axis=0)
        o_ref[:, a * LANES:(a + 1) * LANES] = pair.T.astype(o_ref.dtype)


def _nsa_attention(proj, q_rope, k_cmp, v_cmp_t, ks_r, vs_t, kw_r, vw_t, gates, overlap_t, B, S):
    nq = S // TQ
    nc = S // CMP_STRIDE
    nk = S // LANES
    G = NSA_KV_HEADS
    rt = H_NSA // G * TQ
    k_full = pl.BlockSpec((S, LANES), lambda b, i: (b, 0))
    vt_full = pl.BlockSpec((1, nk, LANES, LANES), lambda b, i: (b, 0, 0, 0))
    return pl.pallas_call(
        functools.partial(_nsa_kernel, S=S),
        grid=(B, nq),
        in_specs=[pl.BlockSpec((TQ, W_HEADS), lambda b, i: (b * nq + i, C_NQ // W_HEADS)),
                  pl.BlockSpec((TQ, W_HEADS), lambda b, i: (b * nq + i, 0)),
                  pl.BlockSpec((1, nc, LANES), lambda b, i: (b, 0, 0)),
                  pl.BlockSpec((1, LANES, nc), lambda b, i: (b, 0, 0)),
                  k_full, vt_full, k_full, vt_full,
                  pl.BlockSpec((TQ, LANES), lambda b, i: (b * nq + i, 0)),
                  pl.BlockSpec(overlap_t.shape, lambda b, i: (0, 0))],
        out_specs=pl.BlockSpec((TQ, W_HEADS), lambda b, i: (b * nq + i, 0)),
        out_shape=jax.ShapeDtypeStruct((B * S, W_HEADS), BF16),
        scratch_shapes=[pltpu.VMEM((2 * G, HEAD_DIM, rt), F32), pltpu.VMEM((2 * G, 1, rt), F32),
                        pltpu.VMEM((2 * G, 1, rt), F32)],
        compiler_params=_cparams(("parallel", "arbitrary")),
    )(proj, q_rope, k_cmp, v_cmp_t, ks_r, vs_t, kw_r, vw_t, gates, overlap_t)


def _merge_kernel(x_ref, osb_ref, onsa_ref, ofox_ref, m0_ref, m1_ref, m2_ref,
                  wsb_ref, wnsa_ref, wfox_ref, wout_ref, g_ref, o_ref):
    y = jax.nn.sigmoid(m0_ref[...]) * _dot(osb_ref[...], wsb_ref[...])
    y = y + jax.nn.sigmoid(m1_ref[...]) * _dot(onsa_ref[...], wnsa_ref[...])
    y = y + jax.nn.sigmoid(m2_ref[...]) * _dot(ofox_ref[...], wfox_ref[...])
    z = _dot(y.astype(BF16), wout_ref[...])
    o_ref[...] = x_ref[...] + _rms(z, g_ref[...])


def _merge(x, o_sb, o_nsa, o_fox, proj, w_sb, w_nsa, w_fox, w_out, g, tm):
    T, D = x.shape
    row = lambda w: pl.BlockSpec((tm, w), lambda i: (i, 0))
    full = lambda a: pl.BlockSpec(a.shape, lambda i: (0, 0))
    gate = lambda c: pl.BlockSpec((tm, D), lambda i: (i, C_MERGE // D + c))
    return pl.pallas_call(
        _merge_kernel,
        grid=(T // tm,),
        in_specs=[row(D), row(W_HEADS), row(W_HEADS), row(W_HEADS), gate(0), gate(1), gate(2),
                  full(w_sb), full(w_nsa), full(w_fox), full(w_out), full(g)],
        out_specs=row(D),
        out_shape=jax.ShapeDtypeStruct((T, D), F32),
        compiler_params=_cparams(("parallel",)),
    )(x, o_sb, o_nsa, o_fox, proj, proj, proj, w_sb, w_nsa, w_fox, w_out, g)


def _mem_kv_kernel(mem_ref, g_ref, wk_ref, wv_ref, k_ref, v_ref):
    mn = _rms(mem_ref[0], g_ref[...]).astype(BF16)
    k_ref[0] = _dot(mn, wk_ref[...]).astype(BF16)
    v_ref[0] = _dot(mn, wv_ref[...]).astype(BF16)


def _mem_kv(mem, g, wk, wv):
    B, M, D = mem.shape
    full = lambda a: pl.BlockSpec(a.shape, lambda b: (0, 0))
    out_spec = pl.BlockSpec((1, M, W_MEM), lambda b: (b, 0, 0))
    return pl.pallas_call(
        _mem_kv_kernel,
        grid=(B,),
        in_specs=[pl.BlockSpec((1, M, D), lambda b: (b, 0, 0)), full(g), full(wk), full(wv)],
        out_specs=[out_spec, out_spec],
        out_shape=[jax.ShapeDtypeStruct((B, M, W_MEM), BF16)] * 2,
        compiler_params=_cparams(("parallel",)),
    )(mem, g, wk, wv)


def _mem_attn_kernel(x_ref, g_ref, wq_ref, k_ref, v_ref, wo_ref, gp_ref, o_ref):
    x = x_ref[...]
    q = _dot(_rms(x, g_ref[...]).astype(BF16), wq_ref[...]) * SCALE
    k = k_ref[0]
    v = v_ref[0]
    lane = lax.broadcasted_iota(jnp.int32, q.shape, 1)
    o = jnp.zeros(q.shape, F32)
    for h in range(H_MEM):
        in_h = jnp.logical_and(lane >= h * HEAD_DIM, lane < (h + 1) * HEAD_DIM)
        s = _dot_nt(jnp.where(in_h, q, 0.0).astype(BF16), k)
        e = jnp.exp(s - jnp.max(s, axis=-1, keepdims=True))
        p = e / jnp.sum(e, axis=-1, keepdims=True)
        o = o + jnp.where(in_h, _dot(p.astype(BF16), v), 0.0)
    y = _dot(o.astype(BF16), wo_ref[...])
    o_ref[...] = x + _rms(y, gp_ref[...])


def _mem_attn(x, g_pre, wq, k, v, wo, g_post, B, S, tm):
    T, D = x.shape
    M = k.shape[1]
    nt = S // tm
    full = lambda a: pl.BlockSpec(a.shape, lambda b, i: (0, 0))
    kv_spec = pl.BlockSpec((1, M, W_MEM), lambda b, i: (b, 0, 0))
    row = pl.BlockSpec((tm, D), lambda b, i: (b * nt + i, 0))
    return pl.pallas_call(
        _mem_attn_kernel,
        grid=(B, nt),
        in_specs=[row, full(g_pre), full(wq), kv_spec, kv_spec, full(wo), full(g_post)],
        out_specs=row,
        out_shape=jax.ShapeDtypeStruct((T, D), F32),
        compiler_params=_cparams(("parallel", "parallel")),
    )(x, g_pre, wq, k, v, wo, g_post)


def _ffn_kernel(x_ref, g_ref, wg_ref, wu_ref, wd_ref, gp_ref, o_ref, h_ref, acc_ref):
    j = pl.program_id(1)

    @pl.when(j == 0)
    def _():
        h_ref[...] = _rms(x_ref[...], g_ref[...]).astype(BF16)
        acc_ref[...] = jnp.zeros_like(acc_ref)

    h = h_ref[...]
    a = _dot(h, wg_ref[...])
    u = _dot(h, wu_ref[...])
    acc_ref[...] += _dot((a * jax.nn.sigmoid(a) * u).astype(BF16), wd_ref[...])

    @pl.when(j == pl.num_programs(1) - 1)
    def _():
        o_ref[...] = x_ref[...] + _rms(acc_ref[...], gp_ref[...])


def _ffn(x, g_pre, wg, wu, wd, g_post, tm, tf):
    T, D = x.shape
    F = wg.shape[1]
    row = pl.BlockSpec((tm, D), lambda i, j: (i, 0))
    vec = pl.BlockSpec((1, D), lambda i, j: (0, 0))
    return pl.pallas_call(
        _ffn_kernel,
        grid=(T // tm, F // tf),
        in_specs=[row, vec,
                  pl.BlockSpec((D, tf), lambda i, j: (0, j)),
                  pl.BlockSpec((D, tf), lambda i, j: (0, j)),
                  pl.BlockSpec((tf, D), lambda i, j: (j, 0)),
                  vec],
        out_specs=row,
        out_shape=jax.ShapeDtypeStruct((T, D), F32),
        scratch_shapes=[pltpu.VMEM((tm, D), BF16), pltpu.VMEM((tm, D), F32)],
        compiler_params=_cparams(("parallel", "arbitrary")),
    )(x, g_pre, wg, wu, wd, g_post)


def _scan_matrix():
    j = np.arange(LANES)
    later = (j[:, None] > j[None, :]).astype(np.float32)
    u = np.concatenate([later, np.ones((LANES, LANES), np.float32)], axis=1)
    return jnp.asarray(np.concatenate([u, u], axis=0), dtype=BF16)


def _overlap_t(S):
    nc, ns = S // CMP_STRIDE, S // SEL_BLOCK
    c0 = np.arange(nc) * CMP_STRIDE
    n0 = np.arange(ns) * SEL_BLOCK
    ov = (c0[None, :] < n0[:, None] + SEL_BLOCK) & (c0[None, :] + CMP_LEN > n0[:, None])
    ov = ov & (np.arange(nc)[None, :] < nc - 1)
    ov = ov.astype(np.float32)
    return jnp.asarray(np.concatenate([ov, ov], axis=1), dtype=BF16)


FOX_F_LANE = 24
N_PARTS = 3


def _fox_bias_placement():
    n_pair = H_FOX // 2
    pa = np.zeros((N_PARTS * LANES, n_pair * 2 * LANES), np.float32)
    pb = np.zeros((N_PARTS * LANES, n_pair * LANES), np.float32)
    oa = np.zeros((1, n_pair * 2 * LANES), np.float32)
    ob = np.zeros((1, n_pair * LANES), np.float32)
    for p in range(n_pair):
        for e in range(2):
            src = FOX_F_LANE + 2 * p + e
            for x in range(N_PARTS):
                pa[x * LANES + src, (2 * p + e) * LANES + 8 * e + x] = 1.0
                oa[0, (2 * p + e) * LANES + 8 * e + N_PARTS + x] = 1.0
                pb[x * LANES + src, p * LANES + 8 * e + N_PARTS + x] = -1.0
                ob[0, p * LANES + 8 * e + x] = 1.0
    return (jnp.asarray(pa, dtype=BF16), jnp.asarray(pb, dtype=BF16), jnp.asarray(oa), jnp.asarray(ob))


def _rope_rows():
    half = ROPE_DIM // 2
    inv_freq = ROPE_THETA ** (-jnp.arange(half, dtype=F32) / half)
    d = np.arange(LANES) % HEAD_DIM
    rot = d < ROPE_DIM
    freq = jnp.where(jnp.asarray(rot), inv_freq[jnp.asarray(d % half)], 0.0)
    s_up = jnp.asarray(((d >= half) & rot).astype(np.float32))
    s_dn = jnp.asarray(-(d < half).astype(np.float32))
    rows = jnp.stack([freq, s_up, s_dn] + [jnp.zeros((LANES,), F32)] * 5)
    return rows.astype(F32)


def _reorder_w_in(w):
    pad = jnp.zeros((w.shape[0], MISC_W - (_O_FOX_Q - _O_NSA_G) - (_O_MERGE - _O_FOX_F)), w.dtype)
    return jnp.concatenate([
        w[:, _O_MERGE:_O_END],
        w[:, :_O_NSA_G - 6 * LANES],
        w[:, _O_FOX_Q:_O_FOX_F],
        w[:, _O_NSA_G - 6 * LANES:_O_NSA_G],
        w[:, _O_NSA_G:_O_FOX_Q],
        w[:, _O_FOX_F:_O_MERGE],
        pad], axis=1)


def _row_tile(n, target):
    t = min(n, target)
    while n % t:
        t //= 2
    return t


def kernel(x, mem, positions, g_pre_mix, g_post_mix, g_pre_mem, g_mem, g_post_mem, g_pre_ffn, g_post_ffn,
           w_in, b_fox_f, cmp_pe_k, cmp_w1_k, cmp_b1_k, cmp_w2_k, cmp_pe_v, cmp_w1_v, cmp_b1_v, cmp_w2_v,
           w_up_sb, w_up_nsa, w_up_fox, w_out, w_mem_q, w_mem_k, w_mem_v, w_mem_o,
           w_ffn_gate, w_ffn_up, w_ffn_down):
    B, S, D = x.shape
    T = B * S
    depth = w_in.shape[0]
    u2 = _scan_matrix()
    overlap_t = _overlap_t(S)
    rope_c = _rope_rows()
    place = _fox_bias_placement()
    pos3 = positions.reshape(B, S, 1)
    vec = lambda g: g.reshape(1, -1)
    tm_big = _row_tile(T, 1024)
    tm_mid = _row_tile(T, 512)
    d_ff = w_ffn_gate.shape[2]
    tf = d_ff // 2 if (d_ff // 2) % LANES == 0 else d_ff

    w_in_b = w_in.astype(BF16)
    xf = x.reshape(T, D)
    for l in range(depth):
        proj = _norm_matmul(xf, vec(g_pre_mix[l]), _reorder_w_in(w_in_b[l]), tm_big, N_IN // 5)

        o_sb = _sb_attention(proj, u2, B, S)

        bias_row = jnp.zeros((1, LANES), F32).at[0, FOX_F_LANE:FOX_F_LANE + H_FOX].set(b_fox_f[l])
        gates, fox_a, fox_b, q_rope, ks_r, kw_r, v_fox_t, vs_t, vw_t = _prep(
            proj, bias_row, pos3, rope_c, place, B, S)
        o_fox = _fox_attention(proj, v_fox_t, fox_a, fox_b, B, S)

        k_cmp, v_cmp_t = _compress(
            proj, *_compress_weights(cmp_pe_k[l], cmp_w1_k[l], cmp_b1_k[l], cmp_w2_k[l]),
            *_compress_weights(cmp_pe_v[l], cmp_w1_v[l], cmp_b1_v[l], cmp_w2_v[l]), B, S)
        o_nsa = _nsa_attention(proj, q_rope, k_cmp, v_cmp_t, ks_r, vs_t, kw_r, vw_t, gates, overlap_t, B, S)

        xf = _merge(xf, o_sb, o_nsa, o_fox, proj, w_up_sb[l].astype(BF16), w_up_nsa[l].astype(BF16),
                    w_up_fox[l].astype(BF16), w_out[l].astype(BF16), vec(g_post_mix[l]), tm_mid)

        k_mem, v_mem = _mem_kv(mem, vec(g_mem[l]), w_mem_k[l].astype(BF16), w_mem_v[l].astype(BF16))
        xf = _mem_attn(xf, vec(g_pre_mem[l]), w_mem_q[l].astype(BF16), k_mem, v_mem,
                       w_mem_o[l].astype(BF16), vec(g_post_mem[l]), B, S, _row_tile(S, 512))

        xf = _ffn(xf, vec(g_pre_ffn[l]), w_ffn_gate[l].astype(BF16), w_ffn_up[l].astype(BF16),
                  w_ffn_down[l].astype(BF16), vec(g_post_ffn[l]), tm_mid, tf)
    return xf.reshape(B, S, D)
```

```python
import functools

import numpy as np
import jax
import jax.numpy as jnp
from jax import lax
from jax.experimental import pallas as pl
from jax.experimental.pallas import tpu as pltpu

F32 = jnp.float32
BF16 = jnp.bfloat16

D_MODEL = 1024
HEAD_DIM = 64
H_SB = 8
H_NSA = 8
NSA_KV_HEADS = 2
H_FOX = 8
H_MEM = 4
N_BRANCH = 3
ROPE_THETA = 500000.0
ROPE_DIM = HEAD_DIM // 4
CMP_STRIDE = 16
CMP_LEN = 2 * CMP_STRIDE
CMP_HIDDEN = 256
SEL_BLOCK = 64
SEL_SHIFT = SEL_BLOCK.bit_length() - 1
SEL_TOPK = 8
WINDOW = 512
W_HEADS = 8 * HEAD_DIM
W_MEM = H_MEM * HEAD_DIM
EPS = 1e-6
SCALE = HEAD_DIM ** -0.5
NEG = -1e30
LOG_ZERO = -104.0

LANES = 128
TQ = 128
TQA = 256

C_MERGE = 0
C_SBQ, C_SBK, C_SBV = 3072, 3584, 4096
C_NQ = 4608
C_FQ, C_FK, C_FV = 5120, 5632, 6144
C_NKV = 6656
C_MISC = 7424
MISC_W = 256
N_IN = 7680
_O_NSA_G, _O_FOX_Q, _O_FOX_F, _O_MERGE, _O_END = 2816, 2840, 4376, 4384, 7456

VMEM_LIMIT = 56 * 1024 * 1024


def _cparams(sem):
    return pltpu.CompilerParams(dimension_semantics=sem, vmem_limit_bytes=VMEM_LIMIT)


def _dot(a, b):
    return jnp.dot(a, b, preferred_element_type=F32)


def _dot_nt(a, b):
    return lax.dot_general(a, b, (((1,), (1,)), ((), ())), preferred_element_type=F32)


def _rms(x, g):
    ms = jnp.mean(x * x, axis=-1, keepdims=True)
    return x * lax.rsqrt(ms + EPS) * g


def _split_bf16(x, axis=1):
    hi = x.astype(BF16)
    lo = (x - hi.astype(F32)).astype(BF16)
    return jnp.concatenate([hi, lo], axis=axis)


def _norm_matmul_kernel(x_ref, g_ref, w_ref, o_ref, h_ref):
    @pl.when(pl.program_id(1) == 0)
    def _():
        h_ref[...] = _rms(x_ref[...], g_ref[...]).astype(BF16)

    o_ref[...] = _dot(h_ref[...], w_ref[...])


def _norm_matmul(x, g, w, tm, tn):
    T, D = x.shape
    N = w.shape[1]
    return pl.pallas_call(
        _norm_matmul_kernel,
        grid=(T // tm, N // tn),
        in_specs=[pl.BlockSpec((tm, D), lambda i, j: (i, 0)),
                  pl.BlockSpec((1, D), lambda i, j: (0, 0)),
                  pl.BlockSpec((D, tn), lambda i, j: (0, j))],
        out_specs=pl.BlockSpec((tm, tn), lambda i, j: (i, j)),
        out_shape=jax.ShapeDtypeStruct((T, N), F32),
        scratch_shapes=[pltpu.VMEM((tm, D), BF16)],
        compiler_params=_cparams(("parallel", "arbitrary")),
    )(x, g, w)


def _stack_pair(q, low):
    return jnp.concatenate([jnp.where(low, q, 0.0), jnp.where(low, 0.0, q)], axis=0).astype(BF16)


def _sb_kernel(q_ref, k_ref, v_ref, u_ref, o_ref, acc_ref, cs_ref, arg_ref, rs_ref):
    i = pl.program_id(2)
    rt = 2 * TQA
    n_grp = TQA // LANES
    lane = lax.broadcasted_iota(jnp.int32, (rt, LANES), 1)
    t_q = (lax.broadcasted_iota(jnp.int32, (rt, LANES), 0) & (TQA - 1)) + i * TQA
    low = lax.broadcasted_iota(jnp.int32, (TQA, LANES), 1) < HEAD_DIM
    qs = _stack_pair(q_ref[...] * SCALE, low)

    def offsets(kb_first):
        return [pl.multiple_of((kb_first - d) * LANES, LANES) for d in range(n_grp)]

    def logits(offs):
        return [_dot_nt(qs, k_ref[pl.ds(off, LANES), :].astype(BF16)) for off in offs]

    def scan(offs, zs, on_diagonal):
        log_keeps, log_betas, stricts = [], [], []
        for off, z in zip(offs, zs):
            log_keep = -(jnp.maximum(z, 0.0) + jnp.log(1.0 + jnp.exp(-jnp.abs(z))))
            log_betas.append(z + log_keep)
            if on_diagonal:
                stricts.append((lane + off) < t_q)
                log_keep = jnp.where(stricts[-1], log_keep, 0.0)
            log_keeps.append(log_keep)
        c2s = [_dot(_split_bf16(log_keep), u_ref[...]) for log_keep in log_keeps]
        args = [log_beta + c2[:, :LANES] for log_beta, c2 in zip(log_betas, c2s)]
        if on_diagonal:
            args = [jnp.where(strict, arg, NEG) for strict, arg in zip(stricts, args)]
        return args, [c2[:, LANES:] for c2 in c2s]

    def stash(args, row_sums):
        for d in range(n_grp):
            arg_ref[d] = args[d]
            rs_ref[d] = row_sums[d]

    def weights(offs, args, row_sums):
        cs = cs_ref[...]
        pvs = []
        for d, off in enumerate(offs):
            w = jnp.exp(args[d] + cs)
            pvs.append(_dot(w.astype(BF16), v_ref[pl.ds(off, LANES), :].astype(BF16)))
            cs = cs + row_sums[d]
        return cs, pvs

    def accumulate(cs, pvs):
        acc = acc_ref[...]
        for pv in pvs:
            acc = acc + pv
        acc_ref[...] = acc
        cs_ref[...] = cs

    acc_ref[...] = jnp.zeros_like(acc_ref)
    cs_ref[...] = jnp.zeros_like(cs_ref)
    first = offsets(n_grp * i + n_grp - 1)
    stash(*scan(first, logits(first), True))

    def cond(carry):
        j, dead = carry
        return jnp.logical_and(j < i, dead == 0)

    def body(carry):
        j, _ = carry
        cur = offsets(n_grp * (i - j) + n_grp - 1)
        nxt = offsets(n_grp * (i - j) - 1)
        args = [arg_ref[d] for d in range(n_grp)]
        row_sums = [rs_ref[d] for d in range(n_grp)]
        zs = logits(nxt)
        cs, pvs = weights(cur, args, row_sums)
        nxt_args, nxt_sums = scan(nxt, zs, False)
        accumulate(cs, pvs)
        stash(nxt_args, nxt_sums)
        return j + 1, (jnp.max(cs) < LOG_ZERO).astype(jnp.int32)

    _, dead = lax.while_loop(cond, body, (jnp.int32(0), jnp.int32(0)))

    @pl.when(dead == 0)
    def _():
        accumulate(*weights(offsets(n_grp - 1), [arg_ref[d] for d in range(n_grp)],
                            [rs_ref[d] for d in range(n_grp)]))

    acc = acc_ref[...]
    o_ref[...] = jnp.where(low, acc[:TQA], acc[TQA:]).astype(o_ref.dtype)


def _sb_attention(proj, u2, B, S):
    nq = S // TQA
    cq, ck, cv = C_SBQ // LANES, C_SBK // LANES, C_SBV // LANES
    return pl.pallas_call(
        _sb_kernel,
        grid=(B, H_SB // 2, nq),
        in_specs=[pl.BlockSpec((TQA, LANES), lambda b, p, i: (b * nq + i, cq + p)),
                  pl.BlockSpec((S, LANES), lambda b, p, i: (b, ck + p)),
                  pl.BlockSpec((S, LANES), lambda b, p, i: (b, cv + p)),
                  pl.BlockSpec((2 * LANES, 2 * LANES), lambda b, p, i: (0, 0))],
        out_specs=pl.BlockSpec((TQA, LANES), lambda b, p, i: (b * nq + i, p)),
        out_shape=jax.ShapeDtypeStruct((B * S, W_HEADS), BF16),
        scratch_shapes=[pltpu.VMEM((2 * TQA, LANES), F32), pltpu.VMEM((2 * TQA, LANES), F32),
                        pltpu.VMEM((TQA // LANES, 2 * TQA, LANES), F32),
                        pltpu.VMEM((TQA // LANES, 2 * TQA, LANES), F32)],
        compiler_params=_cparams(("parallel", "parallel", "arbitrary")),
    )(proj, proj, proj, u2)


def _fox_kernel(q_ref, k_ref, vt_ref, fa_ref, fb_ref, o_ref, acc_ref, m_ref, l_ref, s_ref, cmax_ref):
    i = pl.program_id(2)
    rt = 2 * TQA
    low = lax.broadcasted_iota(jnp.int32, (TQA, LANES), 1) < HEAD_DIM
    qs = _stack_pair(q_ref[...] * SCALE, low)
    qa = jnp.concatenate([qs, jnp.concatenate([fa_ref[:, :LANES], fa_ref[:, LANES:]], axis=0)], axis=1)
    acc_ref[...] = jnp.zeros_like(acc_ref)
    l_ref[...] = jnp.zeros_like(l_ref)
    m_ref[...] = jnp.full(m_ref.shape, NEG, F32)

    def scores(kb, on_diagonal):
        off = pl.multiple_of(kb * TQA, TQA)
        kbias = jnp.concatenate([k_ref[pl.ds(off, TQA), :].astype(BF16), fb_ref[pl.ds(off, TQA), :]], axis=1)
        s = _dot_nt(kbias, qa)
        if on_diagonal:
            causal = lax.broadcasted_iota(jnp.int32, (TQA, rt), 0) <= (
                lax.broadcasted_iota(jnp.int32, (TQA, rt), 1) & (TQA - 1))
            s = jnp.where(causal, s, NEG)
        return s

    def stash(s):
        s_ref[...] = s
        cmax_ref[...] = jnp.max(s, axis=0, keepdims=True)

    def absorb(kb, s, cmax):
        m_old = m_ref[...]
        m_new = jnp.maximum(m_old, cmax)
        p = jnp.exp(s - m_new)
        alpha = jnp.exp(m_old - m_new)
        m_ref[...] = m_new
        l_ref[...] = alpha * l_ref[...] + jnp.sum(p, axis=0, keepdims=True)
        p = p.astype(BF16)
        vt = vt_ref[0, 0, kb]
        return alpha, [_dot(vt[e * HEAD_DIM:(e + 1) * HEAD_DIM, :], p[:, e * TQA:(e + 1) * TQA]) for e in range(2)]

    def accumulate(alpha, pvs):
        for e in range(2):
            acc_ref[e] = alpha[:, e * TQA:(e + 1) * TQA] * acc_ref[e] + pvs[e]

    stash(scores(i, True))

    def body(j, carry):
        kb = i - j
        s_cur, cmax = s_ref[...], cmax_ref[...]
        s_next = scores(kb - 1, False)
        alpha, pvs = absorb(kb, s_cur, cmax)
        stash(s_next)
        accumulate(alpha, pvs)
        return carry

    lax.fori_loop(0, i, body, 0)
    accumulate(*absorb(0, s_ref[...], cmax_ref[...]))
    l = l_ref[...]
    den = jnp.where(l > 0.0, l, 1.0)
    o_t = jnp.concatenate([acc_ref[0] / den[:, :TQA], acc_ref[1] / den[:, TQA:]], axis=0)
    o_ref[...] = o_t.T.astype(o_ref.dtype)


def _fox_attention(proj, v_t, fox_a, fox_b, B, S):
    nq = S // TQA
    cq, ck = C_FQ // LANES, C_FK // LANES
    return pl.pallas_call(
        _fox_kernel,
        grid=(B, H_FOX // 2, nq),
        in_specs=[pl.BlockSpec((TQA, LANES), lambda b, p, i: (b * nq + i, cq + p)),
                  pl.BlockSpec((S, LANES), lambda b, p, i: (b, ck + p)),
                  pl.BlockSpec((1, 1, nq, LANES, TQA), lambda b, p, i: (b, p, 0, 0, 0)),
                  pl.BlockSpec((TQA, 2 * LANES), lambda b, p, i: (b * nq + i, p)),
                  pl.BlockSpec((S, LANES), lambda b, p, i: (b, p))],
        out_specs=pl.BlockSpec((TQA, LANES), lambda b, p, i: (b * nq + i, p)),
        out_shape=jax.ShapeDtypeStruct((B * S, W_HEADS), BF16),
        scratch_shapes=[pltpu.VMEM((2, HEAD_DIM, TQA), F32), pltpu.VMEM((1, 2 * TQA), F32),
                        pltpu.VMEM((1, 2 * TQA), F32), pltpu.VMEM((TQA, 2 * TQA), F32),
                        pltpu.VMEM((1, 2 * TQA), F32)],
        compiler_params=_cparams(("parallel", "parallel", "arbitrary")),
    )(proj, proj, v_t, fox_a, fox_b)


def _rope(x, cos, s_up, s_dn):
    return x * cos + pltpu.roll(x, ROPE_DIM // 2, axis=1) * s_up + pltpu.roll(x, LANES - ROPE_DIM // 2, axis=1) * s_dn


def _prep_kernel(misc_ref, bias_ref, pos_ref, rc_ref, q_ref, ks_ref, kw_ref, pa_ref, pb_ref, oa_ref, ob_ref,
                 vf_ref, vs_ref, vw_ref,
                 gates_ref, fa_ref, fb_ref, qr_ref, ksr_ref, kwr_ref, vft_ref, vst_ref, vwt_ref, carry_ref, *, ts):
    @pl.when(pl.program_id(1) == 0)
    def _():
        carry_ref[...] = jnp.zeros_like(carry_ref)

    logits = misc_ref[...]
    gates_ref[...] = jax.nn.sigmoid(logits)
    zf = logits + bias_ref[...]
    x = jnp.minimum(zf, 0.0) - jnp.log(1.0 + jnp.exp(-jnp.abs(zf)))
    row = lax.broadcasted_iota(jnp.int32, (ts, LANES), 0)
    sh = 1
    while sh < ts:
        x = x + jnp.where(row >= sh, pltpu.roll(x, sh, axis=0), 0.0)
        sh *= 2
    x = x + carry_ref[...]
    carry_ref[...] = x[ts - 1:ts, :]
    hi = x.astype(BF16)
    r1 = x - hi.astype(F32)
    mid = r1.astype(BF16)
    lo = (r1 - mid.astype(F32)).astype(BF16)
    parts = jnp.concatenate([hi, mid, lo], axis=1)
    fa_ref[...] = (_dot(parts, pa_ref[...]) + oa_ref[...]).astype(BF16)
    fb_ref[...] = (_dot(parts, pb_ref[...]) + ob_ref[...]).astype(BF16)

    ang = pos_ref[0].astype(F32) * rc_ref[0:1, :]
    cos = jnp.cos(ang)
    sin = jnp.sin(ang)
    s_up = sin * rc_ref[1:2, :]
    s_dn = sin * rc_ref[2:3, :]
    for a in range(W_HEADS // LANES):
        qa = q_ref[:, a * LANES:(a + 1) * LANES]
        qr_ref[:, a * LANES:(a + 1) * LANES] = _rope(qa, cos, s_up, s_dn) * SCALE
    ksr_ref[...] = _rope(ks_ref[...], cos, s_up, s_dn).astype(BF16)
    kwr_ref[...] = _rope(kw_ref[...], cos, s_up, s_dn).astype(BF16)

    for p in range(H_FOX // 2):
        for u in range(ts // TQA):
            vft_ref[0, p, u] = vf_ref[u * TQA:(u + 1) * TQA, p * LANES:(p + 1) * LANES].T.astype(BF16)
    for u in range(ts // LANES):
        vst_ref[0, u] = vs_ref[u * LANES:(u + 1) * LANES, :].T.astype(BF16)
        vwt_ref[0, u] = vw_ref[u * LANES:(u + 1) * LANES, :].T.astype(BF16)


def _prep(proj, bias_row, pos3, rope_c, place, B, S):
    ts = min(512, S)
    nt = S // ts
    T = B * S
    row_blk = lambda w: pl.BlockSpec((ts, w), lambda b, t: (b * nt + t, 0))
    full = lambda a: pl.BlockSpec(a.shape, lambda b, t: (0, 0))
    wa, wb = place[0].shape[1], place[1].shape[1]
    return pl.pallas_call(
        functools.partial(_prep_kernel, ts=ts),
        grid=(B, nt),
        in_specs=[pl.BlockSpec((ts, LANES), lambda b, t: (b * nt + t, C_MISC // LANES)),
                  pl.BlockSpec((1, LANES), lambda b, t: (0, 0)),
                  pl.BlockSpec((1, ts, 1), lambda b, t: (b, t, 0)),
                  pl.BlockSpec((8, LANES), lambda b, t: (0, 0)),
                  pl.BlockSpec((ts, W_HEADS), lambda b, t: (b * nt + t, C_NQ // W_HEADS)),
                  pl.BlockSpec((ts, LANES), lambda b, t: (b * nt + t, C_NKV // LANES + 2)),
                  pl.BlockSpec((ts, LANES), lambda b, t: (b * nt + t, C_NKV // LANES + 4)),
                  full(place[0]), full(place[1]), full(place[2]), full(place[3]),
                  pl.BlockSpec((ts, W_HEADS), lambda b, t: (b * nt + t, C_FV // W_HEADS)),
                  pl.BlockSpec((ts, LANES), lambda b, t: (b * nt + t, C_NKV // LANES + 3)),
                  pl.BlockSpec((ts, LANES), lambda b, t: (b * nt + t, C_NKV // LANES + 5))],
        out_specs=[row_blk(LANES), row_blk(wa), row_blk(wb), row_blk(W_HEADS), row_blk(LANES), row_blk(LANES),
                   pl.BlockSpec((1, H_FOX // 2, ts // TQA, LANES, TQA), lambda b, t: (b, 0, t, 0, 0)),
                   pl.BlockSpec((1, ts // LANES, LANES, LANES), lambda b, t: (b, t, 0, 0)),
                   pl.BlockSpec((1, ts // LANES, LANES, LANES), lambda b, t: (b, t, 0, 0))],
        out_shape=[jax.ShapeDtypeStruct((T, LANES), F32),
                   jax.ShapeDtypeStruct((T, wa), BF16),
                   jax.ShapeDtypeStruct((T, wb), BF16),
                   jax.ShapeDtypeStruct((T, W_HEADS), F32),
                   jax.ShapeDtypeStruct((T, LANES), BF16),
                   jax.ShapeDtypeStruct((T, LANES), BF16),
                   jax.ShapeDtypeStruct((B, H_FOX // 2, S // TQA, LANES, TQA), BF16),
                   jax.ShapeDtypeStruct((B, S // LANES, LANES, LANES), BF16),
                   jax.ShapeDtypeStruct((B, S // LANES, LANES, LANES), BF16)],
        scratch_shapes=[pltpu.VMEM((1, LANES), F32)],
        compiler_params=_cparams(("parallel", "arbitrary")),
    )(proj, bias_row, pos3, rope_c, proj, proj, proj, *place, proj, proj, proj)


def _compress_kernel(kc_ref, vc_ref, pek_ref, w1k_ref, b1k_ref, w2k_ref,
                     pev_ref, w1v_ref, b1v_ref, w2v_ref, ok_ref, ov_ref, *, nc):
    for x_ref, pe_ref, w1_ref, b1_ref, w2_ref, o_ref in (
            (kc_ref, pek_ref, w1k_ref, b1k_ref, w2k_ref, ok_ref),
            (vc_ref, pev_ref, w1v_ref, b1v_ref, w2v_ref, ov_ref)):
        h_first = jnp.zeros((nc, NSA_KV_HEADS * CMP_HIDDEN), F32)
        h_second = jnp.zeros((nc, NSA_KV_HEADS * CMP_HIDDEN), F32)
        for l in range(CMP_STRIDE):
            x = x_ref[pl.ds(l, nc, stride=CMP_STRIDE), :]
            h_first = h_first + _dot((x + pe_ref[l:l + 1, :]).astype(BF16), w1_ref[l])
            h_second = h_second + _dot((x + pe_ref[CMP_STRIDE + l:CMP_STRIDE + l + 1, :]).astype(BF16),
                                       w1_ref[CMP_STRIDE + l])
        h = h_first + pltpu.roll(h_second, nc - 1, axis=0) + b1_ref[...]
        a = h * jax.nn.sigmoid(h)
        out = _dot(a.astype(BF16), w2_ref[...])
        o_ref[0] = (out.T if o_ref is ov_ref else out).astype(o_ref.dtype)


def _compress(proj, pek, w1k, b1k, w2k, pev, w1v, b1v, w2v, B, S):
    nc = S // CMP_STRIDE
    full = lambda a: pl.BlockSpec(a.shape, lambda b: (0,) * a.ndim)
    out_spec = pl.BlockSpec((1, nc, LANES), lambda b: (b, 0, 0))
    out_spec_t = pl.BlockSpec((1, LANES, nc), lambda b: (b, 0, 0))
    return pl.pallas_call(
        functools.partial(_compress_kernel, nc=nc),
        grid=(B,),
        in_specs=[pl.BlockSpec((S, LANES), lambda b: (b, C_NKV // LANES)),
                  pl.BlockSpec((S, LANES), lambda b: (b, C_NKV // LANES + 1)),
                  full(pek), full(w1k), full(b1k), full(w2k),
                  full(pev), full(w1v), full(b1v), full(w2v)],
        out_specs=[out_spec, out_spec_t],
        out_shape=[jax.ShapeDtypeStruct((B, nc, LANES), BF16), jax.ShapeDtypeStruct((B, LANES, nc), BF16)],
        compiler_params=_cparams(("parallel",)),
    )(proj, proj, pek, w1k, b1k, w2k, pev, w1v, b1v, w2v)


def _compress_weights(pe, w1, b1, w2):
    G = NSA_KV_HEADS
    pe2 = jnp.tile(pe, (1, G))
    w1l = w1.astype(BF16).reshape(CMP_LEN, HEAD_DIM, CMP_HIDDEN)
    w2b = w2.astype(BF16)
    z1 = jnp.zeros_like(w1l)
    z2 = jnp.zeros_like(w2b)
    w1_bd = jnp.concatenate([jnp.concatenate([w1l, z1], axis=2), jnp.concatenate([z1, w1l], axis=2)], axis=1)
    w2_bd = jnp.concatenate([jnp.concatenate([w2b, z2], axis=1), jnp.concatenate([z2, w2b], axis=1)], axis=0)
    return pe2, w1_bd, jnp.tile(b1.reshape(1, -1), (1, G)), w2_bd


def _nsa_kernel(qn_ref, qr_ref, kc_ref, vct_ref, ks_ref, vst_ref, kw_ref, vwt_ref, gt_ref, ot_ref,
                o_ref, acc_ref, m_ref, l_ref, *, S):
    i = pl.program_id(1)
    nc = S // CMP_STRIDE
    ns = S // SEL_BLOCK
    k_top = min(SEL_TOPK, ns)
    G = NSA_KV_HEADS
    R = H_NSA // G
    RT = R * TQ
    HD = HEAD_DIM
    low = lax.broadcasted_iota(jnp.int32, (TQ, LANES), 1) < HD
    gates_t = gt_ref[...].T
    qn = qn_ref[...] * SCALE
    qr = qr_ref[...]

    def tiled(mask, s, fill):
        return jnp.concatenate([jnp.where(mask, s[:, r * TQ:(r + 1) * TQ], fill) for r in range(R)], axis=1)

    def stack(qfull, g):
        in_g = low if g == 0 else jnp.logical_not(low)
        parts = []
        for r in range(R):
            a, b = divmod(R * g + r, 2)
            blk = qfull[:, a * LANES:(a + 1) * LANES]
            if b != g:
                blk = pltpu.roll(blk, HD, axis=1)
            parts.append(jnp.where(in_g, blk, 0.0))
        return jnp.concatenate(parts, axis=0).astype(BF16)

    def rows_of(g, x):
        return x[g * HD:(g + 1) * HD, :]


    qn_s = [stack(qn, g) for g in range(G)]
    c_valid = (lax.broadcasted_iota(jnp.int32, (nc, TQ), 0) * CMP_STRIDE + (CMP_LEN - 1)
               <= lax.broadcasted_iota(jnp.int32, (nc, TQ), 1) + i * TQ)
    scs = [tiled(c_valid, _dot_nt(kc_ref[0], qn_s[g]), NEG) for g in range(G)]
    es = [tiled(c_valid, jnp.exp(sc - jnp.max(sc, axis=0, keepdims=True)), 0.0) for sc in scs]
    dens = [jnp.sum(e, axis=0, keepdims=True) for e in es]
    p_cmps = [e / jnp.where(den > 0.0, den, 1.0) for e, den in zip(es, dens)]
    vct = vct_ref[0]
    o_cmps = [_dot(rows_of(g, vct), p_cmps[g].astype(BF16)) for g in range(G)]

    p_slcs = []
    for p in p_cmps:
        p_sum = p[:, 0:TQ]
        for r in range(1, R):
            p_sum = p_sum + p[:, r * TQ:(r + 1) * TQ]
        p_slcs.append(_dot(ot_ref[...], _split_bf16(p_sum, 0)))
    n_idx = lax.broadcasted_iota(jnp.int32, (ns, TQ), 0)
    t_s = lax.broadcasted_iota(jnp.int32, (ns, TQ), 1) + i * TQ
    forced = jnp.logical_or(n_idx == 0, n_idx == (t_s >> SEL_SHIFT))
    s_valid = n_idx * SEL_BLOCK <= t_s
    scores = [jnp.where(forced, 1e4, jnp.where(s_valid, p_slc, -1.0)) for p_slc in p_slcs]
    ranks = [jnp.zeros((ns, TQ), F32) for _ in range(G)]
    for m_i in range(ns):
        for g in range(G):
            sm = scores[g][m_i:m_i + 1, :]
            ge = jnp.where(sm >= scores[g], 1.0, 0.0)
            gt = jnp.where(sm > scores[g], 1.0, 0.0)
            ranks[g] = ranks[g] + jnp.where(n_idx > m_i, ge, gt)
    sels = []
    for g in range(G):
        sel = jnp.where(ranks[g] < k_top, 1.0, 0.0)
        if ns < LANES:
            sel = jnp.concatenate([sel, jnp.zeros((LANES - ns, TQ), F32)], axis=0)
        sels.append(sel.astype(BF16))

    qr_s = [stack(qr, g) for g in range(G)]
    m_ref[...] = jnp.full(m_ref.shape, NEG, F32)
    l_ref[...] = jnp.zeros_like(l_ref)
    acc_ref[...] = jnp.zeros_like(acc_ref)

    def attend(items):
        stats = []
        for c, s, _ in items:
            m_old = m_ref[c]
            stats.append((m_old, jnp.maximum(m_old, jnp.max(s, axis=0, keepdims=True))))
        ps = [jnp.exp(s - m_new) for (_, s, _), (_, m_new) in zip(items, stats)]
        for (c, _, vt), (m_old, m_new), p in zip(items, stats, ps):
            alpha = jnp.exp(m_old - m_new)
            m_ref[c] = m_new
            l_ref[c] = alpha * l_ref[c] + jnp.sum(p, axis=0, keepdims=True)
            acc_ref[c] = alpha * acc_ref[c] + _dot(vt, p.astype(BF16))

    def chosen(g, off, width):
        blk = (lax.broadcasted_iota(jnp.int32, (width, LANES), 0) + off) >> SEL_SHIFT
        expand = jnp.where(blk == lax.broadcasted_iota(jnp.int32, (width, LANES), 1), 1.0, 0.0).astype(BF16)
        return _dot(expand, sels[g]) > 0.5

    n_win = jnp.minimum(i, (WINDOW + TQ - 2) // LANES) + 1
    n_w = n_win + ((i + 1 - n_win) & 1)

    def win_body(j, carry):
        kb = i - j
        off = pl.multiple_of(kb * LANES, LANES)
        s_pos = lax.broadcasted_iota(jnp.int32, (LANES, TQ), 0) + off
        t_q = lax.broadcasted_iota(jnp.int32, (LANES, TQ), 1) + i * TQ
        causal = s_pos <= t_q
        band = jnp.logical_and(causal, t_q - s_pos < WINDOW)
        ks_t, kw_t = ks_ref[pl.ds(off, LANES), :], kw_ref[pl.ds(off, LANES), :]
        vs_t, vw_t = vst_ref[0, kb], vwt_ref[0, kb]
        items = []
        for g in range(G):
            pick = jnp.logical_and(chosen(g, off, LANES), causal)
            items.append((2 * g, tiled(pick, _dot_nt(ks_t, qr_s[g]), NEG), rows_of(g, vs_t)))
            items.append((2 * g + 1, tiled(band, _dot_nt(kw_t, qr_s[g]), NEG), rows_of(g, vw_t)))
        attend(items)
        return carry

    lax.fori_loop(0, n_w, win_body, 0)

    def sel_body(j, carry):
        kb = i - n_w - 2 * j - 1
        off = pl.multiple_of(kb * LANES, LANES)
        ks_t = ks_ref[pl.ds(off, 2 * LANES), :]
        vs_t = jnp.concatenate([vst_ref[0, kb], vst_ref[0, kb + 1]], axis=1)
        attend([(2 * g, tiled(chosen(g, off, 2 * LANES), _dot_nt(ks_t, qr_s[g]), NEG), rows_of(g, vs_t))
                for g in range(G)])
        return carry

    lax.fori_loop(0, (i + 1 - n_w) // 2, sel_body, 0)

    heads = [None] * H_NSA
    for g in range(G):
        l_sel, l_win = l_ref[2 * g], l_ref[2 * g + 1]
        o_sel = acc_ref[2 * g] / jnp.where(l_sel > 0.0, l_sel, 1.0)
        o_win = acc_ref[2 * g + 1] / jnp.where(l_win > 0.0, l_win, 1.0)
        for r in range(R):
            h = R * g + r
            cols = slice(r * TQ, (r + 1) * TQ)
            heads[h] = (gates_t[3 * h:3 * h + 1, :] * o_cmps[g][:, cols]
                        + gates_t[3 * h + 1:3 * h + 2, :] * o_sel[:, cols]
                        + gates_t[3 * h + 2:3 * h + 3, :] * o_win[:, cols])
    for a in range(H_NSA // 2):
        pair = jnp.concatenate([heads[2 * a], heads[2 * a + 1]], axis=0)
        o_ref[:, a * LANES:(a + 1) * LANES] = pair.T.astype(o_ref.dtype)


def _nsa_attention(proj, q_rope, k_cmp, v_cmp_t, ks_r, vs_t, kw_r, vw_t, gates, overlap_t, B, S):
    nq = S // TQ
    nc = S // CMP_STRIDE
    nk = S // LANES
    G = NSA_KV_HEADS
    rt = H_NSA // G * TQ
    k_full = pl.BlockSpec((S, LANES), lambda b, i: (b, 0))
    vt_full = pl.BlockSpec((1, nk, LANES, LANES), lambda b, i: (b, 0, 0, 0))
    return pl.pallas_call(
        functools.partial(_nsa_kernel, S=S),
        grid=(B, nq),
        in_specs=[pl.BlockSpec((TQ, W_HEADS), lambda b, i: (b * nq + i, C_NQ // W_HEADS)),
                  pl.BlockSpec((TQ, W_HEADS), lambda b, i: (b * nq + i, 0)),
                  pl.BlockSpec((1, nc, LANES), lambda b, i: (b, 0, 0)),
                  pl.BlockSpec((1, LANES, nc), lambda b, i: (b, 0, 0)),
                  k_full, vt_full, k_full, vt_full,
                  pl.BlockSpec((TQ, LANES), lambda b, i: (b * nq + i, 0)),
                  pl.BlockSpec(overlap_t.shape, lambda b, i: (0, 0))],
        out_specs=pl.BlockSpec((TQ, W_HEADS), lambda b, i: (b * nq + i, 0)),
        out_shape=jax.ShapeDtypeStruct((B * S, W_HEADS), BF16),
        scratch_shapes=[pltpu.VMEM((2 * G, HEAD_DIM, rt), F32), pltpu.VMEM((2 * G, 1, rt), F32),
                        pltpu.VMEM((2 * G, 1, rt), F32)],
        compiler_params=_cparams(("parallel", "arbitrary")),
    )(proj, q_rope, k_cmp, v_cmp_t, ks_r, vs_t, kw_r, vw_t, gates, overlap_t)


def _merge_kernel(x_ref, osb_ref, onsa_ref, ofox_ref, m0_ref, m1_ref, m2_ref,
                  wsb_ref, wnsa_ref, wfox_ref, wout_ref, g_ref, o_ref):
    y = jax.nn.sigmoid(m0_ref[...]) * _dot(osb_ref[...], wsb_ref[...])
    y = y + jax.nn.sigmoid(m1_ref[...]) * _dot(onsa_ref[...], wnsa_ref[...])
    y = y + jax.nn.sigmoid(m2_ref[...]) * _dot(ofox_ref[...], wfox_ref[...])
    z = _dot(y.astype(BF16), wout_ref[...])
    o_ref[...] = x_ref[...] + _rms(z, g_ref[...])


def _merge(x, o_sb, o_nsa, o_fox, proj, w_sb, w_nsa, w_fox, w_out, g, tm):
    T, D = x.shape
    row = lambda w: pl.BlockSpec((tm, w), lambda i: (i, 0))
    full = lambda a: pl.BlockSpec(a.shape, lambda i: (0, 0))
    gate = lambda c: pl.BlockSpec((tm, D), lambda i: (i, C_MERGE // D + c))
    return pl.pallas_call(
        _merge_kernel,
        grid=(T // tm,),
        in_specs=[row(D), row(W_HEADS), row(W_HEADS), row(W_HEADS), gate(0), gate(1), gate(2),
                  full(w_sb), full(w_nsa), full(w_fox), full(w_out), full(g)],
        out_specs=row(D),
        out_shape=jax.ShapeDtypeStruct((T, D), F32),
        compiler_params=_cparams(("parallel",)),
    )(x, o_sb, o_nsa, o_fox, proj, proj, proj, w_sb, w_nsa, w_fox, w_out, g)


def _mem_kv_kernel(mem_ref, g_ref, wk_ref, wv_ref, k_ref, v_ref):
    mn = _rms(mem_ref[0], g_ref[...]).astype(BF16)
    k_ref[0] = _dot(mn, wk_ref[...]).astype(BF16)
    v_ref[0] = _dot(mn, wv_ref[...]).astype(BF16)


def _mem_kv(mem, g, wk, wv):
    B, M, D = mem.shape
    full = lambda a: pl.BlockSpec(a.shape, lambda b: (0, 0))
    out_spec = pl.BlockSpec((1, M, W_MEM), lambda b: (b, 0, 0))
    return pl.pallas_call(
        _mem_kv_kernel,
        grid=(B,),
        in_specs=[pl.BlockSpec((1, M, D), lambda b: (b, 0, 0)), full(g), full(wk), full(wv)],
        out_specs=[out_spec, out_spec],
        out_shape=[jax.ShapeDtypeStruct((B, M, W_MEM), BF16)] * 2,
        compiler_params=_cparams(("parallel",)),
    )(mem, g, wk, wv)


def _mem_attn_kernel(x_ref, g_ref, wq_ref, k_ref, v_ref, wo_ref, gp_ref, o_ref):
    x = x_ref[...]
    q = _dot(_rms(x, g_ref[...]).astype(BF16), wq_ref[...]) * SCALE
    k = k_ref[0]
    v = v_ref[0]
    lane = lax.broadcasted_iota(jnp.int32, q.shape, 1)
    o = jnp.zeros(q.shape, F32)
    for h in range(H_MEM):
        in_h = jnp.logical_and(lane >= h * HEAD_DIM, lane < (h + 1) * HEAD_DIM)
        s = _dot_nt(jnp.where(in_h, q, 0.0).astype(BF16), k)
        e = jnp.exp(s - jnp.max(s, axis=-1, keepdims=True))
        p = e / jnp.sum(e, axis=-1, keepdims=True)
        o = o + jnp.where(in_h, _dot(p.astype(BF16), v), 0.0)
    y = _dot(o.astype(BF16), wo_ref[...])
    o_ref[...] = x + _rms(y, gp_ref[...])


def _mem_attn(x, g_pre, wq, k, v, wo, g_post, B, S, tm):
    T, D = x.shape
    M = k.shape[1]
    nt = S // tm
    full = lambda a: pl.BlockSpec(a.shape, lambda b, i: (0, 0))
    kv_spec = pl.BlockSpec((1, M, W_MEM), lambda b, i: (b, 0, 0))
    row = pl.BlockSpec((tm, D), lambda b, i: (b * nt + i, 0))
    return pl.pallas_call(
        _mem_attn_kernel,
        grid=(B, nt),
        in_specs=[row, full(g_pre), full(wq), kv_spec, kv_spec, full(wo), full(g_post)],
        out_specs=row,
        out_shape=jax.ShapeDtypeStruct((T, D), F32),
        compiler_params=_cparams(("parallel", "parallel")),
    )(x, g_pre, wq, k, v, wo, g_post)


def _ffn_kernel(x_ref, g_ref, wg_ref, wu_ref, wd_ref, gp_ref, o_ref, h_ref, acc_ref):
    j = pl.program_id(1)

    @pl.when(j == 0)
    def _():
        h_ref[...] = _rms(x_ref[...], g_ref[...]).astype(BF16)
        acc_ref[...] = jnp.zeros_like(acc_ref)

    h = h_ref[...]
    a = _dot(h, wg_ref[...])
    u = _dot(h, wu_ref[...])
    acc_ref[...] += _dot((a * jax.nn.sigmoid(a) * u).astype(BF16), wd_ref[...])

    @pl.when(j == pl.num_programs(1) - 1)
    def _():
        o_ref[...] = x_ref[...] + _rms(acc_ref[...], gp_ref[...])


def _ffn(x, g_pre, wg, wu, wd, g_post, tm, tf):
    T, D = x.shape
    F = wg.shape[1]
    row = pl.BlockSpec((tm, D), lambda i, j: (i, 0))
    vec = pl.BlockSpec((1, D), lambda i, j: (0, 0))
    return pl.pallas_call(
        _ffn_kernel,
        grid=(T // tm, F // tf),
        in_specs=[row, vec,
                  pl.BlockSpec((D, tf), lambda i, j: (0, j)),
                  pl.BlockSpec((D, tf), lambda i, j: (0, j)),
                  pl.BlockSpec((tf, D), lambda i, j: (j, 0)),
                  vec],
        out_specs=row,
        out_shape=jax.ShapeDtypeStruct((T, D), F32),
        scratch_shapes=[pltpu.VMEM((tm, D), BF16), pltpu.VMEM((tm, D), F32)],
        compiler_params=_cparams(("parallel", "arbitrary")),
    )(x, g_pre, wg, wu, wd, g_post)


def _scan_matrix():
    j = np.arange(LANES)
    later = (j[:, None] > j[None, :]).astype(np.float32)
    u = np.concatenate([later, np.ones((LANES, LANES), np.float32)], axis=1)
    return jnp.asarray(np.concatenate([u, u], axis=0), dtype=BF16)


def _overlap_t(S):
    nc, ns = S // CMP_STRIDE, S // SEL_BLOCK
    c0 = np.arange(nc) * CMP_STRIDE
    n0 = np.arange(ns) * SEL_BLOCK
    ov = (c0[None, :] < n0[:, None] + SEL_BLOCK) & (c0[None, :] + CMP_LEN > n0[:, None])
    ov = ov & (np.arange(nc)[None, :] < nc - 1)
    ov = ov.astype(np.float32)
    return jnp.asarray(np.concatenate([ov, ov], axis=1), dtype=BF16)


FOX_F_LANE = 24
N_PARTS = 3


def _fox_bias_placement():
    n_pair = H_FOX // 2
    pa = np.zeros((N_PARTS * LANES, n_pair * 2 * LANES), np.float32)
    pb = np.zeros((N_PARTS * LANES, n_pair * LANES), np.float32)
    oa = np.zeros((1, n_pair * 2 * LANES), np.float32)
    ob = np.zeros((1, n_pair * LANES), np.float32)
    for p in range(n_pair):
        for e in range(2):
            src = FOX_F_LANE + 2 * p + e
            for x in range(N_PARTS):
                pa[x * LANES + src, (2 * p + e) * LANES + 8 * e + x] = 1.0
                oa[0, (2 * p + e) * LANES + 8 * e + N_PARTS + x] = 1.0
                pb[x * LANES + src, p * LANES + 8 * e + N_PARTS + x] = -1.0
                ob[0, p * LANES + 8 * e + x] = 1.0
    return (jnp.asarray(pa, dtype=BF16), jnp.asarray(pb, dtype=BF16), jnp.asarray(oa), jnp.asarray(ob))


def _rope_rows():
    half = ROPE_DIM // 2
    inv_freq = ROPE_THETA ** (-jnp.arange(half, dtype=F32) / half)
    d = np.arange(LANES) % HEAD_DIM
    rot = d < ROPE_DIM
    freq = jnp.where(jnp.asarray(rot), inv_freq[jnp.asarray(d % half)], 0.0)
    s_up = jnp.asarray(((d >= half) & rot).astype(np.float32))
    s_dn = jnp.asarray(-(d < half).astype(np.float32))
    rows = jnp.stack([freq, s_up, s_dn] + [jnp.zeros((LANES,), F32)] * 5)
    return rows.astype(F32)


def _reorder_w_in(w):
    pad = jnp.zeros((w.shape[0], MISC_W - (_O_FOX_Q - _O_NSA_G) - (_O_MERGE - _O_FOX_F)), w.dtype)
    return jnp.concatenate([
        w[:, _O_MERGE:_O_END],
        w[:, :_O_NSA_G - 6 * LANES],
        w[:, _O_FOX_Q:_O_FOX_F],
        w[:, _O_NSA_G - 6 * LANES:_O_NSA_G],
        w[:, _O_NSA_G:_O_FOX_Q],
        w[:, _O_FOX_F:_O_MERGE],
        pad], axis=1)


def _row_tile(n, target):
    t = min(n, target)
    while n % t:
        t //= 2
    return t


def kernel(x, mem, positions, g_pre_mix, g_post_mix, g_pre_mem, g_mem, g_post_mem, g_pre_ffn, g_post_ffn,
           w_in, b_fox_f, cmp_pe_k, cmp_w1_k, cmp_b1_k, cmp_w2_k, cmp_pe_v, cmp_w1_v, cmp_b1_v, cmp_w2_v,
           w_up_sb, w_up_nsa, w_up_fox, w_out, w_mem_q, w_mem_k, w_mem_v, w_mem_o,
           w_ffn_gate, w_ffn_up, w_ffn_down):
    B, S, D = x.shape
    T = B * S
    depth = w_in.shape[0]
    u2 = _scan_matrix()
    overlap_t = _overlap_t(S)
    rope_c = _rope_rows()
    place = _fox_bias_placement()
    pos3 = positions.reshape(B, S, 1)
    vec = lambda g: g.reshape(1, -1)
    tm_big = _row_tile(T, 1024)
    tm_mid = _row_tile(T, 512)
    d_ff = w_ffn_gate.shape[2]
    tf = d_ff // 2 if (d_ff // 2) % LANES == 0 else d_ff

    w_in_b = w_in.astype(BF16)
    xf = x.reshape(T, D)
    for l in range(depth):
        proj = _norm_matmul(xf, vec(g_pre_mix[l]), _reorder_w_in(w_in_b[l]), tm_big, N_IN // 5)

        o_sb = _sb_attention(proj, u2, B, S)

        bias_row = jnp.zeros((1, LANES), F32).at[0, FOX_F_LANE:FOX_F_LANE + H_FOX].set(b_fox_f[l])
        gates, fox_a, fox_b, q_rope, ks_r, kw_r, v_fox_t, vs_t, vw_t = _prep(
            proj, bias_row, pos3, rope_c, place, B, S)
        o_fox = _fox_attention(proj, v_fox_t, fox_a, fox_b, B, S)

        k_cmp, v_cmp_t = _compress(
            proj, *_compress_weights(cmp_pe_k[l], cmp_w1_k[l], cmp_b1_k[l], cmp_w2_k[l]),
            *_compress_weights(cmp_pe_v[l], cmp_w1_v[l], cmp_b1_v[l], cmp_w2_v[l]), B, S)
        o_nsa = _nsa_attention(proj, q_rope, k_cmp, v_cmp_t, ks_r, vs_t, kw_r, vw_t, gates, overlap_t, B, S)

        xf = _merge(xf, o_sb, o_nsa, o_fox, proj, w_up_sb[l].astype(BF16), w_up_nsa[l].astype(BF16),
                    w_up_fox[l].astype(BF16), w_out[l].astype(BF16), vec(g_post_mix[l]), tm_mid)

        k_mem, v_mem = _mem_kv(mem, vec(g_mem[l]), w_mem_k[l].astype(BF16), w_mem_v[l].astype(BF16))
        xf = _mem_attn(xf, vec(g_pre_mem[l]), w_mem_q[l].astype(BF16), k_mem, v_mem,
                       w_mem_o[l].astype(BF16), vec(g_post_mem[l]), B, S, _row_tile(S, 512))

        xf = _ffn(xf, vec(g_pre_ffn[l]), w_ffn_gate[l].astype(BF16), w_ffn_up[l].astype(BF16),
                  w_ffn_down[l].astype(BF16), vec(g_post_ffn[l]), tm_mid, tf)
    return xf.reshape(B, S, D)
```

```python
import functools

import numpy as np
import jax
import jax.numpy as jnp
from jax import lax
from jax.experimental import pallas as pl
from jax.experimental.pallas import tpu as pltpu

F32 = jnp.float32
BF16 = jnp.bfloat16

D_MODEL = 1024
HEAD_DIM = 64
H_SB = 8
H_NSA = 8
NSA_KV_HEADS = 2
H_FOX = 8
H_MEM = 4
N_BRANCH = 3
ROPE_THETA = 500000.0
ROPE_DIM = HEAD_DIM // 4
CMP_STRIDE = 16
CMP_LEN = 2 * CMP_STRIDE
CMP_HIDDEN = 256
SEL_BLOCK = 64
SEL_SHIFT = SEL_BLOCK.bit_length() - 1
SEL_TOPK = 8
WINDOW = 512
W_HEADS = 8 * HEAD_DIM
W_MEM = H_MEM * HEAD_DIM
EPS = 1e-6
SCALE = HEAD_DIM ** -0.5
NEG = -1e30
LOG_ZERO = -104.0

LANES = 128
TQ = 128
TQA = 256

C_MERGE = 0
C_SBQ, C_SBK, C_SBV = 3072, 3584, 4096
C_NQ = 4608
C_FQ, C_FK, C_FV = 5120, 5632, 6144
C_NKV = 6656
C_MISC = 7424
MISC_W = 256
N_IN = 7680
_O_NSA_G, _O_FOX_Q, _O_FOX_F, _O_MERGE, _O_END = 2816, 2840, 4376, 4384, 7456

VMEM_LIMIT = 56 * 1024 * 1024


def _cparams(sem):
    return pltpu.CompilerParams(dimension_semantics=sem, vmem_limit_bytes=VMEM_LIMIT)


def _dot(a, b):
    return jnp.dot(a, b, preferred_element_type=F32)


def _dot_nt(a, b):
    return lax.dot_general(a, b, (((1,), (1,)), ((), ())), preferred_element_type=F32)


def _rms(x, g):
    ms = jnp.mean(x * x, axis=-1, keepdims=True)
    return x * lax.rsqrt(ms + EPS) * g


def _split_bf16(x, axis=1):
    hi = x.astype(BF16)
    lo = (x - hi.astype(F32)).astype(BF16)
    return jnp.concatenate([hi, lo], axis=axis)


def _norm_matmul_kernel(x_ref, g_ref, w_ref, o_ref, h_ref):
    @pl.when(pl.program_id(1) == 0)
    def _():
        h_ref[...] = _rms(x_ref[...], g_ref[...]).astype(BF16)

    o_ref[...] = _dot(h_ref[...], w_ref[...])


def _norm_matmul(x, g, w, tm, tn):
    T, D = x.shape
    N = w.shape[1]
    return pl.pallas_call(
        _norm_matmul_kernel,
        grid=(T // tm, N // tn),
        in_specs=[pl.BlockSpec((tm, D), lambda i, j: (i, 0)),
                  pl.BlockSpec((1, D), lambda i, j: (0, 0)),
                  pl.BlockSpec((D, tn), lambda i, j: (0, j))],
        out_specs=pl.BlockSpec((tm, tn), lambda i, j: (i, j)),
        out_shape=jax.ShapeDtypeStruct((T, N), F32),
        scratch_shapes=[pltpu.VMEM((tm, D), BF16)],
        compiler_params=_cparams(("parallel", "arbitrary")),
    )(x, g, w)


def _stack_pair(q, low):
    return jnp.concatenate([jnp.where(low, q, 0.0), jnp.where(low, 0.0, q)], axis=0).astype(BF16)


def _sb_kernel(q_ref, k_ref, v_ref, u_ref, o_ref, acc_ref, cs_ref, arg_ref, rs_ref):
    i = pl.program_id(2)
    rt = 2 * TQA
    n_grp = TQA // LANES
    lane = lax.broadcasted_iota(jnp.int32, (rt, LANES), 1)
    t_q = (lax.broadcasted_iota(jnp.int32, (rt, LANES), 0) & (TQA - 1)) + i * TQA
    low = lax.broadcasted_iota(jnp.int32, (TQA, LANES), 1) < HEAD_DIM
    qs = _stack_pair(q_ref[...] * SCALE, low)

    def offsets(kb_first):
        return [pl.multiple_of((kb_first - d) * LANES, LANES) for d in range(n_grp)]

    def logits(offs):
        return [_dot_nt(qs, k_ref[pl.ds(off, LANES), :].astype(BF16)) for off in offs]

    def scan(offs, zs, on_diagonal):
        log_keeps, log_betas, stricts = [], [], []
        for off, z in zip(offs, zs):
            log_keep = -(jnp.maximum(z, 0.0) + jnp.log(1.0 + jnp.exp(-jnp.abs(z))))
            log_betas.append(z + log_keep)
            if on_diagonal:
                stricts.append((lane + off) < t_q)
                log_keep = jnp.where(stricts[-1], log_keep, 0.0)
            log_keeps.append(log_keep)
        c2s = [_dot(_split_bf16(log_keep), u_ref[...]) for log_keep in log_keeps]
        args = [log_beta + c2[:, :LANES] for log_beta, c2 in zip(log_betas, c2s)]
        if on_diagonal:
            args = [jnp.where(strict, arg, NEG) for strict, arg in zip(stricts, args)]
        return args, [c2[:, LANES:] for c2 in c2s]

    def stash(args, row_sums):
        for d in range(n_grp):
            arg_ref[d] = args[d]
            rs_ref[d] = row_sums[d]

    def weights(offs, args, row_sums):
        cs = cs_ref[...]
        pvs = []
        for d, off in enumerate(offs):
            w = jnp.exp(args[d] + cs)
            pvs.append(_dot(w.astype(BF16), v_ref[pl.ds(off, LANES), :].astype(BF16)))
            cs = cs + row_sums[d]
        return cs, pvs

    def accumulate(cs, pvs):
        acc = acc_ref[...]
        for pv in pvs:
            acc = acc + pv
        acc_ref[...] = acc
        cs_ref[...] = cs

    acc_ref[...] = jnp.zeros_like(acc_ref)
    cs_ref[...] = jnp.zeros_like(cs_ref)
    def reaches_past(cs, row_sums):
        for rs in row_sums:
            cs = cs + rs
        return (jnp.max(cs) >= LOG_ZERO).astype(jnp.int32)

    first = offsets(n_grp * i + n_grp - 1)
    first_args, first_sums = scan(first, logits(first), True)
    stash(first_args, first_sums)

    def cond(carry):
        j, alive = carry
        return jnp.logical_and(j < i, alive == 1)

    def body(carry):
        j, _ = carry
        cur = offsets(n_grp * (i - j) + n_grp - 1)
        nxt = offsets(n_grp * (i - j) - 1)
        args = [arg_ref[d] for d in range(n_grp)]
        row_sums = [rs_ref[d] for d in range(n_grp)]
        zs = logits(nxt)
        cs, pvs = weights(cur, args, row_sums)
        nxt_args, nxt_sums = scan(nxt, zs, False)
        alive = reaches_past(cs, nxt_sums)
        accumulate(cs, pvs)
        stash(nxt_args, nxt_sums)
        return j + 1, alive

    j_last, _ = lax.while_loop(cond, body, (jnp.int32(0), reaches_past(cs_ref[...], first_sums)))
    last = offsets(n_grp * (i - j_last) + n_grp - 1)
    accumulate(*weights(last, [arg_ref[d] for d in range(n_grp)], [rs_ref[d] for d in range(n_grp)]))
    acc = acc_ref[...]
    o_ref[...] = jnp.where(low, acc[:TQA], acc[TQA:]).astype(o_ref.dtype)


def _sb_attention(proj, u2, B, S):
    nq = S // TQA
    cq, ck, cv = C_SBQ // LANES, C_SBK // LANES, C_SBV // LANES
    return pl.pallas_call(
        _sb_kernel,
        grid=(B, H_SB // 2, nq),
        in_specs=[pl.BlockSpec((TQA, LANES), lambda b, p, i: (b * nq + i, cq + p)),
                  pl.BlockSpec((S, LANES), lambda b, p, i: (b, ck + p)),
                  pl.BlockSpec((S, LANES), lambda b, p, i: (b, cv + p)),
                  pl.BlockSpec((2 * LANES, 2 * LANES), lambda b, p, i: (0, 0))],
        out_specs=pl.BlockSpec((TQA, LANES), lambda b, p, i: (b * nq + i, p)),
        out_shape=jax.ShapeDtypeStruct((B * S, W_HEADS), BF16),
        scratch_shapes=[pltpu.VMEM((2 * TQA, LANES), F32), pltpu.VMEM((2 * TQA, LANES), F32),
                        pltpu.VMEM((TQA // LANES, 2 * TQA, LANES), F32),
                        pltpu.VMEM((TQA // LANES, 2 * TQA, LANES), F32)],
        compiler_params=_cparams(("parallel", "parallel", "arbitrary")),
    )(proj, proj, proj, u2)


def _fox_kernel(q_ref, k_ref, vt_ref, fa_ref, fb_ref, o_ref, acc_ref, m_ref, l_ref, s_ref, cmax_ref):
    i = pl.program_id(2)
    rt = 2 * TQA
    low = lax.broadcasted_iota(jnp.int32, (TQA, LANES), 1) < HEAD_DIM
    qs = _stack_pair(q_ref[...] * SCALE, low)
    qa = jnp.concatenate([qs, jnp.concatenate([fa_ref[:, :LANES], fa_ref[:, LANES:]], axis=0)], axis=1)
    acc_ref[...] = jnp.zeros_like(acc_ref)
    l_ref[...] = jnp.zeros_like(l_ref)
    m_ref[...] = jnp.full(m_ref.shape, NEG, F32)

    def scores(kb, on_diagonal):
        off = pl.multiple_of(kb * TQA, TQA)
        kbias = jnp.concatenate([k_ref[pl.ds(off, TQA), :].astype(BF16), fb_ref[pl.ds(off, TQA), :]], axis=1)
        s = _dot_nt(kbias, qa)
        if on_diagonal:
            causal = lax.broadcasted_iota(jnp.int32, (TQA, rt), 0) <= (
                lax.broadcasted_iota(jnp.int32, (TQA, rt), 1) & (TQA - 1))
            s = jnp.where(causal, s, NEG)
        return s

    def stash(s):
        s_ref[...] = s
        cmax_ref[...] = jnp.max(s, axis=0, keepdims=True)

    def absorb(kb, s, cmax):
        m_old = m_ref[...]
        m_new = jnp.maximum(m_old, cmax)
        p = jnp.exp(s - m_new)
        alpha = jnp.exp(m_old - m_new)
        m_ref[...] = m_new
        l_ref[...] = alpha * l_ref[...] + jnp.sum(p, axis=0, keepdims=True)
        p = p.astype(BF16)
        vt = vt_ref[0, 0, kb]
        return alpha, [_dot(vt[e * HEAD_DIM:(e + 1) * HEAD_DIM, :], p[:, e * TQA:(e + 1) * TQA]) for e in range(2)]

    def accumulate(alpha, pvs):
        for e in range(2):
            acc_ref[e] = alpha[:, e * TQA:(e + 1) * TQA] * acc_ref[e] + pvs[e]

    stash(scores(i, True))

    def body(j, carry):
        kb = i - j
        s_cur, cmax = s_ref[...], cmax_ref[...]
        s_next = scores(kb - 1, False)
        alpha, pvs = absorb(kb, s_cur, cmax)
        stash(s_next)
        accumulate(alpha, pvs)
        return carry

    lax.fori_loop(0, i, body, 0)
    accumulate(*absorb(0, s_ref[...], cmax_ref[...]))
    l = l_ref[...]
    den = jnp.where(l > 0.0, l, 1.0)
    o_t = jnp.concatenate([acc_ref[0] / den[:, :TQA], acc_ref[1] / den[:, TQA:]], axis=0)
    o_ref[...] = o_t.T.astype(o_ref.dtype)


def _fox_attention(proj, v_t, fox_a, fox_b, B, S):
    nq = S // TQA
    cq, ck = C_FQ // LANES, C_FK // LANES
    return pl.pallas_call(
        _fox_kernel,
        grid=(B, H_FOX // 2, nq),
        in_specs=[pl.BlockSpec((TQA, LANES), lambda b, p, i: (b * nq + i, cq + p)),
                  pl.BlockSpec((S, LANES), lambda b, p, i: (b, ck + p)),
                  pl.BlockSpec((1, 1, nq, LANES, TQA), lambda b, p, i: (b, p, 0, 0, 0)),
                  pl.BlockSpec((TQA, 2 * LANES), lambda b, p, i: (b * nq + i, p)),
                  pl.BlockSpec((S, LANES), lambda b, p, i: (b, p))],
        out_specs=pl.BlockSpec((TQA, LANES), lambda b, p, i: (b * nq + i, p)),
        out_shape=jax.ShapeDtypeStruct((B * S, W_HEADS), BF16),
        scratch_shapes=[pltpu.VMEM((2, HEAD_DIM, TQA), F32), pltpu.VMEM((1, 2 * TQA), F32),
                        pltpu.VMEM((1, 2 * TQA), F32), pltpu.VMEM((TQA, 2 * TQA), F32),
                        pltpu.VMEM((1, 2 * TQA), F32)],
        compiler_params=_cparams(("parallel", "parallel", "arbitrary")),
    )(proj, proj, v_t, fox_a, fox_b)


def _rope(x, cos, s_up, s_dn):
    return x * cos + pltpu.roll(x, ROPE_DIM // 2, axis=1) * s_up + pltpu.roll(x, LANES - ROPE_DIM // 2, axis=1) * s_dn


def _prep_kernel(misc_ref, bias_ref, pos_ref, rc_ref, q_ref, ks_ref, kw_ref, pa_ref, pb_ref, oa_ref, ob_ref,
                 vf_ref, vs_ref, vw_ref,
                 gates_ref, fa_ref, fb_ref, qr_ref, ksr_ref, kwr_ref, vft_ref, vst_ref, vwt_ref, carry_ref, *, ts):
    @pl.when(pl.program_id(1) == 0)
    def _():
        carry_ref[...] = jnp.zeros_like(carry_ref)

    logits = misc_ref[...]
    gates_ref[...] = jax.nn.sigmoid(logits)
    zf = logits + bias_ref[...]
    x = jnp.minimum(zf, 0.0) - jnp.log(1.0 + jnp.exp(-jnp.abs(zf)))
    row = lax.broadcasted_iota(jnp.int32, (ts, LANES), 0)
    sh = 1
    while sh < ts:
        x = x + jnp.where(row >= sh, pltpu.roll(x, sh, axis=0), 0.0)
        sh *= 2
    x = x + carry_ref[...]
    carry_ref[...] = x[ts - 1:ts, :]
    hi = x.astype(BF16)
    r1 = x - hi.astype(F32)
    mid = r1.astype(BF16)
    lo = (r1 - mid.astype(F32)).astype(BF16)
    parts = jnp.concatenate([hi, mid, lo], axis=1)
    fa_ref[...] = (_dot(parts, pa_ref[...]) + oa_ref[...]).astype(BF16)
    fb_ref[...] = (_dot(parts, pb_ref[...]) + ob_ref[...]).astype(BF16)

    ang = pos_ref[0].astype(F32) * rc_ref[0:1, :]
    cos = jnp.cos(ang)
    sin = jnp.sin(ang)
    s_up = sin * rc_ref[1:2, :]
    s_dn = sin * rc_ref[2:3, :]
    for a in range(W_HEADS // LANES):
        qa = q_ref[:, a * LANES:(a + 1) * LANES]
        qr_ref[:, a * LANES:(a + 1) * LANES] = _rope(qa, cos, s_up, s_dn) * SCALE
    ksr_ref[...] = _rope(ks_ref[...], cos, s_up, s_dn).astype(BF16)
    kwr_ref[...] = _rope(kw_ref[...], cos, s_up, s_dn).astype(BF16)

    for p in range(H_FOX // 2):
        for u in range(ts // TQA):
            vft_ref[0, p, u] = vf_ref[u * TQA:(u + 1) * TQA, p * LANES:(p + 1) * LANES].T.astype(BF16)
    for u in range(ts // LANES):
        vst_ref[0, u] = vs_ref[u * LANES:(u + 1) * LANES, :].T.astype(BF16)
        vwt_ref[0, u] = vw_ref[u * LANES:(u + 1) * LANES, :].T.astype(BF16)


def _prep(proj, bias_row, pos3, rope_c, place, B, S):
    ts = min(512, S)
    nt = S // ts
    T = B * S
    row_blk = lambda w: pl.BlockSpec((ts, w), lambda b, t: (b * nt + t, 0))
    full = lambda a: pl.BlockSpec(a.shape, lambda b, t: (0, 0))
    wa, wb = place[0].shape[1], place[1].shape[1]
    return pl.pallas_call(
        functools.partial(_prep_kernel, ts=ts),
        grid=(B, nt),
        in_specs=[pl.BlockSpec((ts, LANES), lambda b, t: (b * nt + t, C_MISC // LANES)),
                  pl.BlockSpec((1, LANES), lambda b, t: (0, 0)),
                  pl.BlockSpec((1, ts, 1), lambda b, t: (b, t, 0)),
                  pl.BlockSpec((8, LANES), lambda b, t: (0, 0)),
                  pl.BlockSpec((ts, W_HEADS), lambda b, t: (b * nt + t, C_NQ // W_HEADS)),
                  pl.BlockSpec((ts, LANES), lambda b, t: (b * nt + t, C_NKV // LANES + 2)),
                  pl.BlockSpec((ts, LANES), lambda b, t: (b * nt + t, C_NKV // LANES + 4)),
                  full(place[0]), full(place[1]), full(place[2]), full(place[3]),
                  pl.BlockSpec((ts, W_HEADS), lambda b, t: (b * nt + t, C_FV // W_HEADS)),
                  pl.BlockSpec((ts, LANES), lambda b, t: (b * nt + t, C_NKV // LANES + 3)),
                  pl.BlockSpec((ts, LANES), lambda b, t: (b * nt + t, C_NKV // LANES + 5))],
        out_specs=[row_blk(LANES), row_blk(wa), row_blk(wb), row_blk(W_HEADS), row_blk(LANES), row_blk(LANES),
                   pl.BlockSpec((1, H_FOX // 2, ts // TQA, LANES, TQA), lambda b, t: (b, 0, t, 0, 0)),
                   pl.BlockSpec((1, ts // LANES, LANES, LANES), lambda b, t: (b, t, 0, 0)),
                   pl.BlockSpec((1, ts // LANES, LANES, LANES), lambda b, t: (b, t, 0, 0))],
        out_shape=[jax.ShapeDtypeStruct((T, LANES), F32),
                   jax.ShapeDtypeStruct((T, wa), BF16),
                   jax.ShapeDtypeStruct((T, wb), BF16),
                   jax.ShapeDtypeStruct((T, W_HEADS), F32),
                   jax.ShapeDtypeStruct((T, LANES), BF16),
                   jax.ShapeDtypeStruct((T, LANES), BF16),
                   jax.ShapeDtypeStruct((B, H_FOX // 2, S // TQA, LANES, TQA), BF16),
                   jax.ShapeDtypeStruct((B, S // LANES, LANES, LANES), BF16),
                   jax.ShapeDtypeStruct((B, S // LANES, LANES, LANES), BF16)],
        scratch_shapes=[pltpu.VMEM((1, LANES), F32)],
        compiler_params=_cparams(("parallel", "arbitrary")),
    )(proj, bias_row, pos3, rope_c, proj, proj, proj, *place, proj, proj, proj)


def _compress_kernel(kc_ref, vc_ref, pek_ref, w1k_ref, b1k_ref, w2k_ref,
                     pev_ref, w1v_ref, b1v_ref, w2v_ref, ok_ref, ov_ref, *, nc):
    for x_ref, pe_ref, w1_ref, b1_ref, w2_ref, o_ref in (
            (kc_ref, pek_ref, w1k_ref, b1k_ref, w2k_ref, ok_ref),
            (vc_ref, pev_ref, w1v_ref, b1v_ref, w2v_ref, ov_ref)):
        h_first = jnp.zeros((nc, NSA_KV_HEADS * CMP_HIDDEN), F32)
        h_second = jnp.zeros((nc, NSA_KV_HEADS * CMP_HIDDEN), F32)
        for l in range(CMP_STRIDE):
            x = x_ref[pl.ds(l, nc, stride=CMP_STRIDE), :]
            h_first = h_first + _dot((x + pe_ref[l:l + 1, :]).astype(BF16), w1_ref[l])
            h_second = h_second + _dot((x + pe_ref[CMP_STRIDE + l:CMP_STRIDE + l + 1, :]).astype(BF16),
                                       w1_ref[CMP_STRIDE + l])
        h = h_first + pltpu.roll(h_second, nc - 1, axis=0) + b1_ref[...]
        a = h * jax.nn.sigmoid(h)
        out = _dot(a.astype(BF16), w2_ref[...])
        o_ref[0] = (out.T if o_ref is ov_ref else out).astype(o_ref.dtype)


def _compress(proj, pek, w1k, b1k, w2k, pev, w1v, b1v, w2v, B, S):
    nc = S // CMP_STRIDE
    full = lambda a: pl.BlockSpec(a.shape, lambda b: (0,) * a.ndim)
    out_spec = pl.BlockSpec((1, nc, LANES), lambda b: (b, 0, 0))
    out_spec_t = pl.BlockSpec((1, LANES, nc), lambda b: (b, 0, 0))
    return pl.pallas_call(
        functools.partial(_compress_kernel, nc=nc),
        grid=(B,),
        in_specs=[pl.BlockSpec((S, LANES), lambda b: (b, C_NKV // LANES)),
                  pl.BlockSpec((S, LANES), lambda b: (b, C_NKV // LANES + 1)),
                  full(pek), full(w1k), full(b1k), full(w2k),
                  full(pev), full(w1v), full(b1v), full(w2v)],
        out_specs=[out_spec, out_spec_t],
        out_shape=[jax.ShapeDtypeStruct((B, nc, LANES), BF16), jax.ShapeDtypeStruct((B, LANES, nc), BF16)],
        compiler_params=_cparams(("parallel",)),
    )(proj, proj, pek, w1k, b1k, w2k, pev, w1v, b1v, w2v)


def _compress_weights(pe, w1, b1, w2):
    G = NSA_KV_HEADS
    pe2 = jnp.tile(pe, (1, G))
    w1l = w1.astype(BF16).reshape(CMP_LEN, HEAD_DIM, CMP_HIDDEN)
    w2b = w2.astype(BF16)
    z1 = jnp.zeros_like(w1l)
    z2 = jnp.zeros_like(w2b)
    w1_bd = jnp.concatenate([jnp.concatenate([w1l, z1], axis=2), jnp.concatenate([z1, w1l], axis=2)], axis=1)
    w2_bd = jnp.concatenate([jnp.concatenate([w2b, z2], axis=1), jnp.concatenate([z2, w2b], axis=1)], axis=0)
    return pe2, w1_bd, jnp.tile(b1.reshape(1, -1), (1, G)), w2_bd


def _nsa_kernel(qn_ref, qr_ref, kc_ref, vct_ref, ks_ref, vst_ref, kw_ref, vwt_ref, gt_ref, ot_ref,
                o_ref, acc_ref, m_ref, l_ref, s_ref, cmax_ref, *, S):
    i = pl.program_id(1)
    nc = S // CMP_STRIDE
    ns = S // SEL_BLOCK
    k_top = min(SEL_TOPK, ns)
    G = NSA_KV_HEADS
    R = H_NSA // G
    RT = R * TQ
    HD = HEAD_DIM
    low = lax.broadcasted_iota(jnp.int32, (TQ, LANES), 1) < HD
    gates_t = gt_ref[...].T
    qn = qn_ref[...] * SCALE
    qr = qr_ref[...]

    def tiled(mask, s, fill):
        return jnp.concatenate([jnp.where(mask, s[:, r * TQ:(r + 1) * TQ], fill) for r in range(R)], axis=1)

    def stack(qfull, g):
        in_g = low if g == 0 else jnp.logical_not(low)
        parts = []
        for r in range(R):
            a, b = divmod(R * g + r, 2)
            blk = qfull[:, a * LANES:(a + 1) * LANES]
            if b != g:
                blk = pltpu.roll(blk, HD, axis=1)
            parts.append(jnp.where(in_g, blk, 0.0))
        return jnp.concatenate(parts, axis=0).astype(BF16)

    def rows_of(g, x):
        return x[g * HD:(g + 1) * HD, :]


    qn_s = [stack(qn, g) for g in range(G)]
    c_valid = (lax.broadcasted_iota(jnp.int32, (nc, TQ), 0) * CMP_STRIDE + (CMP_LEN - 1)
               <= lax.broadcasted_iota(jnp.int32, (nc, TQ), 1) + i * TQ)
    scs = [tiled(c_valid, _dot_nt(kc_ref[0], qn_s[g]), NEG) for g in range(G)]
    es = [tiled(c_valid, jnp.exp(sc - jnp.max(sc, axis=0, keepdims=True)), 0.0) for sc in scs]
    dens = [jnp.sum(e, axis=0, keepdims=True) for e in es]
    p_cmps = [e / jnp.where(den > 0.0, den, 1.0) for e, den in zip(es, dens)]
    vct = vct_ref[0]
    o_cmps = [_dot(rows_of(g, vct), p_cmps[g].astype(BF16)) for g in range(G)]

    p_slcs = []
    for p in p_cmps:
        p_sum = p[:, 0:TQ]
        for r in range(1, R):
            p_sum = p_sum + p[:, r * TQ:(r + 1) * TQ]
        p_slcs.append(_dot(ot_ref[...], _split_bf16(p_sum, 0)))
    n_idx = lax.broadcasted_iota(jnp.int32, (ns, TQ), 0)
    t_s = lax.broadcasted_iota(jnp.int32, (ns, TQ), 1) + i * TQ
    forced = jnp.logical_or(n_idx == 0, n_idx == (t_s >> SEL_SHIFT))
    s_valid = n_idx * SEL_BLOCK <= t_s
    scores = [jnp.where(forced, 1e4, jnp.where(s_valid, p_slc, -1.0)) for p_slc in p_slcs]
    ranks = [jnp.zeros((ns, TQ), F32) for _ in range(G)]
    for m_i in range(ns):
        for g in range(G):
            sm = scores[g][m_i:m_i + 1, :]
            ge = jnp.where(sm >= scores[g], 1.0, 0.0)
            gt = jnp.where(sm > scores[g], 1.0, 0.0)
            ranks[g] = ranks[g] + jnp.where(n_idx > m_i, ge, gt)
    sels = []
    for g in range(G):
        sel = jnp.where(ranks[g] < k_top, 1.0, 0.0)
        if ns < LANES:
            sel = jnp.concatenate([sel, jnp.zeros((LANES - ns, TQ), F32)], axis=0)
        sels.append(sel.astype(BF16))

    qr_s = [stack(qr, g) for g in range(G)]
    m_ref[...] = jnp.full(m_ref.shape, NEG, F32)
    l_ref[...] = jnp.zeros_like(l_ref)
    acc_ref[...] = jnp.zeros_like(acc_ref)

    def stash(scored):
        for c, sc in scored:
            s_ref[c, 0:sc.shape[0], :] = sc
            cmax_ref[c] = jnp.max(sc, axis=0, keepdims=True)

    def absorb(items):
        stats = []
        for c, _, cmax, _ in items:
            m_old = m_ref[c]
            stats.append((m_old, jnp.maximum(m_old, cmax)))
        ps = [jnp.exp(sc - m_new) for (_, sc, _, _), (_, m_new) in zip(items, stats)]
        out = []
        for (c, _, _, vt), (m_old, m_new), p in zip(items, stats, ps):
            alpha = jnp.exp(m_old - m_new)
            m_ref[c] = m_new
            l_ref[c] = alpha * l_ref[c] + jnp.sum(p, axis=0, keepdims=True)
            out.append((c, alpha, _dot(vt, p.astype(BF16))))
        return out

    def accumulate(updates):
        for c, alpha, pv in updates:
            acc_ref[c] = alpha * acc_ref[c] + pv

    def chosen(g, off, width):
        blk = (lax.broadcasted_iota(jnp.int32, (width, LANES), 0) + off) >> SEL_SHIFT
        expand = jnp.where(blk == lax.broadcasted_iota(jnp.int32, (width, LANES), 1), 1.0, 0.0).astype(BF16)
        return _dot(expand, sels[g]) > 0.5

    n_win = jnp.minimum(i, (WINDOW + TQ - 2) // LANES) + 1
    n_w = n_win + ((i + 1 - n_win) & 1)
    n_old = (i + 1 - n_w) // 2

    def win_scores(j):
        off = pl.multiple_of((i - j) * LANES, LANES)
        s_pos = lax.broadcasted_iota(jnp.int32, (LANES, TQ), 0) + off
        t_q = lax.broadcasted_iota(jnp.int32, (LANES, TQ), 1) + i * TQ
        causal = s_pos <= t_q
        band = jnp.logical_and(causal, t_q - s_pos < WINDOW)
        ks_t, kw_t = ks_ref[pl.ds(off, LANES), :], kw_ref[pl.ds(off, LANES), :]
        scored = []
        for g in range(G):
            pick = jnp.logical_and(chosen(g, off, LANES), causal)
            scored.append((2 * g, tiled(pick, _dot_nt(ks_t, qr_s[g]), NEG)))
            scored.append((2 * g + 1, tiled(band, _dot_nt(kw_t, qr_s[g]), NEG)))
        return scored

    def win_items(j):
        vs_t, vw_t = vst_ref[0, i - j], vwt_ref[0, i - j]
        items = []
        for g in range(G):
            items.append((2 * g, s_ref[2 * g, 0:LANES, :], cmax_ref[2 * g], rows_of(g, vs_t)))
            items.append((2 * g + 1, s_ref[2 * g + 1, 0:LANES, :], cmax_ref[2 * g + 1], rows_of(g, vw_t)))
        return items

    def old_first_tile(j):
        return i - n_w - 2 * j - 1

    def old_scores(j):
        off = pl.multiple_of(old_first_tile(j) * LANES, LANES)
        ks_t = ks_ref[pl.ds(off, 2 * LANES), :]
        return [(2 * g, tiled(chosen(g, off, 2 * LANES), _dot_nt(ks_t, qr_s[g]), NEG)) for g in range(G)]

    def old_items(j):
        kb = old_first_tile(j)
        vs_t = jnp.concatenate([vst_ref[0, kb], vst_ref[0, kb + 1]], axis=1)
        return [(2 * g, s_ref[2 * g], cmax_ref[2 * g], rows_of(g, vs_t)) for g in range(G)]

    stash(win_scores(0))

    def win_body(j, carry):
        items = win_items(j - 1)
        scored = win_scores(j)
        updates = absorb(items)
        stash(scored)
        accumulate(updates)
        return carry

    lax.fori_loop(1, n_w, win_body, 0)
    accumulate(absorb(win_items(n_w - 1)))

    @pl.when(n_old > 0)
    def _():
        stash(old_scores(0))

        def old_body(j, carry):
            items = old_items(j - 1)
            scored = old_scores(j)
            updates = absorb(items)
            stash(scored)
            accumulate(updates)
            return carry

        lax.fori_loop(1, n_old, old_body, 0)
        accumulate(absorb(old_items(n_old - 1)))

    heads = [None] * H_NSA
    for g in range(G):
        l_sel, l_win = l_ref[2 * g], l_ref[2 * g + 1]
        o_sel = acc_ref[2 * g] / jnp.where(l_sel > 0.0, l_sel, 1.0)
        o_win = acc_ref[2 * g + 1] / jnp.where(l_win > 0.0, l_win, 1.0)
        for r in range(R):
            h = R * g + r
            cols = slice(r * TQ, (r + 1) * TQ)
            heads[h] = (gates_t[3 * h:3 * h + 1, :] * o_cmps[g][:, cols]
                        + gates_t[3 * h + 1:3 * h + 2, :] * o_sel[:, cols]
                        + gates_t[3 * h + 2:3 * h + 3, :] * o_win[:, cols])
    for a in range(H_NSA // 2):
        pair = jnp.concatenate([heads[2 * a], heads[2 * a + 1]], axis=0)
        o_ref[:, a * LANES:(a + 1) * LANES] = pair.T.astype(o_ref.dtype)


def _nsa_attention(proj, q_rope, k_cmp, v_cmp_t, ks_r, vs_t, kw_r, vw_t, gates, overlap_t, B, S):
    nq = S // TQ
    nc = S // CMP_STRIDE
    nk = S // LANES
    G = NSA_KV_HEADS
    rt = H_NSA // G * TQ
    k_full = pl.BlockSpec((S, LANES), lambda b, i: (b, 0))
    vt_full = pl.BlockSpec((1, nk, LANES, LANES), lambda b, i: (b, 0, 0, 0))
    return pl.pallas_call(
        functools.partial(_nsa_kernel, S=S),
        grid=(B, nq),
        in_specs=[pl.BlockSpec((TQ, W_HEADS), lambda b, i: (b * nq + i, C_NQ // W_HEADS)),
                  pl.BlockSpec((TQ, W_HEADS), lambda b, i: (b * nq + i, 0)),
                  pl.BlockSpec((1, nc, LANES), lambda b, i: (b, 0, 0)),
                  pl.BlockSpec((1, LANES, nc), lambda b, i: (b, 0, 0)),
                  k_full, vt_full, k_full, vt_full,
                  pl.BlockSpec((TQ, LANES), lambda b, i: (b * nq + i, 0)),
                  pl.BlockSpec(overlap_t.shape, lambda b, i: (0, 0))],
        out_specs=pl.BlockSpec((TQ, W_HEADS), lambda b, i: (b * nq + i, 0)),
        out_shape=jax.ShapeDtypeStruct((B * S, W_HEADS), BF16),
        scratch_shapes=[pltpu.VMEM((2 * G, HEAD_DIM, rt), F32), pltpu.VMEM((2 * G, 1, rt), F32),
                        pltpu.VMEM((2 * G, 1, rt), F32), pltpu.VMEM((2 * G, 2 * LANES, rt), F32),
                        pltpu.VMEM((2 * G, 1, rt), F32)],
        compiler_params=_cparams(("parallel", "arbitrary")),
    )(proj, q_rope, k_cmp, v_cmp_t, ks_r, vs_t, kw_r, vw_t, gates, overlap_t)


def _merge_kernel(x_ref, osb_ref, onsa_ref, ofox_ref, m0_ref, m1_ref, m2_ref,
                  wsb_ref, wnsa_ref, wfox_ref, wout_ref, g_ref, o_ref):
    y = jax.nn.sigmoid(m0_ref[...]) * _dot(osb_ref[...], wsb_ref[...])
    y = y + jax.nn.sigmoid(m1_ref[...]) * _dot(onsa_ref[...], wnsa_ref[...])
    y = y + jax.nn.sigmoid(m2_ref[...]) * _dot(ofox_ref[...], wfox_ref[...])
    z = _dot(y.astype(BF16), wout_ref[...])
    o_ref[...] = x_ref[...] + _rms(z, g_ref[...])


def _merge(x, o_sb, o_nsa, o_fox, proj, w_sb, w_nsa, w_fox, w_out, g, tm):
    T, D = x.shape
    row = lambda w: pl.BlockSpec((tm, w), lambda i: (i, 0))
    full = lambda a: pl.BlockSpec(a.shape, lambda i: (0, 0))
    gate = lambda c: pl.BlockSpec((tm, D), lambda i: (i, C_MERGE // D + c))
    return pl.pallas_call(
        _merge_kernel,
        grid=(T // tm,),
        in_specs=[row(D), row(W_HEADS), row(W_HEADS), row(W_HEADS), gate(0), gate(1), gate(2),
                  full(w_sb), full(w_nsa), full(w_fox), full(w_out), full(g)],
        out_specs=row(D),
        out_shape=jax.ShapeDtypeStruct((T, D), F32),
        compiler_params=_cparams(("parallel",)),
    )(x, o_sb, o_nsa, o_fox, proj, proj, proj, w_sb, w_nsa, w_fox, w_out, g)


def _mem_kv_kernel(mem_ref, g_ref, wk_ref, wv_ref, k_ref, v_ref):
    mn = _rms(mem_ref[0], g_ref[...]).astype(BF16)
    k_ref[0] = _dot(mn, wk_ref[...]).astype(BF16)
    v_ref[0] = _dot(mn, wv_ref[...]).astype(BF16)


def _mem_kv(mem, g, wk, wv):
    B, M, D = mem.shape
    full = lambda a: pl.BlockSpec(a.shape, lambda b: (0, 0))
    out_spec = pl.BlockSpec((1, M, W_MEM), lambda b: (b, 0, 0))
    return pl.pallas_call(
        _mem_kv_kernel,
        grid=(B,),
        in_specs=[pl.BlockSpec((1, M, D), lambda b: (b, 0, 0)), full(g), full(wk), full(wv)],
        out_specs=[out_spec, out_spec],
        out_shape=[jax.ShapeDtypeStruct((B, M, W_MEM), BF16)] * 2,
        compiler_params=_cparams(("parallel",)),
    )(mem, g, wk, wv)


def _mem_attn_kernel(x_ref, g_ref, wq_ref, k_ref, v_ref, wo_ref, gp_ref, o_ref):
    x = x_ref[...]
    q = _dot(_rms(x, g_ref[...]).astype(BF16), wq_ref[...]) * SCALE
    k = k_ref[0]
    v = v_ref[0]
    lane = lax.broadcasted_iota(jnp.int32, q.shape, 1)
    o = jnp.zeros(q.shape, F32)
    for h in range(H_MEM):
        in_h = jnp.logical_and(lane >= h * HEAD_DIM, lane < (h + 1) * HEAD_DIM)
        s = _dot_nt(jnp.where(in_h, q, 0.0).astype(BF16), k)
        e = jnp.exp(s - jnp.max(s, axis=-1, keepdims=True))
        p = e / jnp.sum(e, axis=-1, keepdims=True)
        o = o + jnp.where(in_h, _dot(p.astype(BF16), v), 0.0)
    y = _dot(o.astype(BF16), wo_ref[...])
    o_ref[...] = x + _rms(y, gp_ref[...])


def _mem_attn(x, g_pre, wq, k, v, wo, g_post, B, S, tm):
    T, D = x.shape
    M = k.shape[1]
    nt = S // tm
    full = lambda a: pl.BlockSpec(a.shape, lambda b, i: (0, 0))
    kv_spec = pl.BlockSpec((1, M, W_MEM), lambda b, i: (b, 0, 0))
    row = pl.BlockSpec((tm, D), lambda b, i: (b * nt + i, 0))
    return pl.pallas_call(
        _mem_attn_kernel,
        grid=(B, nt),
        in_specs=[row, full(g_pre), full(wq), kv_spec, kv_spec, full(wo), full(g_post)],
        out_specs=row,
        out_shape=jax.ShapeDtypeStruct((T, D), F32),
        compiler_params=_cparams(("parallel", "parallel")),
    )(x, g_pre, wq, k, v, wo, g_post)


def _ffn_kernel(x_ref, g_ref, wg_ref, wu_ref, wd_ref, gp_ref, o_ref, h_ref, acc_ref):
    j = pl.program_id(1)

    @pl.when(j == 0)
    def _():
        h_ref[...] = _rms(x_ref[...], g_ref[...]).astype(BF16)
        acc_ref[...] = jnp.zeros_like(acc_ref)

    h = h_ref[...]
    a = _dot(h, wg_ref[...])
    u = _dot(h, wu_ref[...])
    acc_ref[...] += _dot((a * jax.nn.sigmoid(a) * u).astype(BF16), wd_ref[...])

    @pl.when(j == pl.num_programs(1) - 1)
    def _():
        o_ref[...] = x_ref[...] + _rms(acc_ref[...], gp_ref[...])


def _ffn(x, g_pre, wg, wu, wd, g_post, tm, tf):
    T, D = x.shape
    F = wg.shape[1]
    row = pl.BlockSpec((tm, D), lambda i, j: (i, 0))
    vec = pl.BlockSpec((1, D), lambda i, j: (0, 0))
    return pl.pallas_call(
        _ffn_kernel,
        grid=(T // tm, F // tf),
        in_specs=[row, vec,
                  pl.BlockSpec((D, tf), lambda i, j: (0, j)),
                  pl.BlockSpec((D, tf), lambda i, j: (0, j)),
                  pl.BlockSpec((tf, D), lambda i, j: (j, 0)),
                  vec],
        out_specs=row,
        out_shape=jax.ShapeDtypeStruct((T, D), F32),
        scratch_shapes=[pltpu.VMEM((tm, D), BF16), pltpu.VMEM((tm, D), F32)],
        compiler_params=_cparams(("parallel", "arbitrary")),
    )(x, g_pre, wg, wu, wd, g_post)


def _scan_matrix():
    j = np.arange(LANES)
    later = (j[:, None] > j[None, :]).astype(np.float32)
    u = np.concatenate([later, np.ones((LANES, LANES), np.float32)], axis=1)
    return jnp.asarray(np.concatenate([u, u], axis=0), dtype=BF16)


def _overlap_t(S):
    nc, ns = S // CMP_STRIDE, S // SEL_BLOCK
    c0 = np.arange(nc) * CMP_STRIDE
    n0 = np.arange(ns) * SEL_BLOCK
    ov = (c0[None, :] < n0[:, None] + SEL_BLOCK) & (c0[None, :] + CMP_LEN > n0[:, None])
    ov = ov & (np.arange(nc)[None, :] < nc - 1)
    ov = ov.astype(np.float32)
    return jnp.asarray(np.concatenate([ov, ov], axis=1), dtype=BF16)


FOX_F_LANE = 24
N_PARTS = 3


def _fox_bias_placement():
    n_pair = H_FOX // 2
    pa = np.zeros((N_PARTS * LANES, n_pair * 2 * LANES), np.float32)
    pb = np.zeros((N_PARTS * LANES, n_pair * LANES), np.float32)
    oa = np.zeros((1, n_pair * 2 * LANES), np.float32)
    ob = np.zeros((1, n_pair * LANES), np.float32)
    for p in range(n_pair):
        for e in range(2):
            src = FOX_F_LANE + 2 * p + e
            for x in range(N_PARTS):
                pa[x * LANES + src, (2 * p + e) * LANES + 8 * e + x] = 1.0
                oa[0, (2 * p + e) * LANES + 8 * e + N_PARTS + x] = 1.0
                pb[x * LANES + src, p * LANES + 8 * e + N_PARTS + x] = -1.0
                ob[0, p * LANES + 8 * e + x] = 1.0
    return (jnp.asarray(pa, dtype=BF16), jnp.asarray(pb, dtype=BF16), jnp.asarray(oa), jnp.asarray(ob))


def _rope_rows():
    half = ROPE_DIM // 2
    inv_freq = ROPE_THETA ** (-jnp.arange(half, dtype=F32) / half)
    d = np.arange(LANES) % HEAD_DIM
    rot = d < ROPE_DIM
    freq = jnp.where(jnp.asarray(rot), inv_freq[jnp.asarray(d % half)], 0.0)
    s_up = jnp.asarray(((d >= half) & rot).astype(np.float32))
    s_dn = jnp.asarray(-(d < half).astype(np.float32))
    rows = jnp.stack([freq, s_up, s_dn] + [jnp.zeros((LANES,), F32)] * 5)
    return rows.astype(F32)


def _reorder_w_in(w):
    pad = jnp.zeros((w.shape[0], MISC_W - (_O_FOX_Q - _O_NSA_G) - (_O_MERGE - _O_FOX_F)), w.dtype)
    return jnp.concatenate([
        w[:, _O_MERGE:_O_END],
        w[:, :_O_NSA_G - 6 * LANES],
        w[:, _O_FOX_Q:_O_FOX_F],
        w[:, _O_NSA_G - 6 * LANES:_O_NSA_G],
        w[:, _O_NSA_G:_O_FOX_Q],
        w[:, _O_FOX_F:_O_MERGE],
        pad], axis=1)


def _row_tile(n, target):
    t = min(n, target)
    while n % t:
        t //= 2
    return t


def kernel(x, mem, positions, g_pre_mix, g_post_mix, g_pre_mem, g_mem, g_post_mem, g_pre_ffn, g_post_ffn,
           w_in, b_fox_f, cmp_pe_k, cmp_w1_k, cmp_b1_k, cmp_w2_k, cmp_pe_v, cmp_w1_v, cmp_b1_v, cmp_w2_v,
           w_up_sb, w_up_nsa, w_up_fox, w_out, w_mem_q, w_mem_k, w_mem_v, w_mem_o,
           w_ffn_gate, w_ffn_up, w_ffn_down):
    B, S, D = x.shape
    T = B * S
    depth = w_in.shape[0]
    u2 = _scan_matrix()
    overlap_t = _overlap_t(S)
    rope_c = _rope_rows()
    place = _fox_bias_placement()
    pos3 = positions.reshape(B, S, 1)
    vec = lambda g: g.reshape(1, -1)
    tm_big = _row_tile(T, 1024)
    tm_mid = _row_tile(T, 512)
    d_ff = w_ffn_gate.shape[2]
    tf = d_ff // 2 if (d_ff // 2) % LANES == 0 else d_ff

    w_in_b = w_in.astype(BF16)
    xf = x.reshape(T, D)
    for l in range(depth):
        proj = _norm_matmul(xf, vec(g_pre_mix[l]), _reorder_w_in(w_in_b[l]), tm_big, N_IN // 5)

        o_sb = _sb_attention(proj, u2, B, S)

        bias_row = jnp.zeros((1, LANES), F32).at[0, FOX_F_LANE:FOX_F_LANE + H_FOX].set(b_fox_f[l])
        gates, fox_a, fox_b, q_rope, ks_r, kw_r, v_fox_t, vs_t, vw_t = _prep(
            proj, bias_row, pos3, rope_c, place, B, S)
        o_fox = _fox_attention(proj, v_fox_t, fox_a, fox_b, B, S)

        k_cmp, v_cmp_t = _compress(
            proj, *_compress_weights(cmp_pe_k[l], cmp_w1_k[l], cmp_b1_k[l], cmp_w2_k[l]),
            *_compress_weights(cmp_pe_v[l], cmp_w1_v[l], cmp_b1_v[l], cmp_w2_v[l]), B, S)
        o_nsa = _nsa_attention(proj, q_rope, k_cmp, v_cmp_t, ks_r, vs_t, kw_r, vw_t, gates, overlap_t, B, S)

        xf = _merge(xf, o_sb, o_nsa, o_fox, proj, w_up_sb[l].astype(BF16), w_up_nsa[l].astype(BF16),
                    w_up_fox[l].astype(BF16), w_out[l].astype(BF16), vec(g_post_mix[l]), tm_mid)

        k_mem, v_mem = _mem_kv(mem, vec(g_mem[l]), w_mem_k[l].astype(BF16), w_mem_v[l].astype(BF16))
        xf = _mem_attn(xf, vec(g_pre_mem[l]), w_mem_q[l].astype(BF16), k_mem, v_mem,
                       w_mem_o[l].astype(BF16), vec(g_post_mem[l]), B, S, _row_tile(S, 512))

        xf = _ffn(xf, vec(g_pre_ffn[l]), w_ffn_gate[l].astype(BF16), w_ffn_up[l].astype(BF16),
                  w_ffn_down[l].astype(BF16), vec(g_post_ffn[l]), tm_mid, tf)
    return xf.reshape(B, S, D)
```

```python
import functools

import numpy as np
import jax
import jax.numpy as jnp
from jax import lax
from jax.experimental import pallas as pl
from jax.experimental.pallas import tpu as pltpu

F32 = jnp.float32
BF16 = jnp.bfloat16

D_MODEL = 1024
HEAD_DIM = 64
H_SB = 8
H_NSA = 8
NSA_KV_HEADS = 2
H_FOX = 8
H_MEM = 4
N_BRANCH = 3
ROPE_THETA = 500000.0
ROPE_DIM = HEAD_DIM // 4
CMP_STRIDE = 16
CMP_LEN = 2 * CMP_STRIDE
CMP_HIDDEN = 256
SEL_BLOCK = 64
SEL_SHIFT = SEL_BLOCK.bit_length() - 1
SEL_TOPK = 8
WINDOW = 512
W_HEADS = 8 * HEAD_DIM
W_MEM = H_MEM * HEAD_DIM
EPS = 1e-6
SCALE = HEAD_DIM ** -0.5
NEG = -1e30
LOG_ZERO = -104.0

LANES = 128
TQ = 128
TQA = 256
TQF = 512

C_MERGE = 0
C_SBQ, C_SBK, C_SBV = 3072, 3584, 4096
C_NQ = 4608
C_FQ, C_FK, C_FV = 5120, 5632, 6144
C_NKV = 6656
C_MISC = 7424
MISC_W = 256
N_IN = 7680
_O_NSA_G, _O_FOX_Q, _O_FOX_F, _O_MERGE, _O_END = 2816, 2840, 4376, 4384, 7456

VMEM_LIMIT = 56 * 1024 * 1024


def _cparams(sem):
    return pltpu.CompilerParams(dimension_semantics=sem, vmem_limit_bytes=VMEM_LIMIT)


def _dot(a, b):
    return jnp.dot(a, b, preferred_element_type=F32)


def _dot_nt(a, b):
    return lax.dot_general(a, b, (((1,), (1,)), ((), ())), preferred_element_type=F32)


def _rms(x, g):
    ms = jnp.mean(x * x, axis=-1, keepdims=True)
    return x * lax.rsqrt(ms + EPS) * g


def _split_bf16(x, axis=1):
    hi = x.astype(BF16)
    lo = (x - hi.astype(F32)).astype(BF16)
    return jnp.concatenate([hi, lo], axis=axis)


def _norm_matmul_kernel(x_ref, g_ref, w_ref, o_ref, h_ref):
    @pl.when(pl.program_id(1) == 0)
    def _():
        h_ref[...] = _rms(x_ref[...], g_ref[...]).astype(BF16)

    o_ref[...] = _dot(h_ref[...], w_ref[...])


def _norm_matmul(x, g, w, tm, tn):
    T, D = x.shape
    N = w.shape[1]
    return pl.pallas_call(
        _norm_matmul_kernel,
        grid=(T // tm, N // tn),
        in_specs=[pl.BlockSpec((tm, D), lambda i, j: (i, 0)),
                  pl.BlockSpec((1, D), lambda i, j: (0, 0)),
                  pl.BlockSpec((D, tn), lambda i, j: (0, j))],
        out_specs=pl.BlockSpec((tm, tn), lambda i, j: (i, j)),
        out_shape=jax.ShapeDtypeStruct((T, N), F32),
        scratch_shapes=[pltpu.VMEM((tm, D), BF16)],
        compiler_params=_cparams(("parallel", "arbitrary")),
    )(x, g, w)


def _stack_pair(q, low):
    return jnp.concatenate([jnp.where(low, q, 0.0), jnp.where(low, 0.0, q)], axis=0).astype(BF16)


def _sb_kernel(q_ref, k_ref, v_ref, u_ref, o_ref, acc_ref, cs_ref, arg_ref, rs_ref):
    i = pl.program_id(2)
    rt = 2 * TQA
    n_grp = TQA // LANES
    lane = lax.broadcasted_iota(jnp.int32, (rt, LANES), 1)
    t_q = (lax.broadcasted_iota(jnp.int32, (rt, LANES), 0) & (TQA - 1)) + i * TQA
    low = lax.broadcasted_iota(jnp.int32, (TQA, LANES), 1) < HEAD_DIM
    qs = _stack_pair(q_ref[...] * SCALE, low)

    def offsets(kb_first):
        return [pl.multiple_of((kb_first - d) * LANES, LANES) for d in range(n_grp)]

    def logits(offs):
        return [_dot_nt(qs, k_ref[pl.ds(off, LANES), :].astype(BF16)) for off in offs]

    def scan(offs, zs, on_diagonal):
        log_keeps, log_betas, stricts = [], [], []
        for off, z in zip(offs, zs):
            log_keep = -(jnp.maximum(z, 0.0) + jnp.log(1.0 + jnp.exp(-jnp.abs(z))))
            log_betas.append(z + log_keep)
            if on_diagonal:
                stricts.append((lane + off) < t_q)
                log_keep = jnp.where(stricts[-1], log_keep, 0.0)
            log_keeps.append(log_keep)
        c2s = [_dot(_split_bf16(log_keep), u_ref[...]) for log_keep in log_keeps]
        args = [log_beta + c2[:, :LANES] for log_beta, c2 in zip(log_betas, c2s)]
        if on_diagonal:
            args = [jnp.where(strict, arg, NEG) for strict, arg in zip(stricts, args)]
        return args, [c2[:, LANES:] for c2 in c2s]

    def stash(args, row_sums):
        for d in range(n_grp):
            arg_ref[d] = args[d]
            rs_ref[d] = row_sums[d]

    def weights(offs, args, row_sums):
        cs = cs_ref[...]
        pvs = []
        for d, off in enumerate(offs):
            w = jnp.exp(args[d] + cs)
            pvs.append(_dot(w.astype(BF16), v_ref[pl.ds(off, LANES), :].astype(BF16)))
            cs = cs + row_sums[d]
        return cs, pvs

    def accumulate(cs, pvs):
        acc = acc_ref[...]
        for pv in pvs:
            acc = acc + pv
        acc_ref[...] = acc
        cs_ref[...] = cs

    acc_ref[...] = jnp.zeros_like(acc_ref)
    cs_ref[...] = jnp.zeros_like(cs_ref)
    def reaches_past(cs, row_sums):
        for rs in row_sums:
            cs = cs + rs
        return (jnp.max(cs) >= LOG_ZERO).astype(jnp.int32)

    first = offsets(n_grp * i + n_grp - 1)
    first_args, first_sums = scan(first, logits(first), True)
    stash(first_args, first_sums)

    def cond(carry):
        j, alive = carry
        return jnp.logical_and(j < i, alive == 1)

    def body(carry):
        j, _ = carry
        cur = offsets(n_grp * (i - j) + n_grp - 1)
        nxt = offsets(n_grp * (i - j) - 1)
        args = [arg_ref[d] for d in range(n_grp)]
        row_sums = [rs_ref[d] for d in range(n_grp)]
        zs = logits(nxt)
        cs, pvs = weights(cur, args, row_sums)
        nxt_args, nxt_sums = scan(nxt, zs, False)
        alive = reaches_past(cs, nxt_sums)
        accumulate(cs, pvs)
        stash(nxt_args, nxt_sums)
        return j + 1, alive

    j_last, _ = lax.while_loop(cond, body, (jnp.int32(0), reaches_past(cs_ref[...], first_sums)))
    last = offsets(n_grp * (i - j_last) + n_grp - 1)
    accumulate(*weights(last, [arg_ref[d] for d in range(n_grp)], [rs_ref[d] for d in range(n_grp)]))
    acc = acc_ref[...]
    o_ref[...] = jnp.where(low, acc[:TQA], acc[TQA:]).astype(o_ref.dtype)


def _sb_attention(proj, u2, B, S):
    nq = S // TQA
    cq, ck, cv = C_SBQ // LANES, C_SBK // LANES, C_SBV // LANES
    return pl.pallas_call(
        _sb_kernel,
        grid=(B, H_SB // 2, nq),
        in_specs=[pl.BlockSpec((TQA, LANES), lambda b, p, i: (b * nq + i, cq + p)),
                  pl.BlockSpec((S, LANES), lambda b, p, i: (b, ck + p)),
                  pl.BlockSpec((S, LANES), lambda b, p, i: (b, cv + p)),
                  pl.BlockSpec((2 * LANES, 2 * LANES), lambda b, p, i: (0, 0))],
        out_specs=pl.BlockSpec((TQA, LANES), lambda b, p, i: (b * nq + i, p)),
        out_shape=jax.ShapeDtypeStruct((B * S, W_HEADS), BF16),
        scratch_shapes=[pltpu.VMEM((2 * TQA, LANES), F32), pltpu.VMEM((2 * TQA, LANES), F32),
                        pltpu.VMEM((TQA // LANES, 2 * TQA, LANES), F32),
                        pltpu.VMEM((TQA // LANES, 2 * TQA, LANES), F32)],
        compiler_params=_cparams(("parallel", "parallel", "arbitrary")),
    )(proj, proj, proj, u2)


def _fox_kernel(q_ref, k_ref, vt_ref, fa_ref, fb_ref, o_ref, acc_ref, m_ref, l_ref, s_ref, cmax_ref):
    i = pl.program_id(2)
    rt = 2 * TQF
    n_diag = TQF // TQA
    low = lax.broadcasted_iota(jnp.int32, (TQF, LANES), 1) < HEAD_DIM
    qs = _stack_pair(q_ref[...] * SCALE, low)
    qa = jnp.concatenate([qs, jnp.concatenate([fa_ref[:, :LANES], fa_ref[:, LANES:]], axis=0)], axis=1)
    acc_ref[...] = jnp.zeros_like(acc_ref)
    l_ref[...] = jnp.zeros_like(l_ref)
    m_ref[...] = jnp.full(m_ref.shape, NEG, F32)

    def scores(kb, on_diagonal):
        off = pl.multiple_of(kb * TQA, TQA)
        kbias = jnp.concatenate([k_ref[pl.ds(off, TQA), :].astype(BF16), fb_ref[pl.ds(off, TQA), :]], axis=1)
        s = _dot_nt(kbias, qa)
        if on_diagonal:
            key_pos = lax.broadcasted_iota(jnp.int32, (TQA, rt), 0) + (kb - n_diag * i) * TQA
            q_pos = lax.broadcasted_iota(jnp.int32, (TQA, rt), 1) & (TQF - 1)
            s = jnp.where(key_pos <= q_pos, s, NEG)
        return s

    def stash(s):
        s_ref[...] = s
        cmax_ref[...] = jnp.max(s, axis=0, keepdims=True)

    def absorb(kb, s, cmax):
        m_old = m_ref[...]
        m_new = jnp.maximum(m_old, cmax)
        p = jnp.exp(s - m_new)
        alpha = jnp.exp(m_old - m_new)
        m_ref[...] = m_new
        l_ref[...] = alpha * l_ref[...] + jnp.sum(p, axis=0, keepdims=True)
        p = p.astype(BF16)
        vt = vt_ref[0, 0, kb]
        return alpha, [_dot(vt[e * HEAD_DIM:(e + 1) * HEAD_DIM, :], p[:, e * TQF:(e + 1) * TQF]) for e in range(2)]

    def accumulate(alpha, pvs):
        for e in range(2):
            acc_ref[e] = alpha[:, e * TQF:(e + 1) * TQF] * acc_ref[e] + pvs[e]

    d0 = n_diag * i
    stash(scores(d0, True))
    for d in range(1, n_diag):
        s_cur, cmax = s_ref[...], cmax_ref[...]
        s_next = scores(d0 + d, True)
        alpha, pvs = absorb(d0 + d - 1, s_cur, cmax)
        stash(s_next)
        accumulate(alpha, pvs)

    def body(j, carry):
        kb = jnp.where(j == 0, d0 + n_diag - 1, d0 - j)
        s_cur, cmax = s_ref[...], cmax_ref[...]
        s_next = scores(d0 - 1 - j, False)
        alpha, pvs = absorb(kb, s_cur, cmax)
        stash(s_next)
        accumulate(alpha, pvs)
        return carry

    lax.fori_loop(0, d0, body, 0)
    accumulate(*absorb(jnp.where(i == 0, n_diag - 1, 0), s_ref[...], cmax_ref[...]))
    l = l_ref[...]
    den = jnp.where(l > 0.0, l, 1.0)
    o_t = jnp.concatenate([acc_ref[0] / den[:, :TQF], acc_ref[1] / den[:, TQF:]], axis=0)
    o_ref[...] = o_t.T.astype(o_ref.dtype)


def _fox_attention(proj, v_t, fox_a, fox_b, B, S):
    nq = S // TQF
    cq, ck = C_FQ // LANES, C_FK // LANES
    return pl.pallas_call(
        _fox_kernel,
        grid=(B, H_FOX // 2, nq),
        in_specs=[pl.BlockSpec((TQF, LANES), lambda b, p, i: (b * nq + i, cq + p)),
                  pl.BlockSpec((S, LANES), lambda b, p, i: (b, ck + p)),
                  pl.BlockSpec((1, 1, S // TQA, LANES, TQA), lambda b, p, i: (b, p, 0, 0, 0)),
                  pl.BlockSpec((TQF, 2 * LANES), lambda b, p, i: (b * nq + i, p)),
                  pl.BlockSpec((S, LANES), lambda b, p, i: (b, p))],
        out_specs=pl.BlockSpec((TQF, LANES), lambda b, p, i: (b * nq + i, p)),
        out_shape=jax.ShapeDtypeStruct((B * S, W_HEADS), BF16),
        scratch_shapes=[pltpu.VMEM((2, HEAD_DIM, TQF), F32), pltpu.VMEM((1, 2 * TQF), F32),
                        pltpu.VMEM((1, 2 * TQF), F32), pltpu.VMEM((TQA, 2 * TQF), F32),
                        pltpu.VMEM((1, 2 * TQF), F32)],
        compiler_params=_cparams(("parallel", "parallel", "arbitrary")),
    )(proj, proj, v_t, fox_a, fox_b)


def _rope(x, cos, s_up, s_dn):
    return x * cos + pltpu.roll(x, ROPE_DIM // 2, axis=1) * s_up + pltpu.roll(x, LANES - ROPE_DIM // 2, axis=1) * s_dn


def _prep_kernel(misc_ref, bias_ref, pos_ref, rc_ref, q_ref, ks_ref, kw_ref, pa_ref, pb_ref, oa_ref, ob_ref,
                 vf_ref, vs_ref, vw_ref,
                 gates_ref, fa_ref, fb_ref, qr_ref, ksr_ref, kwr_ref, vft_ref, vst_ref, vwt_ref, carry_ref, *, ts):
    @pl.when(pl.program_id(1) == 0)
    def _():
        carry_ref[...] = jnp.zeros_like(carry_ref)

    logits = misc_ref[...]
    gates_ref[...] = jax.nn.sigmoid(logits)
    zf = logits + bias_ref[...]
    x = jnp.minimum(zf, 0.0) - jnp.log(1.0 + jnp.exp(-jnp.abs(zf)))
    row = lax.broadcasted_iota(jnp.int32, (ts, LANES), 0)
    sh = 1
    while sh < ts:
        x = x + jnp.where(row >= sh, pltpu.roll(x, sh, axis=0), 0.0)
        sh *= 2
    x = x + carry_ref[...]
    carry_ref[...] = x[ts - 1:ts, :]
    hi = x.astype(BF16)
    r1 = x - hi.astype(F32)
    mid = r1.astype(BF16)
    lo = (r1 - mid.astype(F32)).astype(BF16)
    parts = jnp.concatenate([hi, mid, lo], axis=1)
    fa_ref[...] = (_dot(parts, pa_ref[...]) + oa_ref[...]).astype(BF16)
    fb_ref[...] = (_dot(parts, pb_ref[...]) + ob_ref[...]).astype(BF16)

    ang = pos_ref[0].astype(F32) * rc_ref[0:1, :]
    cos = jnp.cos(ang)
    sin = jnp.sin(ang)
    s_up = sin * rc_ref[1:2, :]
    s_dn = sin * rc_ref[2:3, :]
    for a in range(W_HEADS // LANES):
        qa = q_ref[:, a * LANES:(a + 1) * LANES]
        qr_ref[:, a * LANES:(a + 1) * LANES] = _rope(qa, cos, s_up, s_dn) * SCALE
    ksr_ref[...] = _rope(ks_ref[...], cos, s_up, s_dn).astype(BF16)
    kwr_ref[...] = _rope(kw_ref[...], cos, s_up, s_dn).astype(BF16)

    for p in range(H_FOX // 2):
        for u in range(ts // TQA):
            vft_ref[0, p, u] = vf_ref[u * TQA:(u + 1) * TQA, p * LANES:(p + 1) * LANES].T.astype(BF16)
    for u in range(ts // LANES):
        vst_ref[0, u] = vs_ref[u * LANES:(u + 1) * LANES, :].T.astype(BF16)
        vwt_ref[0, u] = vw_ref[u * LANES:(u + 1) * LANES, :].T.astype(BF16)


def _prep(proj, bias_row, pos3, rope_c, place, B, S):
    ts = min(512, S)
    nt = S // ts
    T = B * S
    row_blk = lambda w: pl.BlockSpec((ts, w), lambda b, t: (b * nt + t, 0))
    full = lambda a: pl.BlockSpec(a.shape, lambda b, t: (0, 0))
    wa, wb = place[0].shape[1], place[1].shape[1]
    return pl.pallas_call(
        functools.partial(_prep_kernel, ts=ts),
        grid=(B, nt),
        in_specs=[pl.BlockSpec((ts, LANES), lambda b, t: (b * nt + t, C_MISC // LANES)),
                  pl.BlockSpec((1, LANES), lambda b, t: (0, 0)),
                  pl.BlockSpec((1, ts, 1), lambda b, t: (b, t, 0)),
                  pl.BlockSpec((8, LANES), lambda b, t: (0, 0)),
                  pl.BlockSpec((ts, W_HEADS), lambda b, t: (b * nt + t, C_NQ // W_HEADS)),
                  pl.BlockSpec((ts, LANES), lambda b, t: (b * nt + t, C_NKV // LANES + 2)),
                  pl.BlockSpec((ts, LANES), lambda b, t: (b * nt + t, C_NKV // LANES + 4)),
                  full(place[0]), full(place[1]), full(place[2]), full(place[3]),
                  pl.BlockSpec((ts, W_HEADS), lambda b, t: (b * nt + t, C_FV // W_HEADS)),
                  pl.BlockSpec((ts, LANES), lambda b, t: (b * nt + t, C_NKV // LANES + 3)),
                  pl.BlockSpec((ts, LANES), lambda b, t: (b * nt + t, C_NKV // LANES + 5))],
        out_specs=[row_blk(LANES), row_blk(wa), row_blk(wb), row_blk(W_HEADS), row_blk(LANES), row_blk(LANES),
                   pl.BlockSpec((1, H_FOX // 2, ts // TQA, LANES, TQA), lambda b, t: (b, 0, t, 0, 0)),
                   pl.BlockSpec((1, ts // LANES, LANES, LANES), lambda b, t: (b, t, 0, 0)),
                   pl.BlockSpec((1, ts // LANES, LANES, LANES), lambda b, t: (b, t, 0, 0))],
        out_shape=[jax.ShapeDtypeStruct((T, LANES), F32),
                   jax.ShapeDtypeStruct((T, wa), BF16),
                   jax.ShapeDtypeStruct((T, wb), BF16),
                   jax.ShapeDtypeStruct((T, W_HEADS), F32),
                   jax.ShapeDtypeStruct((T, LANES), BF16),
                   jax.ShapeDtypeStruct((T, LANES), BF16),
                   jax.ShapeDtypeStruct((B, H_FOX // 2, S // TQA, LANES, TQA), BF16),
                   jax.ShapeDtypeStruct((B, S // LANES, LANES, LANES), BF16),
                   jax.ShapeDtypeStruct((B, S // LANES, LANES, LANES), BF16)],
        scratch_shapes=[pltpu.VMEM((1, LANES), F32)],
        compiler_params=_cparams(("parallel", "arbitrary")),
    )(proj, bias_row, pos3, rope_c, proj, proj, proj, *place, proj, proj, proj)


def _compress_kernel(kc_ref, vc_ref, pek_ref, w1k_ref, b1k_ref, w2k_ref,
                     pev_ref, w1v_ref, b1v_ref, w2v_ref, ok_ref, ov_ref, *, nc):
    for x_ref, pe_ref, w1_ref, b1_ref, w2_ref, o_ref in (
            (kc_ref, pek_ref, w1k_ref, b1k_ref, w2k_ref, ok_ref),
            (vc_ref, pev_ref, w1v_ref, b1v_ref, w2v_ref, ov_ref)):
        h_first = jnp.zeros((nc, NSA_KV_HEADS * CMP_HIDDEN), F32)
        h_second = jnp.zeros((nc, NSA_KV_HEADS * CMP_HIDDEN), F32)
        for l in range(CMP_STRIDE):
            x = x_ref[pl.ds(l, nc, stride=CMP_STRIDE), :]
            h_first = h_first + _dot((x + pe_ref[l:l + 1, :]).astype(BF16), w1_ref[l])
            h_second = h_second + _dot((x + pe_ref[CMP_STRIDE + l:CMP_STRIDE + l + 1, :]).astype(BF16),
                                       w1_ref[CMP_STRIDE + l])
        h = h_first + pltpu.roll(h_second, nc - 1, axis=0) + b1_ref[...]
        a = h * jax.nn.sigmoid(h)
        out = _dot(a.astype(BF16), w2_ref[...])
        o_ref[0] = (out.T if o_ref is ov_ref else out).astype(o_ref.dtype)


def _compress(proj, pek, w1k, b1k, w2k, pev, w1v, b1v, w2v, B, S):
    nc = S // CMP_STRIDE
    full = lambda a: pl.BlockSpec(a.shape, lambda b: (0,) * a.ndim)
    out_spec = pl.BlockSpec((1, nc, LANES), lambda b: (b, 0, 0))
    out_spec_t = pl.BlockSpec((1, LANES, nc), lambda b: (b, 0, 0))
    return pl.pallas_call(
        functools.partial(_compress_kernel, nc=nc),
        grid=(B,),
        in_specs=[pl.BlockSpec((S, LANES), lambda b: (b, C_NKV // LANES)),
                  pl.BlockSpec((S, LANES), lambda b: (b, C_NKV // LANES + 1)),
                  full(pek), full(w1k), full(b1k), full(w2k),
                  full(pev), full(w1v), full(b1v), full(w2v)],
        out_specs=[out_spec, out_spec_t],
        out_shape=[jax.ShapeDtypeStruct((B, nc, LANES), BF16), jax.ShapeDtypeStruct((B, LANES, nc), BF16)],
        compiler_params=_cparams(("parallel",)),
    )(proj, proj, pek, w1k, b1k, w2k, pev, w1v, b1v, w2v)


def _compress_weights(pe, w1, b1, w2):
    G = NSA_KV_HEADS
    pe2 = jnp.tile(pe, (1, G))
    w1l = w1.astype(BF16).reshape(CMP_LEN, HEAD_DIM, CMP_HIDDEN)
    w2b = w2.astype(BF16)
    z1 = jnp.zeros_like(w1l)
    z2 = jnp.zeros_like(w2b)
    w1_bd = jnp.concatenate([jnp.concatenate([w1l, z1], axis=2), jnp.concatenate([z1, w1l], axis=2)], axis=1)
    w2_bd = jnp.concatenate([jnp.concatenate([w2b, z2], axis=1), jnp.concatenate([z2, w2b], axis=1)], axis=0)
    return pe2, w1_bd, jnp.tile(b1.reshape(1, -1), (1, G)), w2_bd


def _nsa_kernel(qn_ref, qr_ref, kc_ref, vct_ref, ks_ref, vst_ref, kw_ref, vwt_ref, gt_ref, ot_ref,
                o_ref, acc_ref, m_ref, l_ref, s_ref, cmax_ref, *, S):
    i = pl.program_id(1)
    nc = S // CMP_STRIDE
    ns = S // SEL_BLOCK
    k_top = min(SEL_TOPK, ns)
    G = NSA_KV_HEADS
    R = H_NSA // G
    RT = R * TQ
    HD = HEAD_DIM
    low = lax.broadcasted_iota(jnp.int32, (TQ, LANES), 1) < HD
    gates_t = gt_ref[...].T
    qn = qn_ref[...] * SCALE
    qr = qr_ref[...]

    def tiled(mask, s, fill):
        return jnp.concatenate([jnp.where(mask, s[:, r * TQ:(r + 1) * TQ], fill) for r in range(R)], axis=1)

    def stack(qfull, g):
        in_g = low if g == 0 else jnp.logical_not(low)
        parts = []
        for r in range(R):
            a, b = divmod(R * g + r, 2)
            blk = qfull[:, a * LANES:(a + 1) * LANES]
            if b != g:
                blk = pltpu.roll(blk, HD, axis=1)
            parts.append(jnp.where(in_g, blk, 0.0))
        return jnp.concatenate(parts, axis=0).astype(BF16)

    def rows_of(g, x):
        return x[g * HD:(g + 1) * HD, :]


    qn_s = [stack(qn, g) for g in range(G)]
    c_valid = (lax.broadcasted_iota(jnp.int32, (nc, TQ), 0) * CMP_STRIDE + (CMP_LEN - 1)
               <= lax.broadcasted_iota(jnp.int32, (nc, TQ), 1) + i * TQ)
    scs = [tiled(c_valid, _dot_nt(kc_ref[0], qn_s[g]), NEG) for g in range(G)]
    es = [tiled(c_valid, jnp.exp(sc - jnp.max(sc, axis=0, keepdims=True)), 0.0) for sc in scs]
    dens = [jnp.sum(e, axis=0, keepdims=True) for e in es]
    p_cmps = [e / jnp.where(den > 0.0, den, 1.0) for e, den in zip(es, dens)]
    vct = vct_ref[0]
    o_cmps = [_dot(rows_of(g, vct), p_cmps[g].astype(BF16)) for g in range(G)]

    p_slcs = []
    for p in p_cmps:
        p_sum = p[:, 0:TQ]
        for r in range(1, R):
            p_sum = p_sum + p[:, r * TQ:(r + 1) * TQ]
        p_slcs.append(_dot(ot_ref[...], _split_bf16(p_sum, 0)))
    n_idx = lax.broadcasted_iota(jnp.int32, (ns, TQ), 0)
    t_s = lax.broadcasted_iota(jnp.int32, (ns, TQ), 1) + i * TQ
    forced = jnp.logical_or(n_idx == 0, n_idx == (t_s >> SEL_SHIFT))
    s_valid = n_idx * SEL_BLOCK <= t_s
    scores = [jnp.where(forced, 1e4, jnp.where(s_valid, p_slc, -1.0)) for p_slc in p_slcs]
    ranks = [jnp.zeros((ns, TQ), F32) for _ in range(G)]
    for m_i in range(ns):
        for g in range(G):
            sm = scores[g][m_i:m_i + 1, :]
            ge = jnp.where(sm >= scores[g], 1.0, 0.0)
            gt = jnp.where(sm > scores[g], 1.0, 0.0)
            ranks[g] = ranks[g] + jnp.where(n_idx > m_i, ge, gt)
    sels = []
    for g in range(G):
        sel = jnp.where(ranks[g] < k_top, 1.0, 0.0)
        if ns < LANES:
            sel = jnp.concatenate([sel, jnp.zeros((LANES - ns, TQ), F32)], axis=0)
        sels.append(sel.astype(BF16))

    qr_s = [stack(qr, g) for g in range(G)]
    m_ref[...] = jnp.full(m_ref.shape, NEG, F32)
    l_ref[...] = jnp.zeros_like(l_ref)
    acc_ref[...] = jnp.zeros_like(acc_ref)

    def stash(scored):
        for c, sc in scored:
            s_ref[c, 0:sc.shape[0], :] = sc
            cmax_ref[c] = jnp.max(sc, axis=0, keepdims=True)

    def absorb(items):
        stats = []
        for c, _, cmax, _ in items:
            m_old = m_ref[c]
            stats.append((m_old, jnp.maximum(m_old, cmax)))
        ps = [jnp.exp(sc - m_new) for (_, sc, _, _), (_, m_new) in zip(items, stats)]
        out = []
        for (c, _, _, vt), (m_old, m_new), p in zip(items, stats, ps):
            alpha = jnp.exp(m_old - m_new)
            m_ref[c] = m_new
            l_ref[c] = alpha * l_ref[c] + jnp.sum(p, axis=0, keepdims=True)
            out.append((c, alpha, _dot(vt, p.astype(BF16))))
        return out

    def accumulate(updates):
        for c, alpha, pv in updates:
            acc_ref[c] = alpha * acc_ref[c] + pv

    def chosen(g, off, width):
        blk = (lax.broadcasted_iota(jnp.int32, (width, LANES), 0) + off) >> SEL_SHIFT
        expand = jnp.where(blk == lax.broadcasted_iota(jnp.int32, (width, LANES), 1), 1.0, 0.0).astype(BF16)
        return _dot(expand, sels[g]) > 0.5

    n_win = jnp.minimum(i, (WINDOW + TQ - 2) // LANES) + 1
    n_w = n_win + ((i + 1 - n_win) & 1)
    n_old = (i + 1 - n_w) // 2

    def win_scores(j):
        off = pl.multiple_of((i - j) * LANES, LANES)
        s_pos = lax.broadcasted_iota(jnp.int32, (LANES, TQ), 0) + off
        t_q = lax.broadcasted_iota(jnp.int32, (LANES, TQ), 1) + i * TQ
        causal = s_pos <= t_q
        band = jnp.logical_and(causal, t_q - s_pos < WINDOW)
        ks_t, kw_t = ks_ref[pl.ds(off, LANES), :], kw_ref[pl.ds(off, LANES), :]
        scored = []
        for g in range(G):
            pick = jnp.logical_and(chosen(g, off, LANES), causal)
            scored.append((2 * g, tiled(pick, _dot_nt(ks_t, qr_s[g]), NEG)))
            scored.append((2 * g + 1, tiled(band, _dot_nt(kw_t, qr_s[g]), NEG)))
        return scored

    def win_items(j):
        vs_t, vw_t = vst_ref[0, i - j], vwt_ref[0, i - j]
        items = []
        for g in range(G):
            items.append((2 * g, s_ref[2 * g, 0:LANES, :], cmax_ref[2 * g], rows_of(g, vs_t)))
            items.append((2 * g + 1, s_ref[2 * g + 1, 0:LANES, :], cmax_ref[2 * g + 1], rows_of(g, vw_t)))
        return items

    def old_first_tile(j):
        return i - n_w - 2 * j - 1

    def old_scores(j):
        off = pl.multiple_of(old_first_tile(j) * LANES, LANES)
        ks_t = ks_ref[pl.ds(off, 2 * LANES), :]
        return [(2 * g, tiled(chosen(g, off, 2 * LANES), _dot_nt(ks_t, qr_s[g]), NEG)) for g in range(G)]

    def old_items(j):
        kb = old_first_tile(j)
        vs_t = jnp.concatenate([vst_ref[0, kb], vst_ref[0, kb + 1]], axis=1)
        return [(2 * g, s_ref[2 * g], cmax_ref[2 * g], rows_of(g, vs_t)) for g in range(G)]

    stash(win_scores(0))

    def win_body(j, carry):
        items = win_items(j - 1)
        scored = win_scores(j)
        updates = absorb(items)
        stash(scored)
        accumulate(updates)
        return carry

    lax.fori_loop(1, n_w, win_body, 0)
    accumulate(absorb(win_items(n_w - 1)))

    @pl.when(n_old > 0)
    def _():
        stash(old_scores(0))

        def old_body(j, carry):
            items = old_items(j - 1)
            scored = old_scores(j)
            updates = absorb(items)
            stash(scored)
            accumulate(updates)
            return carry

        lax.fori_loop(1, n_old, old_body, 0)
        accumulate(absorb(old_items(n_old - 1)))

    heads = [None] * H_NSA
    for g in range(G):
        l_sel, l_win = l_ref[2 * g], l_ref[2 * g + 1]
        o_sel = acc_ref[2 * g] / jnp.where(l_sel > 0.0, l_sel, 1.0)
        o_win = acc_ref[2 * g + 1] / jnp.where(l_win > 0.0, l_win, 1.0)
        for r in range(R):
            h = R * g + r
            cols = slice(r * TQ, (r + 1) * TQ)
            heads[h] = (gates_t[3 * h:3 * h + 1, :] * o_cmps[g][:, cols]
                        + gates_t[3 * h + 1:3 * h + 2, :] * o_sel[:, cols]
                        + gates_t[3 * h + 2:3 * h + 3, :] * o_win[:, cols])
    for a in range(H_NSA // 2):
        pair = jnp.concatenate([heads[2 * a], heads[2 * a + 1]], axis=0)
        o_ref[:, a * LANES:(a + 1) * LANES] = pair.T.astype(o_ref.dtype)


def _nsa_attention(proj, q_rope, k_cmp, v_cmp_t, ks_r, vs_t, kw_r, vw_t, gates, overlap_t, B, S):
    nq = S // TQ
    nc = S // CMP_STRIDE
    nk = S // LANES
    G = NSA_KV_HEADS
    rt = H_NSA // G * TQ
    k_full = pl.BlockSpec((S, LANES), lambda b, i: (b, 0))
    vt_full = pl.BlockSpec((1, nk, LANES, LANES), lambda b, i: (b, 0, 0, 0))
    return pl.pallas_call(
        functools.partial(_nsa_kernel, S=S),
        grid=(B, nq),
        in_specs=[pl.BlockSpec((TQ, W_HEADS), lambda b, i: (b * nq + i, C_NQ // W_HEADS)),
                  pl.BlockSpec((TQ, W_HEADS), lambda b, i: (b * nq + i, 0)),
                  pl.BlockSpec((1, nc, LANES), lambda b, i: (b, 0, 0)),
                  pl.BlockSpec((1, LANES, nc), lambda b, i: (b, 0, 0)),
                  k_full, vt_full, k_full, vt_full,
                  pl.BlockSpec((TQ, LANES), lambda b, i: (b * nq + i, 0)),
                  pl.BlockSpec(overlap_t.shape, lambda b, i: (0, 0))],
        out_specs=pl.BlockSpec((TQ, W_HEADS), lambda b, i: (b * nq + i, 0)),
        out_shape=jax.ShapeDtypeStruct((B * S, W_HEADS), BF16),
        scratch_shapes=[pltpu.VMEM((2 * G, HEAD_DIM, rt), F32), pltpu.VMEM((2 * G, 1, rt), F32),
                        pltpu.VMEM((2 * G, 1, rt), F32), pltpu.VMEM((2 * G, 2 * LANES, rt), F32),
                        pltpu.VMEM((2 * G, 1, rt), F32)],
        compiler_params=_cparams(("parallel", "arbitrary")),
    )(proj, q_rope, k_cmp, v_cmp_t, ks_r, vs_t, kw_r, vw_t, gates, overlap_t)


def _merge_kernel(x_ref, osb_ref, onsa_ref, ofox_ref, m0_ref, m1_ref, m2_ref,
                  wsb_ref, wnsa_ref, wfox_ref, wout_ref, g_ref, o_ref):
    y = jax.nn.sigmoid(m0_ref[...]) * _dot(osb_ref[...], wsb_ref[...])
    y = y + jax.nn.sigmoid(m1_ref[...]) * _dot(onsa_ref[...], wnsa_ref[...])
    y = y + jax.nn.sigmoid(m2_ref[...]) * _dot(ofox_ref[...], wfox_ref[...])
    z = _dot(y.astype(BF16), wout_ref[...])
    o_ref[...] = x_ref[...] + _rms(z, g_ref[...])


def _merge(x, o_sb, o_nsa, o_fox, proj, w_sb, w_nsa, w_fox, w_out, g, tm):
    T, D = x.shape
    row = lambda w: pl.BlockSpec((tm, w), lambda i: (i, 0))
    full = lambda a: pl.BlockSpec(a.shape, lambda i: (0, 0))
    gate = lambda c: pl.BlockSpec((tm, D), lambda i: (i, C_MERGE // D + c))
    return pl.pallas_call(
        _merge_kernel,
        grid=(T // tm,),
        in_specs=[row(D), row(W_HEADS), row(W_HEADS), row(W_HEADS), gate(0), gate(1), gate(2),
                  full(w_sb), full(w_nsa), full(w_fox), full(w_out), full(g)],
        out_specs=row(D),
        out_shape=jax.ShapeDtypeStruct((T, D), F32),
        compiler_params=_cparams(("parallel",)),
    )(x, o_sb, o_nsa, o_fox, proj, proj, proj, w_sb, w_nsa, w_fox, w_out, g)


def _mem_kv_kernel(mem_ref, g_ref, wk_ref, wv_ref, k_ref, v_ref):
    mn = _rms(mem_ref[0], g_ref[...]).astype(BF16)
    k_ref[0] = _dot(mn, wk_ref[...]).astype(BF16)
    v_ref[0] = _dot(mn, wv_ref[...]).astype(BF16)


def _mem_kv(mem, g, wk, wv):
    B, M, D = mem.shape
    full = lambda a: pl.BlockSpec(a.shape, lambda b: (0, 0))
    out_spec = pl.BlockSpec((1, M, W_MEM), lambda b: (b, 0, 0))
    return pl.pallas_call(
        _mem_kv_kernel,
        grid=(B,),
        in_specs=[pl.BlockSpec((1, M, D), lambda b: (b, 0, 0)), full(g), full(wk), full(wv)],
        out_specs=[out_spec, out_spec],
        out_shape=[jax.ShapeDtypeStruct((B, M, W_MEM), BF16)] * 2,
        compiler_params=_cparams(("parallel",)),
    )(mem, g, wk, wv)


def _mem_attn_kernel(x_ref, g_ref, wq_ref, k_ref, v_ref, wo_ref, gp_ref, o_ref):
    x = x_ref[...]
    q = _dot(_rms(x, g_ref[...]).astype(BF16), wq_ref[...]) * SCALE
    k = k_ref[0]
    v = v_ref[0]
    lane = lax.broadcasted_iota(jnp.int32, q.shape, 1)
    o = jnp.zeros(q.shape, F32)
    for h in range(H_MEM):
        in_h = jnp.logical_and(lane >= h * HEAD_DIM, lane < (h + 1) * HEAD_DIM)
        s = _dot_nt(jnp.where(in_h, q, 0.0).astype(BF16), k)
        e = jnp.exp(s - jnp.max(s, axis=-1, keepdims=True))
        p = e / jnp.sum(e, axis=-1, keepdims=True)
        o = o + jnp.where(in_h, _dot(p.astype(BF16), v), 0.0)
    y = _dot(o.astype(BF16), wo_ref[...])
    o_ref[...] = x + _rms(y, gp_ref[...])


def _mem_attn(x, g_pre, wq, k, v, wo, g_post, B, S, tm):
    T, D = x.shape
    M = k.shape[1]
    nt = S // tm
    full = lambda a: pl.BlockSpec(a.shape, lambda b, i: (0, 0))
    kv_spec = pl.BlockSpec((1, M, W_MEM), lambda b, i: (b, 0, 0))
    row = pl.BlockSpec((tm, D), lambda b, i: (b * nt + i, 0))
    return pl.pallas_call(
        _mem_attn_kernel,
        grid=(B, nt),
        in_specs=[row, full(g_pre), full(wq), kv_spec, kv_spec, full(wo), full(g_post)],
        out_specs=row,
        out_shape=jax.ShapeDtypeStruct((T, D), F32),
        compiler_params=_cparams(("parallel", "parallel")),
    )(x, g_pre, wq, k, v, wo, g_post)


def _ffn_kernel(x_ref, g_ref, wg_ref, wu_ref, wd_ref, gp_ref, o_ref, h_ref, acc_ref):
    j = pl.program_id(1)

    @pl.when(j == 0)
    def _():
        h_ref[...] = _rms(x_ref[...], g_ref[...]).astype(BF16)
        acc_ref[...] = jnp.zeros_like(acc_ref)

    h = h_ref[...]
    a = _dot(h, wg_ref[...])
    u = _dot(h, wu_ref[...])
    acc_ref[...] += _dot((a * jax.nn.sigmoid(a) * u).astype(BF16), wd_ref[...])

    @pl.when(j == pl.num_programs(1) - 1)
    def _():
        o_ref[...] = x_ref[...] + _rms(acc_ref[...], gp_ref[...])


def _ffn(x, g_pre, wg, wu, wd, g_post, tm, tf):
    T, D = x.shape
    F = wg.shape[1]
    row = pl.BlockSpec((tm, D), lambda i, j: (i, 0))
    vec = pl.BlockSpec((1, D), lambda i, j: (0, 0))
    return pl.pallas_call(
        _ffn_kernel,
        grid=(T // tm, F // tf),
        in_specs=[row, vec,
                  pl.BlockSpec((D, tf), lambda i, j: (0, j)),
                  pl.BlockSpec((D, tf), lambda i, j: (0, j)),
                  pl.BlockSpec((tf, D), lambda i, j: (j, 0)),
                  vec],
        out_specs=row,
        out_shape=jax.ShapeDtypeStruct((T, D), F32),
        scratch_shapes=[pltpu.VMEM((tm, D), BF16), pltpu.VMEM((tm, D), F32)],
        compiler_params=_cparams(("parallel", "arbitrary")),
    )(x, g_pre, wg, wu, wd, g_post)


def _scan_matrix():
    j = np.arange(LANES)
    later = (j[:, None] > j[None, :]).astype(np.float32)
    u = np.concatenate([later, np.ones((LANES, LANES), np.float32)], axis=1)
    return jnp.asarray(np.concatenate([u, u], axis=0), dtype=BF16)


def _overlap_t(S):
    nc, ns = S // CMP_STRIDE, S // SEL_BLOCK
    c0 = np.arange(nc) * CMP_STRIDE
    n0 = np.arange(ns) * SEL_BLOCK
    ov = (c0[None, :] < n0[:, None] + SEL_BLOCK) & (c0[None, :] + CMP_LEN > n0[:, None])
    ov = ov & (np.arange(nc)[None, :] < nc - 1)
    ov = ov.astype(np.float32)
    return jnp.asarray(np.concatenate([ov, ov], axis=1), dtype=BF16)


FOX_F_LANE = 24
N_PARTS = 3


def _fox_bias_placement():
    n_pair = H_FOX // 2
    pa = np.zeros((N_PARTS * LANES, n_pair * 2 * LANES), np.float32)
    pb = np.zeros((N_PARTS * LANES, n_pair * LANES), np.float32)
    oa = np.zeros((1, n_pair * 2 * LANES), np.float32)
    ob = np.zeros((1, n_pair * LANES), np.float32)
    for p in range(n_pair):
        for e in range(2):
            src = FOX_F_LANE + 2 * p + e
            for x in range(N_PARTS):
                pa[x * LANES + src, (2 * p + e) * LANES + 8 * e + x] = 1.0
                oa[0, (2 * p + e) * LANES + 8 * e + N_PARTS + x] = 1.0
                pb[x * LANES + src, p * LANES + 8 * e + N_PARTS + x] = -1.0
                ob[0, p * LANES + 8 * e + x] = 1.0
    return (jnp.asarray(pa, dtype=BF16), jnp.asarray(pb, dtype=BF16), jnp.asarray(oa), jnp.asarray(ob))


def _rope_rows():
    half = ROPE_DIM // 2
    inv_freq = ROPE_THETA ** (-jnp.arange(half, dtype=F32) / half)
    d = np.arange(LANES) % HEAD_DIM
    rot = d < ROPE_DIM
    freq = jnp.where(jnp.asarray(rot), inv_freq[jnp.asarray(d % half)], 0.0)
    s_up = jnp.asarray(((d >= half) & rot).astype(np.float32))
    s_dn = jnp.asarray(-(d < half).astype(np.float32))
    rows = jnp.stack([freq, s_up, s_dn] + [jnp.zeros((LANES,), F32)] * 5)
    return rows.astype(F32)


def _reorder_w_in(w):
    pad = jnp.zeros((w.shape[0], MISC_W - (_O_FOX_Q - _O_NSA_G) - (_O_MERGE - _O_FOX_F)), w.dtype)
    return jnp.concatenate([
        w[:, _O_MERGE:_O_END],
        w[:, :_O_NSA_G - 6 * LANES],
        w[:, _O_FOX_Q:_O_FOX_F],
        w[:, _O_NSA_G - 6 * LANES:_O_NSA_G],
        w[:, _O_NSA_G:_O_FOX_Q],
        w[:, _O_FOX_F:_O_MERGE],
        pad], axis=1)


def _row_tile(n, target):
    t = min(n, target)
    while n % t:
        t //= 2
    return t


def kernel(x, mem, positions, g_pre_mix, g_post_mix, g_pre_mem, g_mem, g_post_mem, g_pre_ffn, g_post_ffn,
           w_in, b_fox_f, cmp_pe_k, cmp_w1_k, cmp_b1_k, cmp_w2_k, cmp_pe_v, cmp_w1_v, cmp_b1_v, cmp_w2_v,
           w_up_sb, w_up_nsa, w_up_fox, w_out, w_mem_q, w_mem_k, w_mem_v, w_mem_o,
           w_ffn_gate, w_ffn_up, w_ffn_down):
    B, S, D = x.shape
    T = B * S
    depth = w_in.shape[0]
    u2 = _scan_matrix()
    overlap_t = _overlap_t(S)
    rope_c = _rope_rows()
    place = _fox_bias_placement()
    pos3 = positions.reshape(B, S, 1)
    vec = lambda g: g.reshape(1, -1)
    tm_big = _row_tile(T, 1024)
    tm_mid = _row_tile(T, 512)
    d_ff = w_ffn_gate.shape[2]
    tf = d_ff // 2 if (d_ff // 2) % LANES == 0 else d_ff

    w_in_b = w_in.astype(BF16)
    xf = x.reshape(T, D)
    for l in range(depth):
        proj = _norm_matmul(xf, vec(g_pre_mix[l]), _reorder_w_in(w_in_b[l]), tm_big, N_IN // 3)

        o_sb = _sb_attention(proj, u2, B, S)

        bias_row = jnp.zeros((1, LANES), F32).at[0, FOX_F_LANE:FOX_F_LANE + H_FOX].set(b_fox_f[l])
        gates, fox_a, fox_b, q_rope, ks_r, kw_r, v_fox_t, vs_t, vw_t = _prep(
            proj, bias_row, pos3, rope_c, place, B, S)
        o_fox = _fox_attention(proj, v_fox_t, fox_a, fox_b, B, S)

        k_cmp, v_cmp_t = _compress(
            proj, *_compress_weights(cmp_pe_k[l], cmp_w1_k[l], cmp_b1_k[l], cmp_w2_k[l]),
            *_compress_weights(cmp_pe_v[l], cmp_w1_v[l], cmp_b1_v[l], cmp_w2_v[l]), B, S)
        o_nsa = _nsa_attention(proj, q_rope, k_cmp, v_cmp_t, ks_r, vs_t, kw_r, vw_t, gates, overlap_t, B, S)

        xf = _merge(xf, o_sb, o_nsa, o_fox, proj, w_up_sb[l].astype(BF16), w_up_nsa[l].astype(BF16),
                    w_up_fox[l].astype(BF16), w_out[l].astype(BF16), vec(g_post_mix[l]), tm_mid)

        k_mem, v_mem = _mem_kv(mem, vec(g_mem[l]), w_mem_k[l].astype(BF16), w_mem_v[l].astype(BF16))
        xf = _mem_attn(xf, vec(g_pre_mem[l]), w_mem_q[l].astype(BF16), k_mem, v_mem,
                       w_mem_o[l].astype(BF16), vec(g_post_mem[l]), B, S, _row_tile(S, 512))

        xf = _ffn(xf, vec(g_pre_ffn[l]), w_ffn_gate[l].astype(BF16), w_ffn_up[l].astype(BF16),
                  w_ffn_down[l].astype(BF16), vec(g_post_ffn[l]), tm_mid, tf)
    return xf.reshape(B, S, D)
```

```python
import functools

import numpy as np
import jax
import jax.numpy as jnp
from jax import lax
from jax.experimental import pallas as pl
from jax.experimental.pallas import tpu as pltpu

F32 = jnp.float32
BF16 = jnp.bfloat16

D_MODEL = 1024
HEAD_DIM = 64
H_SB = 8
H_NSA = 8
NSA_KV_HEADS = 2
H_FOX = 8
H_MEM = 4
N_BRANCH = 3
ROPE_THETA = 500000.0
ROPE_DIM = HEAD_DIM // 4
CMP_STRIDE = 16
CMP_LEN = 2 * CMP_STRIDE
CMP_HIDDEN = 256
SEL_BLOCK = 64
SEL_SHIFT = SEL_BLOCK.bit_length() - 1
SEL_TOPK = 8
WINDOW = 512
W_HEADS = 8 * HEAD_DIM
W_MEM = H_MEM * HEAD_DIM
EPS = 1e-6
SCALE = HEAD_DIM ** -0.5
NEG = -1e30
LOG_ZERO = -104.0

LANES = 128
TQ = 128
TQA = 256
TQF = 512

C_SBQ, C_SBK, C_SBV = 0, 512, 1024
C_NQ = 1536
C_FQ, C_FK, C_FV = 2048, 2560, 3072
C_NKV = 3584
C_MISC = 4352
MISC_W = 256
N_IN = 4608
_O_NSA_G, _O_FOX_Q, _O_FOX_F, _O_MERGE, _O_END = 2816, 2840, 4376, 4384, 7456

VMEM_LIMIT = 56 * 1024 * 1024


def _cparams(sem):
    return pltpu.CompilerParams(dimension_semantics=sem, vmem_limit_bytes=VMEM_LIMIT)


def _dot(a, b):
    return jnp.dot(a, b, preferred_element_type=F32)


def _dot_nt(a, b):
    return lax.dot_general(a, b, (((1,), (1,)), ((), ())), preferred_element_type=F32)


def _rms(x, g):
    ms = jnp.mean(x * x, axis=-1, keepdims=True)
    return x * lax.rsqrt(ms + EPS) * g


def _split_bf16(x, axis=1):
    hi = x.astype(BF16)
    lo = (x - hi.astype(F32)).astype(BF16)
    return jnp.concatenate([hi, lo], axis=axis)


def _norm_matmul_kernel(x_ref, g_ref, w_ref, o_ref, h_ref):
    @pl.when(pl.program_id(1) == 0)
    def _():
        h_ref[...] = _rms(x_ref[...], g_ref[...]).astype(BF16)

    o_ref[...] = _dot(h_ref[...], w_ref[...])


def _norm_matmul(x, g, w, tm, tn):
    T, D = x.shape
    N = w.shape[1]
    return pl.pallas_call(
        _norm_matmul_kernel,
        grid=(T // tm, N // tn),
        in_specs=[pl.BlockSpec((tm, D), lambda i, j: (i, 0)),
                  pl.BlockSpec((1, D), lambda i, j: (0, 0)),
                  pl.BlockSpec((D, tn), lambda i, j: (0, j))],
        out_specs=pl.BlockSpec((tm, tn), lambda i, j: (i, j)),
        out_shape=jax.ShapeDtypeStruct((T, N), F32),
        scratch_shapes=[pltpu.VMEM((tm, D), BF16)],
        compiler_params=_cparams(("parallel", "arbitrary")),
    )(x, g, w)


def _stack_pair(q, low):
    return jnp.concatenate([jnp.where(low, q, 0.0), jnp.where(low, 0.0, q)], axis=0).astype(BF16)


def _sb_kernel(q_ref, k_ref, v_ref, u_ref, o_ref, acc_ref, cs_ref, arg_ref, rs_ref):
    i = pl.program_id(2)
    rt = 2 * TQA
    n_grp = TQA // LANES
    lane = lax.broadcasted_iota(jnp.int32, (rt, LANES), 1)
    t_q = (lax.broadcasted_iota(jnp.int32, (rt, LANES), 0) & (TQA - 1)) + i * TQA
    low = lax.broadcasted_iota(jnp.int32, (TQA, LANES), 1) < HEAD_DIM
    qs = _stack_pair(q_ref[...] * SCALE, low)

    def offsets(kb_first):
        return [pl.multiple_of((kb_first - d) * LANES, LANES) for d in range(n_grp)]

    def logits(offs):
        return [_dot_nt(qs, k_ref[pl.ds(off, LANES), :].astype(BF16)) for off in offs]

    def scan(offs, zs, on_diagonal):
        log_keeps, log_betas, stricts = [], [], []
        for off, z in zip(offs, zs):
            log_keep = -(jnp.maximum(z, 0.0) + jnp.log(1.0 + jnp.exp(-jnp.abs(z))))
            log_betas.append(z + log_keep)
            if on_diagonal:
                stricts.append((lane + off) < t_q)
                log_keep = jnp.where(stricts[-1], log_keep, 0.0)
            log_keeps.append(log_keep)
        c2s = [_dot(_split_bf16(log_keep), u_ref[...]) for log_keep in log_keeps]
        args = [log_beta + c2[:, :LANES] for log_beta, c2 in zip(log_betas, c2s)]
        if on_diagonal:
            args = [jnp.where(strict, arg, NEG) for strict, arg in zip(stricts, args)]
        return args, [c2[:, LANES:] for c2 in c2s]

    def stash(args, row_sums):
        for d in range(n_grp):
            arg_ref[d] = args[d]
            rs_ref[d] = row_sums[d]

    def weights(offs, args, row_sums):
        cs = cs_ref[...]
        pvs = []
        for d, off in enumerate(offs):
            w = jnp.exp(args[d] + cs)
            pvs.append(_dot(w.astype(BF16), v_ref[pl.ds(off, LANES), :].astype(BF16)))
            cs = cs + row_sums[d]
        return cs, pvs

    def accumulate(cs, pvs):
        acc = acc_ref[...]
        for pv in pvs:
            acc = acc + pv
        acc_ref[...] = acc
        cs_ref[...] = cs

    acc_ref[...] = jnp.zeros_like(acc_ref)
    cs_ref[...] = jnp.zeros_like(cs_ref)
    def reaches_past(cs, row_sums):
        for rs in row_sums:
            cs = cs + rs
        return (jnp.max(cs) >= LOG_ZERO).astype(jnp.int32)

    first = offsets(n_grp * i + n_grp - 1)
    first_args, first_sums = scan(first, logits(first), True)
    stash(first_args, first_sums)

    def cond(carry):
        j, alive = carry
        return jnp.logical_and(j < i, alive == 1)

    def body(carry):
        j, _ = carry
        cur = offsets(n_grp * (i - j) + n_grp - 1)
        nxt = offsets(n_grp * (i - j) - 1)
        args = [arg_ref[d] for d in range(n_grp)]
        row_sums = [rs_ref[d] for d in range(n_grp)]
        zs = logits(nxt)
        cs, pvs = weights(cur, args, row_sums)
        nxt_args, nxt_sums = scan(nxt, zs, False)
        alive = reaches_past(cs, nxt_sums)
        accumulate(cs, pvs)
        stash(nxt_args, nxt_sums)
        return j + 1, alive

    j_last, _ = lax.while_loop(cond, body, (jnp.int32(0), reaches_past(cs_ref[...], first_sums)))
    last = offsets(n_grp * (i - j_last) + n_grp - 1)
    accumulate(*weights(last, [arg_ref[d] for d in range(n_grp)], [rs_ref[d] for d in range(n_grp)]))
    acc = acc_ref[...]
    o_ref[...] = jnp.where(low, acc[:TQA], acc[TQA:]).astype(o_ref.dtype)


def _sb_attention(proj, u2, B, S):
    nq = S // TQA
    cq, ck, cv = C_SBQ // LANES, C_SBK // LANES, C_SBV // LANES
    return pl.pallas_call(
        _sb_kernel,
        grid=(B, H_SB // 2, nq),
        in_specs=[pl.BlockSpec((TQA, LANES), lambda b, p, i: (b * nq + i, cq + p)),
                  pl.BlockSpec((S, LANES), lambda b, p, i: (b, ck + p)),
                  pl.BlockSpec((S, LANES), lambda b, p, i: (b, cv + p)),
                  pl.BlockSpec((2 * LANES, 2 * LANES), lambda b, p, i: (0, 0))],
        out_specs=pl.BlockSpec((TQA, LANES), lambda b, p, i: (b * nq + i, p)),
        out_shape=jax.ShapeDtypeStruct((B * S, W_HEADS), BF16),
        scratch_shapes=[pltpu.VMEM((2 * TQA, LANES), F32), pltpu.VMEM((2 * TQA, LANES), F32),
                        pltpu.VMEM((TQA // LANES, 2 * TQA, LANES), F32),
                        pltpu.VMEM((TQA // LANES, 2 * TQA, LANES), F32)],
        compiler_params=_cparams(("parallel", "parallel", "arbitrary")),
    )(proj, proj, proj, u2)


def _fox_kernel(q_ref, k_ref, vt_ref, fa_ref, fb_ref, o_ref, acc_ref, m_ref, l_ref, s_ref, cmax_ref):
    i = pl.program_id(2)
    rt = 2 * TQF
    n_diag = TQF // TQA
    low = lax.broadcasted_iota(jnp.int32, (TQF, LANES), 1) < HEAD_DIM
    qs = _stack_pair(q_ref[...] * SCALE, low)
    qa = jnp.concatenate([qs, jnp.concatenate([fa_ref[:, :LANES], fa_ref[:, LANES:]], axis=0)], axis=1)
    acc_ref[...] = jnp.zeros_like(acc_ref)
    l_ref[...] = jnp.zeros_like(l_ref)
    m_ref[...] = jnp.full(m_ref.shape, NEG, F32)

    def scores(kb, on_diagonal):
        off = pl.multiple_of(kb * TQA, TQA)
        kbias = jnp.concatenate([k_ref[pl.ds(off, TQA), :].astype(BF16), fb_ref[pl.ds(off, TQA), :]], axis=1)
        s = _dot_nt(kbias, qa)
        if on_diagonal:
            key_pos = lax.broadcasted_iota(jnp.int32, (TQA, rt), 0) + (kb - n_diag * i) * TQA
            q_pos = lax.broadcasted_iota(jnp.int32, (TQA, rt), 1) & (TQF - 1)
            s = jnp.where(key_pos <= q_pos, s, NEG)
        return s

    def stash(s):
        s_ref[...] = s
        cmax_ref[...] = jnp.max(s, axis=0, keepdims=True)

    def absorb(kb, s, cmax):
        m_old = m_ref[...]
        m_new = jnp.maximum(m_old, cmax)
        p = jnp.exp(s - m_new)
        alpha = jnp.exp(m_old - m_new)
        m_ref[...] = m_new
        l_ref[...] = alpha * l_ref[...] + jnp.sum(p, axis=0, keepdims=True)
        p = p.astype(BF16)
        vt = vt_ref[0, 0, kb]
        return alpha, [_dot(vt[e * HEAD_DIM:(e + 1) * HEAD_DIM, :], p[:, e * TQF:(e + 1) * TQF]) for e in range(2)]

    def accumulate(alpha, pvs):
        for e in range(2):
            acc_ref[e] = alpha[:, e * TQF:(e + 1) * TQF] * acc_ref[e] + pvs[e]

    d0 = n_diag * i
    stash(scores(d0, True))
    for d in range(1, n_diag):
        s_cur, cmax = s_ref[...], cmax_ref[...]
        s_next = scores(d0 + d, True)
        alpha, pvs = absorb(d0 + d - 1, s_cur, cmax)
        stash(s_next)
        accumulate(alpha, pvs)

    def body(j, carry):
        kb = jnp.where(j == 0, d0 + n_diag - 1, d0 - j)
        s_cur, cmax = s_ref[...], cmax_ref[...]
        s_next = scores(d0 - 1 - j, False)
        alpha, pvs = absorb(kb, s_cur, cmax)
        stash(s_next)
        accumulate(alpha, pvs)
        return carry

    lax.fori_loop(0, d0, body, 0)
    accumulate(*absorb(jnp.where(i == 0, n_diag - 1, 0), s_ref[...], cmax_ref[...]))
    l = l_ref[...]
    den = jnp.where(l > 0.0, l, 1.0)
    o_t = jnp.concatenate([acc_ref[0] / den[:, :TQF], acc_ref[1] / den[:, TQF:]], axis=0)
    o_ref[...] = o_t.T.astype(o_ref.dtype)


def _fox_attention(proj, v_t, fox_a, fox_b, B, S):
    nq = S // TQF
    cq, ck = C_FQ // LANES, C_FK // LANES
    return pl.pallas_call(
        _fox_kernel,
        grid=(B, H_FOX // 2, nq),
        in_specs=[pl.BlockSpec((TQF, LANES), lambda b, p, i: (b * nq + i, cq + p)),
                  pl.BlockSpec((S, LANES), lambda b, p, i: (b, ck + p)),
                  pl.BlockSpec((1, 1, S // TQA, LANES, TQA), lambda b, p, i: (b, p, 0, 0, 0)),
                  pl.BlockSpec((TQF, 2 * LANES), lambda b, p, i: (b * nq + i, p)),
                  pl.BlockSpec((S, LANES), lambda b, p, i: (b, p))],
        out_specs=pl.BlockSpec((TQF, LANES), lambda b, p, i: (b * nq + i, p)),
        out_shape=jax.ShapeDtypeStruct((B * S, W_HEADS), BF16),
        scratch_shapes=[pltpu.VMEM((2, HEAD_DIM, TQF), F32), pltpu.VMEM((1, 2 * TQF), F32),
                        pltpu.VMEM((1, 2 * TQF), F32), pltpu.VMEM((TQA, 2 * TQF), F32),
                        pltpu.VMEM((1, 2 * TQF), F32)],
        compiler_params=_cparams(("parallel", "parallel", "arbitrary")),
    )(proj, proj, v_t, fox_a, fox_b)


def _rope(x, cos, s_up, s_dn):
    return x * cos + pltpu.roll(x, ROPE_DIM // 2, axis=1) * s_up + pltpu.roll(x, LANES - ROPE_DIM // 2, axis=1) * s_dn


def _prep_kernel(misc_ref, bias_ref, pos_ref, rc_ref, q_ref, ks_ref, kw_ref, pa_ref, pb_ref, oa_ref, ob_ref,
                 vf_ref, vs_ref, vw_ref,
                 gates_ref, fa_ref, fb_ref, qr_ref, ksr_ref, kwr_ref, vft_ref, vst_ref, vwt_ref, carry_ref, *, ts):
    @pl.when(pl.program_id(1) == 0)
    def _():
        carry_ref[...] = jnp.zeros_like(carry_ref)

    logits = misc_ref[...]
    gates_ref[...] = jax.nn.sigmoid(logits)
    zf = logits + bias_ref[...]
    x = jnp.minimum(zf, 0.0) - jnp.log(1.0 + jnp.exp(-jnp.abs(zf)))
    row = lax.broadcasted_iota(jnp.int32, (ts, LANES), 0)
    sh = 1
    while sh < ts:
        x = x + jnp.where(row >= sh, pltpu.roll(x, sh, axis=0), 0.0)
        sh *= 2
    x = x + carry_ref[...]
    carry_ref[...] = x[ts - 1:ts, :]
    hi = x.astype(BF16)
    r1 = x - hi.astype(F32)
    mid = r1.astype(BF16)
    lo = (r1 - mid.astype(F32)).astype(BF16)
    parts = jnp.concatenate([hi, mid, lo], axis=1)
    fa_ref[...] = (_dot(parts, pa_ref[...]) + oa_ref[...]).astype(BF16)
    fb_ref[...] = (_dot(parts, pb_ref[...]) + ob_ref[...]).astype(BF16)

    ang = pos_ref[0].astype(F32) * rc_ref[0:1, :]
    cos = jnp.cos(ang)
    sin = jnp.sin(ang)
    s_up = sin * rc_ref[1:2, :]
    s_dn = sin * rc_ref[2:3, :]
    for a in range(W_HEADS // LANES):
        qa = q_ref[:, a * LANES:(a + 1) * LANES]
        qr_ref[:, a * LANES:(a + 1) * LANES] = _rope(qa, cos, s_up, s_dn) * SCALE
    ksr_ref[...] = _rope(ks_ref[...], cos, s_up, s_dn).astype(BF16)
    kwr_ref[...] = _rope(kw_ref[...], cos, s_up, s_dn).astype(BF16)

    for p in range(H_FOX // 2):
        for u in range(ts // TQA):
            vft_ref[0, p, u] = vf_ref[u * TQA:(u + 1) * TQA, p * LANES:(p + 1) * LANES].T.astype(BF16)
    for u in range(ts // LANES):
        vst_ref[0, u] = vs_ref[u * LANES:(u + 1) * LANES, :].T.astype(BF16)
        vwt_ref[0, u] = vw_ref[u * LANES:(u + 1) * LANES, :].T.astype(BF16)


def _prep(proj, bias_row, pos3, rope_c, place, B, S):
    ts = min(512, S)
    nt = S // ts
    T = B * S
    row_blk = lambda w: pl.BlockSpec((ts, w), lambda b, t: (b * nt + t, 0))
    full = lambda a: pl.BlockSpec(a.shape, lambda b, t: (0, 0))
    wa, wb = place[0].shape[1], place[1].shape[1]
    return pl.pallas_call(
        functools.partial(_prep_kernel, ts=ts),
        grid=(B, nt),
        in_specs=[pl.BlockSpec((ts, LANES), lambda b, t: (b * nt + t, C_MISC // LANES)),
                  pl.BlockSpec((1, LANES), lambda b, t: (0, 0)),
                  pl.BlockSpec((1, ts, 1), lambda b, t: (b, t, 0)),
                  pl.BlockSpec((8, LANES), lambda b, t: (0, 0)),
                  pl.BlockSpec((ts, W_HEADS), lambda b, t: (b * nt + t, C_NQ // W_HEADS)),
                  pl.BlockSpec((ts, LANES), lambda b, t: (b * nt + t, C_NKV // LANES + 2)),
                  pl.BlockSpec((ts, LANES), lambda b, t: (b * nt + t, C_NKV // LANES + 4)),
                  full(place[0]), full(place[1]), full(place[2]), full(place[3]),
                  pl.BlockSpec((ts, W_HEADS), lambda b, t: (b * nt + t, C_FV // W_HEADS)),
                  pl.BlockSpec((ts, LANES), lambda b, t: (b * nt + t, C_NKV // LANES + 3)),
                  pl.BlockSpec((ts, LANES), lambda b, t: (b * nt + t, C_NKV // LANES + 5))],
        out_specs=[row_blk(LANES), row_blk(wa), row_blk(wb), row_blk(W_HEADS), row_blk(LANES), row_blk(LANES),
                   pl.BlockSpec((1, H_FOX // 2, ts // TQA, LANES, TQA), lambda b, t: (b, 0, t, 0, 0)),
                   pl.BlockSpec((1, ts // LANES, LANES, LANES), lambda b, t: (b, t, 0, 0)),
                   pl.BlockSpec((1, ts // LANES, LANES, LANES), lambda b, t: (b, t, 0, 0))],
        out_shape=[jax.ShapeDtypeStruct((T, LANES), F32),
                   jax.ShapeDtypeStruct((T, wa), BF16),
                   jax.ShapeDtypeStruct((T, wb), BF16),
                   jax.ShapeDtypeStruct((T, W_HEADS), F32),
                   jax.ShapeDtypeStruct((T, LANES), BF16),
                   jax.ShapeDtypeStruct((T, LANES), BF16),
                   jax.ShapeDtypeStruct((B, H_FOX // 2, S // TQA, LANES, TQA), BF16),
                   jax.ShapeDtypeStruct((B, S // LANES, LANES, LANES), BF16),
                   jax.ShapeDtypeStruct((B, S // LANES, LANES, LANES), BF16)],
        scratch_shapes=[pltpu.VMEM((1, LANES), F32)],
        compiler_params=_cparams(("parallel", "arbitrary")),
    )(proj, bias_row, pos3, rope_c, proj, proj, proj, *place, proj, proj, proj)


def _compress_kernel(kc_ref, vc_ref, pek_ref, w1k_ref, b1k_ref, w2k_ref,
                     pev_ref, w1v_ref, b1v_ref, w2v_ref, ok_ref, ov_ref, *, nc):
    for x_ref, pe_ref, w1_ref, b1_ref, w2_ref, o_ref in (
            (kc_ref, pek_ref, w1k_ref, b1k_ref, w2k_ref, ok_ref),
            (vc_ref, pev_ref, w1v_ref, b1v_ref, w2v_ref, ov_ref)):
        h_first = jnp.zeros((nc, NSA_KV_HEADS * CMP_HIDDEN), F32)
        h_second = jnp.zeros((nc, NSA_KV_HEADS * CMP_HIDDEN), F32)
        for l in range(CMP_STRIDE):
            x = x_ref[pl.ds(l, nc, stride=CMP_STRIDE), :]
            h_first = h_first + _dot((x + pe_ref[l:l + 1, :]).astype(BF16), w1_ref[l])
            h_second = h_second + _dot((x + pe_ref[CMP_STRIDE + l:CMP_STRIDE + l + 1, :]).astype(BF16),
                                       w1_ref[CMP_STRIDE + l])
        h = h_first + pltpu.roll(h_second, nc - 1, axis=0) + b1_ref[...]
        a = h * jax.nn.sigmoid(h)
        out = _dot(a.astype(BF16), w2_ref[...])
        o_ref[0] = (out.T if o_ref is ov_ref else out).astype(o_ref.dtype)


def _compress(proj, pek, w1k, b1k, w2k, pev, w1v, b1v, w2v, B, S):
    nc = S // CMP_STRIDE
    full = lambda a: pl.BlockSpec(a.shape, lambda b: (0,) * a.ndim)
    out_spec = pl.BlockSpec((1, nc, LANES), lambda b: (b, 0, 0))
    out_spec_t = pl.BlockSpec((1, LANES, nc), lambda b: (b, 0, 0))
    return pl.pallas_call(
        functools.partial(_compress_kernel, nc=nc),
        grid=(B,),
        in_specs=[pl.BlockSpec((S, LANES), lambda b: (b, C_NKV // LANES)),
                  pl.BlockSpec((S, LANES), lambda b: (b, C_NKV // LANES + 1)),
                  full(pek), full(w1k), full(b1k), full(w2k),
                  full(pev), full(w1v), full(b1v), full(w2v)],
        out_specs=[out_spec, out_spec_t],
        out_shape=[jax.ShapeDtypeStruct((B, nc, LANES), BF16), jax.ShapeDtypeStruct((B, LANES, nc), BF16)],
        compiler_params=_cparams(("parallel",)),
    )(proj, proj, pek, w1k, b1k, w2k, pev, w1v, b1v, w2v)


def _compress_weights(pe, w1, b1, w2):
    G = NSA_KV_HEADS
    pe2 = jnp.tile(pe, (1, G))
    w1l = w1.astype(BF16).reshape(CMP_LEN, HEAD_DIM, CMP_HIDDEN)
    w2b = w2.astype(BF16)
    z1 = jnp.zeros_like(w1l)
    z2 = jnp.zeros_like(w2b)
    w1_bd = jnp.concatenate([jnp.concatenate([w1l, z1], axis=2), jnp.concatenate([z1, w1l], axis=2)], axis=1)
    w2_bd = jnp.concatenate([jnp.concatenate([w2b, z2], axis=1), jnp.concatenate([z2, w2b], axis=1)], axis=0)
    return pe2, w1_bd, jnp.tile(b1.reshape(1, -1), (1, G)), w2_bd


def _nsa_kernel(qn_ref, qr_ref, kc_ref, vct_ref, ks_ref, vst_ref, kw_ref, vwt_ref, gt_ref, ot_ref,
                o_ref, acc_ref, m_ref, l_ref, s_ref, cmax_ref, *, S):
    i = pl.program_id(1)
    nc = S // CMP_STRIDE
    ns = S // SEL_BLOCK
    k_top = min(SEL_TOPK, ns)
    G = NSA_KV_HEADS
    R = H_NSA // G
    RT = R * TQ
    HD = HEAD_DIM
    low = lax.broadcasted_iota(jnp.int32, (TQ, LANES), 1) < HD
    gates_t = gt_ref[...].T
    qn = qn_ref[...] * SCALE
    qr = qr_ref[...]

    def tiled(mask, s, fill):
        return jnp.concatenate([jnp.where(mask, s[:, r * TQ:(r + 1) * TQ], fill) for r in range(R)], axis=1)

    def stack(qfull, g):
        in_g = low if g == 0 else jnp.logical_not(low)
        parts = []
        for r in range(R):
            a, b = divmod(R * g + r, 2)
            blk = qfull[:, a * LANES:(a + 1) * LANES]
            if b != g:
                blk = pltpu.roll(blk, HD, axis=1)
            parts.append(jnp.where(in_g, blk, 0.0))
        return jnp.concatenate(parts, axis=0).astype(BF16)

    def rows_of(g, x):
        return x[g * HD:(g + 1) * HD, :]


    qn_s = [stack(qn, g) for g in range(G)]
    c_valid = (lax.broadcasted_iota(jnp.int32, (nc, TQ), 0) * CMP_STRIDE + (CMP_LEN - 1)
               <= lax.broadcasted_iota(jnp.int32, (nc, TQ), 1) + i * TQ)
    scs = [tiled(c_valid, _dot_nt(kc_ref[0], qn_s[g]), NEG) for g in range(G)]
    es = [tiled(c_valid, jnp.exp(sc - jnp.max(sc, axis=0, keepdims=True)), 0.0) for sc in scs]
    dens = [jnp.sum(e, axis=0, keepdims=True) for e in es]
    p_cmps = [e / jnp.where(den > 0.0, den, 1.0) for e, den in zip(es, dens)]
    vct = vct_ref[0]
    o_cmps = [_dot(rows_of(g, vct), p_cmps[g].astype(BF16)) for g in range(G)]

    p_slcs = []
    for p in p_cmps:
        p_sum = p[:, 0:TQ]
        for r in range(1, R):
            p_sum = p_sum + p[:, r * TQ:(r + 1) * TQ]
        p_slcs.append(_dot(ot_ref[...], _split_bf16(p_sum, 0)))
    n_idx = lax.broadcasted_iota(jnp.int32, (ns, TQ), 0)
    t_s = lax.broadcasted_iota(jnp.int32, (ns, TQ), 1) + i * TQ
    forced = jnp.logical_or(n_idx == 0, n_idx == (t_s >> SEL_SHIFT))
    s_valid = n_idx * SEL_BLOCK <= t_s
    scores = [jnp.where(forced, 1e4, jnp.where(s_valid, p_slc, -1.0)) for p_slc in p_slcs]
    ranks = [jnp.zeros((ns, TQ), F32) for _ in range(G)]
    for m_i in range(ns):
        for g in range(G):
            sm = scores[g][m_i:m_i + 1, :]
            ge = jnp.where(sm >= scores[g], 1.0, 0.0)
            gt = jnp.where(sm > scores[g], 1.0, 0.0)
            ranks[g] = ranks[g] + jnp.where(n_idx > m_i, ge, gt)
    sels = []
    for g in range(G):
        sel = jnp.where(ranks[g] < k_top, 1.0, 0.0)
        if ns < LANES:
            sel = jnp.concatenate([sel, jnp.zeros((LANES - ns, TQ), F32)], axis=0)
        sels.append(sel.astype(BF16))

    qr_s = [stack(qr, g) for g in range(G)]
    m_ref[...] = jnp.full(m_ref.shape, NEG, F32)
    l_ref[...] = jnp.zeros_like(l_ref)
    acc_ref[...] = jnp.zeros_like(acc_ref)

    def stash(scored):
        for c, sc in scored:
            s_ref[c, 0:sc.shape[0], :] = sc
            cmax_ref[c] = jnp.max(sc, axis=0, keepdims=True)

    def absorb(items):
        stats = []
        for c, _, cmax, _ in items:
            m_old = m_ref[c]
            stats.append((m_old, jnp.maximum(m_old, cmax)))
        ps = [jnp.exp(sc - m_new) for (_, sc, _, _), (_, m_new) in zip(items, stats)]
        out = []
        for (c, _, _, vt), (m_old, m_new), p in zip(items, stats, ps):
            alpha = jnp.exp(m_old - m_new)
            m_ref[c] = m_new
            l_ref[c] = alpha * l_ref[c] + jnp.sum(p, axis=0, keepdims=True)
            out.append((c, alpha, _dot(vt, p.astype(BF16))))
        return out

    def accumulate(updates):
        for c, alpha, pv in updates:
            acc_ref[c] = alpha * acc_ref[c] + pv

    def chosen(g, off, width):
        blk = (lax.broadcasted_iota(jnp.int32, (width, LANES), 0) + off) >> SEL_SHIFT
        expand = jnp.where(blk == lax.broadcasted_iota(jnp.int32, (width, LANES), 1), 1.0, 0.0).astype(BF16)
        return _dot(expand, sels[g]) > 0.5

    n_win = jnp.minimum(i, (WINDOW + TQ - 2) // LANES) + 1
    n_w = n_win + ((i + 1 - n_win) & 1)
    n_old = (i + 1 - n_w) // 2

    def win_scores(j):
        off = pl.multiple_of((i - j) * LANES, LANES)
        s_pos = lax.broadcasted_iota(jnp.int32, (LANES, TQ), 0) + off
        t_q = lax.broadcasted_iota(jnp.int32, (LANES, TQ), 1) + i * TQ
        causal = s_pos <= t_q
        band = jnp.logical_and(causal, t_q - s_pos < WINDOW)
        ks_t, kw_t = ks_ref[pl.ds(off, LANES), :], kw_ref[pl.ds(off, LANES), :]
        scored = []
        for g in range(G):
            pick = jnp.logical_and(chosen(g, off, LANES), causal)
            scored.append((2 * g, tiled(pick, _dot_nt(ks_t, qr_s[g]), NEG)))
            scored.append((2 * g + 1, tiled(band, _dot_nt(kw_t, qr_s[g]), NEG)))
        return scored

    def win_items(j):
        vs_t, vw_t = vst_ref[0, i - j], vwt_ref[0, i - j]
        items = []
        for g in range(G):
            items.append((2 * g, s_ref[2 * g, 0:LANES, :], cmax_ref[2 * g], rows_of(g, vs_t)))
            items.append((2 * g + 1, s_ref[2 * g + 1, 0:LANES, :], cmax_ref[2 * g + 1], rows_of(g, vw_t)))
        return items

    def old_first_tile(j):
        return i - n_w - 2 * j - 1

    def old_scores(j):
        off = pl.multiple_of(old_first_tile(j) * LANES, LANES)
        ks_t = ks_ref[pl.ds(off, 2 * LANES), :]
        return [(2 * g, tiled(chosen(g, off, 2 * LANES), _dot_nt(ks_t, qr_s[g]), NEG)) for g in range(G)]

    def old_items(j):
        kb = old_first_tile(j)
        vs_t = jnp.concatenate([vst_ref[0, kb], vst_ref[0, kb + 1]], axis=1)
        return [(2 * g, s_ref[2 * g], cmax_ref[2 * g], rows_of(g, vs_t)) for g in range(G)]

    stash(win_scores(0))

    def win_body(j, carry):
        items = win_items(j - 1)
        scored = win_scores(j)
        updates = absorb(items)
        stash(scored)
        accumulate(updates)
        return carry

    lax.fori_loop(1, n_w, win_body, 0)
    accumulate(absorb(win_items(n_w - 1)))

    @pl.when(n_old > 0)
    def _():
        stash(old_scores(0))

        def old_body(j, carry):
            items = old_items(j - 1)
            scored = old_scores(j)
            updates = absorb(items)
            stash(scored)
            accumulate(updates)
            return carry

        lax.fori_loop(1, n_old, old_body, 0)
        accumulate(absorb(old_items(n_old - 1)))

    heads = [None] * H_NSA
    for g in range(G):
        l_sel, l_win = l_ref[2 * g], l_ref[2 * g + 1]
        o_sel = acc_ref[2 * g] / jnp.where(l_sel > 0.0, l_sel, 1.0)
        o_win = acc_ref[2 * g + 1] / jnp.where(l_win > 0.0, l_win, 1.0)
        for r in range(R):
            h = R * g + r
            cols = slice(r * TQ, (r + 1) * TQ)
            heads[h] = (gates_t[3 * h:3 * h + 1, :] * o_cmps[g][:, cols]
                        + gates_t[3 * h + 1:3 * h + 2, :] * o_sel[:, cols]
                        + gates_t[3 * h + 2:3 * h + 3, :] * o_win[:, cols])
    for a in range(H_NSA // 2):
        pair = jnp.concatenate([heads[2 * a], heads[2 * a + 1]], axis=0)
        o_ref[:, a * LANES:(a + 1) * LANES] = pair.T.astype(o_ref.dtype)


def _nsa_attention(proj, q_rope, k_cmp, v_cmp_t, ks_r, vs_t, kw_r, vw_t, gates, overlap_t, B, S):
    nq = S // TQ
    nc = S // CMP_STRIDE
    nk = S // LANES
    G = NSA_KV_HEADS
    rt = H_NSA // G * TQ
    k_full = pl.BlockSpec((S, LANES), lambda b, i: (b, 0))
    vt_full = pl.BlockSpec((1, nk, LANES, LANES), lambda b, i: (b, 0, 0, 0))
    return pl.pallas_call(
        functools.partial(_nsa_kernel, S=S),
        grid=(B, nq),
        in_specs=[pl.BlockSpec((TQ, W_HEADS), lambda b, i: (b * nq + i, C_NQ // W_HEADS)),
                  pl.BlockSpec((TQ, W_HEADS), lambda b, i: (b * nq + i, 0)),
                  pl.BlockSpec((1, nc, LANES), lambda b, i: (b, 0, 0)),
                  pl.BlockSpec((1, LANES, nc), lambda b, i: (b, 0, 0)),
                  k_full, vt_full, k_full, vt_full,
                  pl.BlockSpec((TQ, LANES), lambda b, i: (b * nq + i, 0)),
                  pl.BlockSpec(overlap_t.shape, lambda b, i: (0, 0))],
        out_specs=pl.BlockSpec((TQ, W_HEADS), lambda b, i: (b * nq + i, 0)),
        out_shape=jax.ShapeDtypeStruct((B * S, W_HEADS), BF16),
        scratch_shapes=[pltpu.VMEM((2 * G, HEAD_DIM, rt), F32), pltpu.VMEM((2 * G, 1, rt), F32),
                        pltpu.VMEM((2 * G, 1, rt), F32), pltpu.VMEM((2 * G, 2 * LANES, rt), F32),
                        pltpu.VMEM((2 * G, 1, rt), F32)],
        compiler_params=_cparams(("parallel", "arbitrary")),
    )(proj, q_rope, k_cmp, v_cmp_t, ks_r, vs_t, kw_r, vw_t, gates, overlap_t)


def _merge_kernel(x_ref, gpre_ref, wm_ref, osb_ref, onsa_ref, ofox_ref,
                  wsb_ref, wnsa_ref, wfox_ref, wout_ref, g_ref, o_ref):
    x = x_ref[...]
    D = x.shape[1]
    h = _rms(x, gpre_ref[...]).astype(BF16)
    y = None
    for c, (b_ref, w_ref) in enumerate(((osb_ref, wsb_ref), (onsa_ref, wnsa_ref), (ofox_ref, wfox_ref))):
        gate = jax.nn.sigmoid(_dot(h, wm_ref[:, c * D:(c + 1) * D]))
        term = gate * _dot(b_ref[...], w_ref[...])
        y = term if y is None else y + term
    z = _dot(y.astype(BF16), wout_ref[...])
    o_ref[...] = x + _rms(z, g_ref[...])


def _merge(x, g_pre, w_merge, o_sb, o_nsa, o_fox, w_sb, w_nsa, w_fox, w_out, g, tm):
    T, D = x.shape
    row = lambda w: pl.BlockSpec((tm, w), lambda i: (i, 0))
    full = lambda a: pl.BlockSpec(a.shape, lambda i: (0, 0))
    return pl.pallas_call(
        _merge_kernel,
        grid=(T // tm,),
        in_specs=[row(D), full(g_pre), full(w_merge), row(W_HEADS), row(W_HEADS), row(W_HEADS),
                  full(w_sb), full(w_nsa), full(w_fox), full(w_out), full(g)],
        out_specs=row(D),
        out_shape=jax.ShapeDtypeStruct((T, D), F32),
        compiler_params=_cparams(("parallel",)),
    )(x, g_pre, w_merge, o_sb, o_nsa, o_fox, w_sb, w_nsa, w_fox, w_out, g)


def _mem_kv_kernel(mem_ref, g_ref, wk_ref, wv_ref, k_ref, vt_ref):
    mn = _rms(mem_ref[0], g_ref[...]).astype(BF16)
    k_ref[0] = _dot(mn, wk_ref[...]).astype(BF16)
    vt_ref[0] = _dot(mn, wv_ref[...]).T.astype(BF16)


def _mem_kv(mem, g, wk, wv):
    B, M, D = mem.shape
    full = lambda a: pl.BlockSpec(a.shape, lambda b: (0, 0))
    return pl.pallas_call(
        _mem_kv_kernel,
        grid=(B,),
        in_specs=[pl.BlockSpec((1, M, D), lambda b: (b, 0, 0)), full(g), full(wk), full(wv)],
        out_specs=[pl.BlockSpec((1, M, W_MEM), lambda b: (b, 0, 0)),
                   pl.BlockSpec((1, W_MEM, M), lambda b: (b, 0, 0))],
        out_shape=[jax.ShapeDtypeStruct((B, M, W_MEM), BF16), jax.ShapeDtypeStruct((B, W_MEM, M), BF16)],
        compiler_params=_cparams(("parallel",)),
    )(mem, g, wk, wv)


def _mem_attn_kernel(x_ref, g_ref, wq_ref, k_ref, vt_ref, wo_ref, gp_ref, o_ref):
    x = x_ref[...]
    q = _dot(_rms(x, g_ref[...]).astype(BF16), wq_ref[...]) * SCALE
    k = k_ref[0]
    vt = vt_ref[0]
    lane = lax.broadcasted_iota(jnp.int32, q.shape, 1)
    heads = range(H_MEM)
    qh = [jnp.where(jnp.logical_and(lane >= h * HEAD_DIM, lane < (h + 1) * HEAD_DIM), q, 0.0).astype(BF16)
          for h in heads]
    ss = [_dot_nt(k, qh[h]) for h in heads]
    es = [jnp.exp(s - jnp.max(s, axis=0, keepdims=True)) for s in ss]
    ps = [(e / jnp.sum(e, axis=0, keepdims=True)).astype(BF16) for e in es]
    o_t = jnp.concatenate([_dot(vt[h * HEAD_DIM:(h + 1) * HEAD_DIM, :], ps[h]) for h in heads], axis=0)
    y = _dot(o_t.T.astype(BF16), wo_ref[...])
    o_ref[...] = x + _rms(y, gp_ref[...])


def _mem_attn(x, g_pre, wq, k, v_t, wo, g_post, B, S, tm):
    T, D = x.shape
    M = k.shape[1]
    nt = S // tm
    full = lambda a: pl.BlockSpec(a.shape, lambda b, i: (0, 0))
    row = pl.BlockSpec((tm, D), lambda b, i: (b * nt + i, 0))
    return pl.pallas_call(
        _mem_attn_kernel,
        grid=(B, nt),
        in_specs=[row, full(g_pre), full(wq), pl.BlockSpec((1, M, W_MEM), lambda b, i: (b, 0, 0)),
                  pl.BlockSpec((1, W_MEM, M), lambda b, i: (b, 0, 0)), full(wo), full(g_post)],
        out_specs=row,
        out_shape=jax.ShapeDtypeStruct((T, D), F32),
        compiler_params=_cparams(("parallel", "parallel")),
    )(x, g_pre, wq, k, v_t, wo, g_post)


def _ffn_kernel(x_ref, g_ref, wg_ref, wu_ref, wd_ref, gp_ref, o_ref, h_ref, acc_ref):
    j = pl.program_id(1)

    @pl.when(j == 0)
    def _():
        h_ref[...] = _rms(x_ref[...], g_ref[...]).astype(BF16)
        acc_ref[...] = jnp.zeros_like(acc_ref)

    h = h_ref[...]
    a = _dot(h, wg_ref[...])
    u = _dot(h, wu_ref[...])
    acc_ref[...] += _dot((a * jax.nn.sigmoid(a) * u).astype(BF16), wd_ref[...])

    @pl.when(j == pl.num_programs(1) - 1)
    def _():
        o_ref[...] = x_ref[...] + _rms(acc_ref[...], gp_ref[...])


def _ffn(x, g_pre, wg, wu, wd, g_post, tm, tf):
    T, D = x.shape
    F = wg.shape[1]
    row = pl.BlockSpec((tm, D), lambda i, j: (i, 0))
    vec = pl.BlockSpec((1, D), lambda i, j: (0, 0))
    return pl.pallas_call(
        _ffn_kernel,
        grid=(T // tm, F // tf),
        in_specs=[row, vec,
                  pl.BlockSpec((D, tf), lambda i, j: (0, j)),
                  pl.BlockSpec((D, tf), lambda i, j: (0, j)),
                  pl.BlockSpec((tf, D), lambda i, j: (j, 0)),
                  vec],
        out_specs=row,
        out_shape=jax.ShapeDtypeStruct((T, D), F32),
        scratch_shapes=[pltpu.VMEM((tm, D), BF16), pltpu.VMEM((tm, D), F32)],
        compiler_params=_cparams(("parallel", "arbitrary")),
    )(x, g_pre, wg, wu, wd, g_post)


def _scan_matrix():
    j = np.arange(LANES)
    later = (j[:, None] > j[None, :]).astype(np.float32)
    u = np.concatenate([later, np.ones((LANES, LANES), np.float32)], axis=1)
    return jnp.asarray(np.concatenate([u, u], axis=0), dtype=BF16)


def _overlap_t(S):
    nc, ns = S // CMP_STRIDE, S // SEL_BLOCK
    c0 = np.arange(nc) * CMP_STRIDE
    n0 = np.arange(ns) * SEL_BLOCK
    ov = (c0[None, :] < n0[:, None] + SEL_BLOCK) & (c0[None, :] + CMP_LEN > n0[:, None])
    ov = ov & (np.arange(nc)[None, :] < nc - 1)
    ov = ov.astype(np.float32)
    return jnp.asarray(np.concatenate([ov, ov], axis=1), dtype=BF16)


FOX_F_LANE = 24
N_PARTS = 3


def _fox_bias_placement():
    n_pair = H_FOX // 2
    pa = np.zeros((N_PARTS * LANES, n_pair * 2 * LANES), np.float32)
    pb = np.zeros((N_PARTS * LANES, n_pair * LANES), np.float32)
    oa = np.zeros((1, n_pair * 2 * LANES), np.float32)
    ob = np.zeros((1, n_pair * LANES), np.float32)
    for p in range(n_pair):
        for e in range(2):
            src = FOX_F_LANE + 2 * p + e
            for x in range(N_PARTS):
                pa[x * LANES + src, (2 * p + e) * LANES + 8 * e + x] = 1.0
                oa[0, (2 * p + e) * LANES + 8 * e + N_PARTS + x] = 1.0
                pb[x * LANES + src, p * LANES + 8 * e + N_PARTS + x] = -1.0
                ob[0, p * LANES + 8 * e + x] = 1.0
    return (jnp.asarray(pa, dtype=BF16), jnp.asarray(pb, dtype=BF16), jnp.asarray(oa), jnp.asarray(ob))


def _rope_rows():
    half = ROPE_DIM // 2
    inv_freq = ROPE_THETA ** (-jnp.arange(half, dtype=F32) / half)
    d = np.arange(LANES) % HEAD_DIM
    rot = d < ROPE_DIM
    freq = jnp.where(jnp.asarray(rot), inv_freq[jnp.asarray(d % half)], 0.0)
    s_up = jnp.asarray(((d >= half) & rot).astype(np.float32))
    s_dn = jnp.asarray(-(d < half).astype(np.float32))
    rows = jnp.stack([freq, s_up, s_dn] + [jnp.zeros((LANES,), F32)] * 5)
    return rows.astype(F32)


def _reorder_w_in(w):
    pad = jnp.zeros((w.shape[0], MISC_W - (_O_FOX_Q - _O_NSA_G) - (_O_MERGE - _O_FOX_F)), w.dtype)
    return jnp.concatenate([
        w[:, :_O_NSA_G - 6 * LANES],
        w[:, _O_FOX_Q:_O_FOX_F],
        w[:, _O_NSA_G - 6 * LANES:_O_NSA_G],
        w[:, _O_NSA_G:_O_FOX_Q],
        w[:, _O_FOX_F:_O_MERGE],
        pad], axis=1)


def _row_tile(n, target):
    t = min(n, target)
    while n % t:
        t //= 2
    return t


def kernel(x, mem, positions, g_pre_mix, g_post_mix, g_pre_mem, g_mem, g_post_mem, g_pre_ffn, g_post_ffn,
           w_in, b_fox_f, cmp_pe_k, cmp_w1_k, cmp_b1_k, cmp_w2_k, cmp_pe_v, cmp_w1_v, cmp_b1_v, cmp_w2_v,
           w_up_sb, w_up_nsa, w_up_fox, w_out, w_mem_q, w_mem_k, w_mem_v, w_mem_o,
           w_ffn_gate, w_ffn_up, w_ffn_down):
    B, S, D = x.shape
    T = B * S
    depth = w_in.shape[0]
    u2 = _scan_matrix()
    overlap_t = _overlap_t(S)
    rope_c = _rope_rows()
    place = _fox_bias_placement()
    pos3 = positions.reshape(B, S, 1)
    vec = lambda g: g.reshape(1, -1)
    tm_big = _row_tile(T, 1024)
    tm_mid = _row_tile(T, 512)
    d_ff = w_ffn_gate.shape[2]
    tf = d_ff // 2 if (d_ff // 2) % LANES == 0 else d_ff

    w_in_b = w_in.astype(BF16)
    xf = x.reshape(T, D)
    for l in range(depth):
        proj = _norm_matmul(xf, vec(g_pre_mix[l]), _reorder_w_in(w_in_b[l]), tm_big, N_IN // 2)

        o_sb = _sb_attention(proj, u2, B, S)

        bias_row = jnp.zeros((1, LANES), F32).at[0, FOX_F_LANE:FOX_F_LANE + H_FOX].set(b_fox_f[l])
        gates, fox_a, fox_b, q_rope, ks_r, kw_r, v_fox_t, vs_t, vw_t = _prep(
            proj, bias_row, pos3, rope_c, place, B, S)
        o_fox = _fox_attention(proj, v_fox_t, fox_a, fox_b, B, S)

        k_cmp, v_cmp_t = _compress(
            proj, *_compress_weights(cmp_pe_k[l], cmp_w1_k[l], cmp_b1_k[l], cmp_w2_k[l]),
            *_compress_weights(cmp_pe_v[l], cmp_w1_v[l], cmp_b1_v[l], cmp_w2_v[l]), B, S)
        o_nsa = _nsa_attention(proj, q_rope, k_cmp, v_cmp_t, ks_r, vs_t, kw_r, vw_t, gates, overlap_t, B, S)

        xf = _merge(xf, vec(g_pre_mix[l]), w_in_b[l][:, _O_MERGE:_O_END], o_sb, o_nsa, o_fox,
                    w_up_sb[l].astype(BF16), w_up_nsa[l].astype(BF16),
                    w_up_fox[l].astype(BF16), w_out[l].astype(BF16), vec(g_post_mix[l]), tm_mid)

        k_mem, v_mem = _mem_kv(mem, vec(g_mem[l]), w_mem_k[l].astype(BF16), w_mem_v[l].astype(BF16))
        xf = _mem_attn(xf, vec(g_pre_mem[l]), w_mem_q[l].astype(BF16), k_mem, v_mem,
                       w_mem_o[l].astype(BF16), vec(g_post_mem[l]), B, S, _row_tile(S, 512))

        xf = _ffn(xf, vec(g_pre_ffn[l]), w_ffn_gate[l].astype(BF16), w_ffn_up[l].astype(BF16),
                  w_ffn_down[l].astype(BF16), vec(g_post_ffn[l]), tm_mid, tf)
    return xf.reshape(B, S, D)
```

```python
import functools

import numpy as np
import jax
import jax.numpy as jnp
from jax import lax
from jax.experimental import pallas as pl
from jax.experimental.pallas import tpu as pltpu

F32 = jnp.float32
BF16 = jnp.bfloat16

D_MODEL = 1024
HEAD_DIM = 64
H_SB = 8
H_NSA = 8
NSA_KV_HEADS = 2
H_FOX = 8
H_MEM = 4
N_BRANCH = 3
ROPE_THETA = 500000.0
ROPE_DIM = HEAD_DIM // 4
CMP_STRIDE = 16
CMP_LEN = 2 * CMP_STRIDE
CMP_HIDDEN = 256
SEL_BLOCK = 64
SEL_SHIFT = SEL_BLOCK.bit_length() - 1
SEL_TOPK = 8
WINDOW = 512
W_HEADS = 8 * HEAD_DIM
W_MEM = H_MEM * HEAD_DIM
EPS = 1e-6
SCALE = HEAD_DIM ** -0.5
NEG = -1e30
LOG_ZERO = -104.0

LANES = 128
TQ = 128
TQA = 256
TQF = 512

C_SBQ, C_SBK, C_SBV = 0, 512, 1024
C_NQ = 1536
C_FQ, C_FK, C_FV = 2048, 2560, 3072
C_NKV = 3584
C_MISC = 4352
MISC_W = 256
N_IN = 4608
_O_NSA_G, _O_FOX_Q, _O_FOX_F, _O_MERGE, _O_END = 2816, 2840, 4376, 4384, 7456

VMEM_LIMIT = 56 * 1024 * 1024


def _cparams(sem):
    return pltpu.CompilerParams(dimension_semantics=sem, vmem_limit_bytes=VMEM_LIMIT)


def _dot(a, b):
    return jnp.dot(a, b, preferred_element_type=F32)


def _dot_nt(a, b):
    return lax.dot_general(a, b, (((1,), (1,)), ((), ())), preferred_element_type=F32)


def _rms(x, g):
    ms = jnp.mean(x * x, axis=-1, keepdims=True)
    return x * lax.rsqrt(ms + EPS) * g


def _split_bf16(x, axis=1):
    hi = x.astype(BF16)
    lo = (x - hi.astype(F32)).astype(BF16)
    return jnp.concatenate([hi, lo], axis=axis)


def _norm_matmul_kernel(x_ref, g_ref, w_ref, o_ref, *, tn):
    h = _rms(x_ref[...], g_ref[...]).astype(BF16)
    for c in range(w_ref.shape[1] // tn):
        o_ref[:, c * tn:(c + 1) * tn] = _dot(h, w_ref[:, c * tn:(c + 1) * tn])


def _norm_matmul(x, g, w, tm, tn):
    T, D = x.shape
    N = w.shape[1]
    return pl.pallas_call(
        functools.partial(_norm_matmul_kernel, tn=tn),
        grid=(T // tm,),
        in_specs=[pl.BlockSpec((tm, D), lambda i: (i, 0)),
                  pl.BlockSpec((1, D), lambda i: (0, 0)),
                  pl.BlockSpec((D, N), lambda i: (0, 0), pipeline_mode=pl.Buffered(1))],
        out_specs=pl.BlockSpec((tm, N), lambda i: (i, 0)),
        out_shape=jax.ShapeDtypeStruct((T, N), F32),
        compiler_params=_cparams(("parallel",)),
    )(x, g, w)


def _stack_pair(q, low):
    return jnp.concatenate([jnp.where(low, q, 0.0), jnp.where(low, 0.0, q)], axis=0).astype(BF16)


def _sb_kernel(q_ref, k_ref, v_ref, u_ref, o_ref, acc_ref, cs_ref, arg_ref, rs_ref):
    i = pl.program_id(2)
    rt = 2 * TQA
    n_grp = TQA // LANES
    lane = lax.broadcasted_iota(jnp.int32, (rt, LANES), 1)
    t_q = (lax.broadcasted_iota(jnp.int32, (rt, LANES), 0) & (TQA - 1)) + i * TQA
    low = lax.broadcasted_iota(jnp.int32, (TQA, LANES), 1) < HEAD_DIM
    qs = _stack_pair(q_ref[...] * SCALE, low)

    def offsets(kb_first):
        return [pl.multiple_of((kb_first - d) * LANES, LANES) for d in range(n_grp)]

    def logits(offs):
        return [_dot_nt(qs, k_ref[pl.ds(off, LANES), :].astype(BF16)) for off in offs]

    def scan(offs, zs, on_diagonal):
        log_keeps, log_betas, stricts = [], [], []
        for off, z in zip(offs, zs):
            log_keep = -(jnp.maximum(z, 0.0) + jnp.log(1.0 + jnp.exp(-jnp.abs(z))))
            log_betas.append(z + log_keep)
            if on_diagonal:
                stricts.append((lane + off) < t_q)
                log_keep = jnp.where(stricts[-1], log_keep, 0.0)
            log_keeps.append(log_keep)
        c2s = [_dot(_split_bf16(log_keep), u_ref[...]) for log_keep in log_keeps]
        args = [log_beta + c2[:, :LANES] for log_beta, c2 in zip(log_betas, c2s)]
        if on_diagonal:
            args = [jnp.where(strict, arg, NEG) for strict, arg in zip(stricts, args)]
        return args, [c2[:, LANES:] for c2 in c2s]

    def stash(args, row_sums):
        for d in range(n_grp):
            arg_ref[d] = args[d]
            rs_ref[d] = row_sums[d]

    def weights(offs, args, row_sums):
        cs = cs_ref[...]
        pvs = []
        for d, off in enumerate(offs):
            w = jnp.exp(args[d] + cs)
            pvs.append(_dot(w.astype(BF16), v_ref[pl.ds(off, LANES), :].astype(BF16)))
            cs = cs + row_sums[d]
        return cs, pvs

    def accumulate(cs, pvs):
        acc = acc_ref[...]
        for pv in pvs:
            acc = acc + pv
        acc_ref[...] = acc
        cs_ref[...] = cs

    acc_ref[...] = jnp.zeros_like(acc_ref)
    cs_ref[...] = jnp.zeros_like(cs_ref)
    def reaches_past(cs, row_sums):
        for rs in row_sums:
            cs = cs + rs
        return (jnp.max(cs) >= LOG_ZERO).astype(jnp.int32)

    first = offsets(n_grp * i + n_grp - 1)
    first_args, first_sums = scan(first, logits(first), True)
    stash(first_args, first_sums)

    def cond(carry):
        j, alive = carry
        return jnp.logical_and(j < i, alive == 1)

    def body(carry):
        j, _ = carry
        cur = offsets(n_grp * (i - j) + n_grp - 1)
        nxt = offsets(n_grp * (i - j) - 1)
        args = [arg_ref[d] for d in range(n_grp)]
        row_sums = [rs_ref[d] for d in range(n_grp)]
        zs = logits(nxt)
        cs, pvs = weights(cur, args, row_sums)
        nxt_args, nxt_sums = scan(nxt, zs, False)
        alive = reaches_past(cs, nxt_sums)
        accumulate(cs, pvs)
        stash(nxt_args, nxt_sums)
        return j + 1, alive

    j_last, _ = lax.while_loop(cond, body, (jnp.int32(0), reaches_past(cs_ref[...], first_sums)))
    last = offsets(n_grp * (i - j_last) + n_grp - 1)
    accumulate(*weights(last, [arg_ref[d] for d in range(n_grp)], [rs_ref[d] for d in range(n_grp)]))
    acc = acc_ref[...]
    o_ref[...] = jnp.where(low, acc[:TQA], acc[TQA:]).astype(o_ref.dtype)


def _sb_attention(proj, u2, B, S):
    nq = S // TQA
    cq, ck, cv = C_SBQ // LANES, C_SBK // LANES, C_SBV // LANES
    return pl.pallas_call(
        _sb_kernel,
        grid=(B, H_SB // 2, nq),
        in_specs=[pl.BlockSpec((TQA, LANES), lambda b, p, i: (b * nq + i, cq + p)),
                  pl.BlockSpec((S, LANES), lambda b, p, i: (b, ck + p)),
                  pl.BlockSpec((S, LANES), lambda b, p, i: (b, cv + p)),
                  pl.BlockSpec((2 * LANES, 2 * LANES), lambda b, p, i: (0, 0))],
        out_specs=pl.BlockSpec((TQA, LANES), lambda b, p, i: (b * nq + i, p)),
        out_shape=jax.ShapeDtypeStruct((B * S, W_HEADS), BF16),
        scratch_shapes=[pltpu.VMEM((2 * TQA, LANES), F32), pltpu.VMEM((2 * TQA, LANES), F32),
                        pltpu.VMEM((TQA // LANES, 2 * TQA, LANES), F32),
                        pltpu.VMEM((TQA // LANES, 2 * TQA, LANES), F32)],
        compiler_params=_cparams(("parallel", "parallel", "arbitrary")),
    )(proj, proj, proj, u2)


def _fox_kernel(q_ref, k_ref, vt_ref, fa_ref, fb_ref, o_ref, acc_ref, m_ref, l_ref, s_ref, cmax_ref):
    i = pl.program_id(2)
    rt = 2 * TQF
    n_diag = TQF // TQA
    low = lax.broadcasted_iota(jnp.int32, (TQF, LANES), 1) < HEAD_DIM
    qs = _stack_pair(q_ref[...] * SCALE, low)
    qa = jnp.concatenate([qs, jnp.concatenate([fa_ref[:, :LANES], fa_ref[:, LANES:]], axis=0)], axis=1)
    acc_ref[...] = jnp.zeros_like(acc_ref)
    l_ref[...] = jnp.zeros_like(l_ref)
    m_ref[...] = jnp.full(m_ref.shape, NEG, F32)

    def scores(kb, on_diagonal):
        off = pl.multiple_of(kb * TQA, TQA)
        kbias = jnp.concatenate([k_ref[pl.ds(off, TQA), :].astype(BF16), fb_ref[pl.ds(off, TQA), :]], axis=1)
        s = _dot_nt(kbias, qa)
        if on_diagonal:
            key_pos = lax.broadcasted_iota(jnp.int32, (TQA, rt), 0) + (kb - n_diag * i) * TQA
            q_pos = lax.broadcasted_iota(jnp.int32, (TQA, rt), 1) & (TQF - 1)
            s = jnp.where(key_pos <= q_pos, s, NEG)
        return s

    def stash(s):
        s_ref[...] = s
        cmax_ref[...] = jnp.max(s, axis=0, keepdims=True)

    def absorb(kb, s, cmax):
        m_old = m_ref[...]
        m_new = jnp.maximum(m_old, cmax)
        p = jnp.exp(s - m_new)
        alpha = jnp.exp(m_old - m_new)
        m_ref[...] = m_new
        l_ref[...] = alpha * l_ref[...] + jnp.sum(p, axis=0, keepdims=True)
        p = p.astype(BF16)
        vt = vt_ref[0, 0, kb]
        return alpha, [_dot(vt[e * HEAD_DIM:(e + 1) * HEAD_DIM, :], p[:, e * TQF:(e + 1) * TQF]) for e in range(2)]

    def accumulate(alpha, pvs):
        for e in range(2):
            acc_ref[e] = alpha[:, e * TQF:(e + 1) * TQF] * acc_ref[e] + pvs[e]

    d0 = n_diag * i
    stash(scores(d0, True))
    for d in range(1, n_diag):
        s_cur, cmax = s_ref[...], cmax_ref[...]
        s_next = scores(d0 + d, True)
        alpha, pvs = absorb(d0 + d - 1, s_cur, cmax)
        stash(s_next)
        accumulate(alpha, pvs)

    def body(j, carry):
        kb = jnp.where(j == 0, d0 + n_diag - 1, d0 - j)
        s_cur, cmax = s_ref[...], cmax_ref[...]
        s_next = scores(d0 - 1 - j, False)
        alpha, pvs = absorb(kb, s_cur, cmax)
        stash(s_next)
        accumulate(alpha, pvs)
        return carry

    lax.fori_loop(0, d0, body, 0)
    accumulate(*absorb(jnp.where(i == 0, n_diag - 1, 0), s_ref[...], cmax_ref[...]))
    l = l_ref[...]
    den = jnp.where(l > 0.0, l, 1.0)
    o_t = jnp.concatenate([acc_ref[0] / den[:, :TQF], acc_ref[1] / den[:, TQF:]], axis=0)
    o_ref[...] = o_t.T.astype(o_ref.dtype)


def _fox_attention(proj, v_t, fox_a, fox_b, B, S):
    nq = S // TQF
    cq, ck = C_FQ // LANES, C_FK // LANES
    return pl.pallas_call(
        _fox_kernel,
        grid=(B, H_FOX // 2, nq),
        in_specs=[pl.BlockSpec((TQF, LANES), lambda b, p, i: (b * nq + i, cq + p)),
                  pl.BlockSpec((S, LANES), lambda b, p, i: (b, ck + p)),
                  pl.BlockSpec((1, 1, S // TQA, LANES, TQA), lambda b, p, i: (b, p, 0, 0, 0)),
                  pl.BlockSpec((TQF, 2 * LANES), lambda b, p, i: (b * nq + i, p)),
                  pl.BlockSpec((S, LANES), lambda b, p, i: (b, p))],
        out_specs=pl.BlockSpec((TQF, LANES), lambda b, p, i: (b * nq + i, p)),
        out_shape=jax.ShapeDtypeStruct((B * S, W_HEADS), BF16),
        scratch_shapes=[pltpu.VMEM((2, HEAD_DIM, TQF), F32), pltpu.VMEM((1, 2 * TQF), F32),
                        pltpu.VMEM((1, 2 * TQF), F32), pltpu.VMEM((TQA, 2 * TQF), F32),
                        pltpu.VMEM((1, 2 * TQF), F32)],
        compiler_params=_cparams(("parallel", "parallel", "arbitrary")),
    )(proj, proj, v_t, fox_a, fox_b)


def _rope(x, cos, s_up, s_dn):
    return x * cos + pltpu.roll(x, ROPE_DIM // 2, axis=1) * s_up + pltpu.roll(x, LANES - ROPE_DIM // 2, axis=1) * s_dn


def _prep_kernel(misc_ref, bias_ref, pos_ref, rc_ref, q_ref, ks_ref, kw_ref, pa_ref, pb_ref, oa_ref, ob_ref,
                 vf_ref, vs_ref, vw_ref,
                 gates_ref, fa_ref, fb_ref, qr_ref, ksr_ref, kwr_ref, vft_ref, vst_ref, vwt_ref, carry_ref, *, ts):
    @pl.when(pl.program_id(1) == 0)
    def _():
        carry_ref[...] = jnp.zeros_like(carry_ref)

    logits = misc_ref[...]
    gates_ref[...] = jax.nn.sigmoid(logits)
    zf = logits + bias_ref[...]
    x = jnp.minimum(zf, 0.0) - jnp.log(1.0 + jnp.exp(-jnp.abs(zf)))
    row = lax.broadcasted_iota(jnp.int32, (ts, LANES), 0)
    sh = 1
    while sh < ts:
        x = x + jnp.where(row >= sh, pltpu.roll(x, sh, axis=0), 0.0)
        sh *= 2
    x = x + carry_ref[...]
    carry_ref[...] = x[ts - 1:ts, :]
    hi = x.astype(BF16)
    r1 = x - hi.astype(F32)
    mid = r1.astype(BF16)
    lo = (r1 - mid.astype(F32)).astype(BF16)
    parts = jnp.concatenate([hi, mid, lo], axis=1)
    fa_ref[...] = (_dot(parts, pa_ref[...]) + oa_ref[...]).astype(BF16)
    fb_ref[...] = (_dot(parts, pb_ref[...]) + ob_ref[...]).astype(BF16)

    ang = pos_ref[0].astype(F32) * rc_ref[0:1, :]
    cos = jnp.cos(ang)
    sin = jnp.sin(ang)
    s_up = sin * rc_ref[1:2, :]
    s_dn = sin * rc_ref[2:3, :]
    for a in range(W_HEADS // LANES):
        qa = q_ref[:, a * LANES:(a + 1) * LANES]
        qr_ref[:, a * LANES:(a + 1) * LANES] = _rope(qa, cos, s_up, s_dn) * SCALE
    ksr_ref[...] = _rope(ks_ref[...], cos, s_up, s_dn).astype(BF16)
    kwr_ref[...] = _rope(kw_ref[...], cos, s_up, s_dn).astype(BF16)

    for p in range(H_FOX // 2):
        for u in range(ts // TQA):
            vft_ref[0, p, u] = vf_ref[u * TQA:(u + 1) * TQA, p * LANES:(p + 1) * LANES].T.astype(BF16)
    for u in range(ts // LANES):
        vst_ref[0, u] = vs_ref[u * LANES:(u + 1) * LANES, :].T.astype(BF16)
        vwt_ref[0, u] = vw_ref[u * LANES:(u + 1) * LANES, :].T.astype(BF16)


def _prep(proj, bias_row, pos3, rope_c, place, B, S):
    ts = min(512, S)
    nt = S // ts
    T = B * S
    row_blk = lambda w: pl.BlockSpec((ts, w), lambda b, t: (b * nt + t, 0))
    full = lambda a: pl.BlockSpec(a.shape, lambda b, t: (0, 0))
    wa, wb = place[0].shape[1], place[1].shape[1]
    return pl.pallas_call(
        functools.partial(_prep_kernel, ts=ts),
        grid=(B, nt),
        in_specs=[pl.BlockSpec((ts, LANES), lambda b, t: (b * nt + t, C_MISC // LANES)),
                  pl.BlockSpec((1, LANES), lambda b, t: (0, 0)),
                  pl.BlockSpec((1, ts, 1), lambda b, t: (b, t, 0)),
                  pl.BlockSpec((8, LANES), lambda b, t: (0, 0)),
                  pl.BlockSpec((ts, W_HEADS), lambda b, t: (b * nt + t, C_NQ // W_HEADS)),
                  pl.BlockSpec((ts, LANES), lambda b, t: (b * nt + t, C_NKV // LANES + 2)),
                  pl.BlockSpec((ts, LANES), lambda b, t: (b * nt + t, C_NKV // LANES + 4)),
                  full(place[0]), full(place[1]), full(place[2]), full(place[3]),
                  pl.BlockSpec((ts, W_HEADS), lambda b, t: (b * nt + t, C_FV // W_HEADS)),
                  pl.BlockSpec((ts, LANES), lambda b, t: (b * nt + t, C_NKV // LANES + 3)),
                  pl.BlockSpec((ts, LANES), lambda b, t: (b * nt + t, C_NKV // LANES + 5))],
        out_specs=[row_blk(LANES), row_blk(wa), row_blk(wb), row_blk(W_HEADS), row_blk(LANES), row_blk(LANES),
                   pl.BlockSpec((1, H_FOX // 2, ts // TQA, LANES, TQA), lambda b, t: (b, 0, t, 0, 0)),
                   pl.BlockSpec((1, ts // LANES, LANES, LANES), lambda b, t: (b, t, 0, 0)),
                   pl.BlockSpec((1, ts // LANES, LANES, LANES), lambda b, t: (b, t, 0, 0))],
        out_shape=[jax.ShapeDtypeStruct((T, LANES), F32),
                   jax.ShapeDtypeStruct((T, wa), BF16),
                   jax.ShapeDtypeStruct((T, wb), BF16),
                   jax.ShapeDtypeStruct((T, W_HEADS), F32),
                   jax.ShapeDtypeStruct((T, LANES), BF16),
                   jax.ShapeDtypeStruct((T, LANES), BF16),
                   jax.ShapeDtypeStruct((B, H_FOX // 2, S // TQA, LANES, TQA), BF16),
                   jax.ShapeDtypeStruct((B, S // LANES, LANES, LANES), BF16),
                   jax.ShapeDtypeStruct((B, S // LANES, LANES, LANES), BF16)],
        scratch_shapes=[pltpu.VMEM((1, LANES), F32)],
        compiler_params=_cparams(("parallel", "arbitrary")),
    )(proj, bias_row, pos3, rope_c, proj, proj, proj, *place, proj, proj, proj)


def _compress_kernel(kc_ref, vc_ref, pek_ref, w1k_ref, b1k_ref, w2k_ref,
                     pev_ref, w1v_ref, b1v_ref, w2v_ref, ok_ref, ov_ref, *, nc):
    for x_ref, pe_ref, w1_ref, b1_ref, w2_ref, o_ref in (
            (kc_ref, pek_ref, w1k_ref, b1k_ref, w2k_ref, ok_ref),
            (vc_ref, pev_ref, w1v_ref, b1v_ref, w2v_ref, ov_ref)):
        h_first = jnp.zeros((nc, NSA_KV_HEADS * CMP_HIDDEN), F32)
        h_second = jnp.zeros((nc, NSA_KV_HEADS * CMP_HIDDEN), F32)
        for l in range(CMP_STRIDE):
            x = x_ref[pl.ds(l, nc, stride=CMP_STRIDE), :]
            h_first = h_first + _dot((x + pe_ref[l:l + 1, :]).astype(BF16), w1_ref[l])
            h_second = h_second + _dot((x + pe_ref[CMP_STRIDE + l:CMP_STRIDE + l + 1, :]).astype(BF16),
                                       w1_ref[CMP_STRIDE + l])
        h = h_first + pltpu.roll(h_second, nc - 1, axis=0) + b1_ref[...]
        a = h * jax.nn.sigmoid(h)
        out = _dot(a.astype(BF16), w2_ref[...])
        o_ref[0] = (out.T if o_ref is ov_ref else out).astype(o_ref.dtype)


def _compress(proj, pek, w1k, b1k, w2k, pev, w1v, b1v, w2v, B, S):
    nc = S // CMP_STRIDE
    full = lambda a: pl.BlockSpec(a.shape, lambda b: (0,) * a.ndim)
    out_spec = pl.BlockSpec((1, nc, LANES), lambda b: (b, 0, 0))
    out_spec_t = pl.BlockSpec((1, LANES, nc), lambda b: (b, 0, 0))
    return pl.pallas_call(
        functools.partial(_compress_kernel, nc=nc),
        grid=(B,),
        in_specs=[pl.BlockSpec((S, LANES), lambda b: (b, C_NKV // LANES)),
                  pl.BlockSpec((S, LANES), lambda b: (b, C_NKV // LANES + 1)),
                  full(pek), full(w1k), full(b1k), full(w2k),
                  full(pev), full(w1v), full(b1v), full(w2v)],
        out_specs=[out_spec, out_spec_t],
        out_shape=[jax.ShapeDtypeStruct((B, nc, LANES), BF16), jax.ShapeDtypeStruct((B, LANES, nc), BF16)],
        compiler_params=_cparams(("parallel",)),
    )(proj, proj, pek, w1k, b1k, w2k, pev, w1v, b1v, w2v)


def _compress_weights(pe, w1, b1, w2):
    G = NSA_KV_HEADS
    pe2 = jnp.tile(pe, (1, G))
    w1l = w1.astype(BF16).reshape(CMP_LEN, HEAD_DIM, CMP_HIDDEN)
    w2b = w2.astype(BF16)
    z1 = jnp.zeros_like(w1l)
    z2 = jnp.zeros_like(w2b)
    w1_bd = jnp.concatenate([jnp.concatenate([w1l, z1], axis=2), jnp.concatenate([z1, w1l], axis=2)], axis=1)
    w2_bd = jnp.concatenate([jnp.concatenate([w2b, z2], axis=1), jnp.concatenate([z2, w2b], axis=1)], axis=0)
    return pe2, w1_bd, jnp.tile(b1.reshape(1, -1), (1, G)), w2_bd


def _nsa_kernel(qn_ref, qr_ref, kc_ref, vct_ref, ks_ref, vst_ref, kw_ref, vwt_ref, gt_ref, ot_ref,
                o_ref, acc_ref, m_ref, l_ref, s_ref, cmax_ref, *, S):
    i = pl.program_id(1)
    nc = S // CMP_STRIDE
    ns = S // SEL_BLOCK
    k_top = min(SEL_TOPK, ns)
    G = NSA_KV_HEADS
    R = H_NSA // G
    RT = R * TQ
    HD = HEAD_DIM
    low = lax.broadcasted_iota(jnp.int32, (TQ, LANES), 1) < HD
    gates_t = gt_ref[...].T
    qn = qn_ref[...] * SCALE
    qr = qr_ref[...]

    def tiled(mask, s, fill):
        return jnp.concatenate([jnp.where(mask, s[:, r * TQ:(r + 1) * TQ], fill) for r in range(R)], axis=1)

    def stack(qfull, g):
        in_g = low if g == 0 else jnp.logical_not(low)
        parts = []
        for r in range(R):
            a, b = divmod(R * g + r, 2)
            blk = qfull[:, a * LANES:(a + 1) * LANES]
            if b != g:
                blk = pltpu.roll(blk, HD, axis=1)
            parts.append(jnp.where(in_g, blk, 0.0))
        return jnp.concatenate(parts, axis=0).astype(BF16)

    def rows_of(g, x):
        return x[g * HD:(g + 1) * HD, :]


    qn_s = [stack(qn, g) for g in range(G)]
    c_valid = (lax.broadcasted_iota(jnp.int32, (nc, TQ), 0) * CMP_STRIDE + (CMP_LEN - 1)
               <= lax.broadcasted_iota(jnp.int32, (nc, TQ), 1) + i * TQ)
    scs = [tiled(c_valid, _dot_nt(kc_ref[0], qn_s[g]), NEG) for g in range(G)]
    es = [tiled(c_valid, jnp.exp(sc - jnp.max(sc, axis=0, keepdims=True)), 0.0) for sc in scs]
    dens = [jnp.sum(e, axis=0, keepdims=True) for e in es]
    p_cmps = [e / jnp.where(den > 0.0, den, 1.0) for e, den in zip(es, dens)]
    vct = vct_ref[0]
    o_cmps = [_dot(rows_of(g, vct), p_cmps[g].astype(BF16)) for g in range(G)]

    p_slcs = []
    for p in p_cmps:
        p_sum = p[:, 0:TQ]
        for r in range(1, R):
            p_sum = p_sum + p[:, r * TQ:(r + 1) * TQ]
        p_slcs.append(_dot(ot_ref[...], _split_bf16(p_sum, 0)))
    n_idx = lax.broadcasted_iota(jnp.int32, (ns, TQ), 0)
    t_s = lax.broadcasted_iota(jnp.int32, (ns, TQ), 1) + i * TQ
    forced = jnp.logical_or(n_idx == 0, n_idx == (t_s >> SEL_SHIFT))
    s_valid = n_idx * SEL_BLOCK <= t_s
    scores = [jnp.where(forced, 1e4, jnp.where(s_valid, p_slc, -1.0)) for p_slc in p_slcs]
    ranks = [jnp.zeros((ns, TQ), F32) for _ in range(G)]
    for m_i in range(ns):
        for g in range(G):
            sm = scores[g][m_i:m_i + 1, :]
            ge = jnp.where(sm >= scores[g], 1.0, 0.0)
            gt = jnp.where(sm > scores[g], 1.0, 0.0)
            ranks[g] = ranks[g] + jnp.where(n_idx > m_i, ge, gt)
    sels = []
    for g in range(G):
        sel = jnp.where(ranks[g] < k_top, 1.0, 0.0)
        if ns < LANES:
            sel = jnp.concatenate([sel, jnp.zeros((LANES - ns, TQ), F32)], axis=0)
        sels.append(sel.astype(BF16))

    qr_s = [stack(qr, g) for g in range(G)]
    m_ref[...] = jnp.full(m_ref.shape, NEG, F32)
    l_ref[...] = jnp.zeros_like(l_ref)
    acc_ref[...] = jnp.zeros_like(acc_ref)

    def stash(scored):
        for c, sc in scored:
            s_ref[c, 0:sc.shape[0], :] = sc
            cmax_ref[c] = jnp.max(sc, axis=0, keepdims=True)

    def absorb(items):
        stats = []
        for c, _, cmax, _ in items:
            m_old = m_ref[c]
            stats.append((m_old, jnp.maximum(m_old, cmax)))
        ps = [jnp.exp(sc - m_new) for (_, sc, _, _), (_, m_new) in zip(items, stats)]
        out = []
        for (c, _, _, vt), (m_old, m_new), p in zip(items, stats, ps):
            alpha = jnp.exp(m_old - m_new)
            m_ref[c] = m_new
            l_ref[c] = alpha * l_ref[c] + jnp.sum(p, axis=0, keepdims=True)
            out.append((c, alpha, _dot(vt, p.astype(BF16))))
        return out

    def accumulate(updates):
        for c, alpha, pv in updates:
            acc_ref[c] = alpha * acc_ref[c] + pv

    def chosen(g, off, width):
        blk = (lax.broadcasted_iota(jnp.int32, (width, LANES), 0) + off) >> SEL_SHIFT
        expand = jnp.where(blk == lax.broadcasted_iota(jnp.int32, (width, LANES), 1), 1.0, 0.0).astype(BF16)
        return _dot(expand, sels[g]) > 0.5

    n_win = jnp.minimum(i, (WINDOW + TQ - 2) // LANES) + 1
    n_w = n_win + ((i + 1 - n_win) & 1)
    n_old = (i + 1 - n_w) // 2

    def win_scores(j):
        off = pl.multiple_of((i - j) * LANES, LANES)
        s_pos = lax.broadcasted_iota(jnp.int32, (LANES, TQ), 0) + off
        t_q = lax.broadcasted_iota(jnp.int32, (LANES, TQ), 1) + i * TQ
        causal = s_pos <= t_q
        band = jnp.logical_and(causal, t_q - s_pos < WINDOW)
        ks_t, kw_t = ks_ref[pl.ds(off, LANES), :], kw_ref[pl.ds(off, LANES), :]
        scored = []
        for g in range(G):
            pick = jnp.logical_and(chosen(g, off, LANES), causal)
            scored.append((2 * g, tiled(pick, _dot_nt(ks_t, qr_s[g]), NEG)))
            scored.append((2 * g + 1, tiled(band, _dot_nt(kw_t, qr_s[g]), NEG)))
        return scored

    def win_items(j):
        vs_t, vw_t = vst_ref[0, i - j], vwt_ref[0, i - j]
        items = []
        for g in range(G):
            items.append((2 * g, s_ref[2 * g, 0:LANES, :], cmax_ref[2 * g], rows_of(g, vs_t)))
            items.append((2 * g + 1, s_ref[2 * g + 1, 0:LANES, :], cmax_ref[2 * g + 1], rows_of(g, vw_t)))
        return items

    def old_first_tile(j):
        return i - n_w - 2 * j - 1

    def old_scores(j):
        off = pl.multiple_of(old_first_tile(j) * LANES, LANES)
        ks_t = ks_ref[pl.ds(off, 2 * LANES), :]
        return [(2 * g, tiled(chosen(g, off, 2 * LANES), _dot_nt(ks_t, qr_s[g]), NEG)) for g in range(G)]

    def old_items(j):
        kb = old_first_tile(j)
        vs_t = jnp.concatenate([vst_ref[0, kb], vst_ref[0, kb + 1]], axis=1)
        return [(2 * g, s_ref[2 * g], cmax_ref[2 * g], rows_of(g, vs_t)) for g in range(G)]

    stash(win_scores(0))

    def win_body(j, carry):
        items = win_items(j - 1)
        scored = win_scores(j)
        updates = absorb(items)
        stash(scored)
        accumulate(updates)
        return carry

    lax.fori_loop(1, n_w, win_body, 0)
    accumulate(absorb(win_items(n_w - 1)))

    @pl.when(n_old > 0)
    def _():
        stash(old_scores(0))

        def old_body(j, carry):
            items = old_items(j - 1)
            scored = old_scores(j)
            updates = absorb(items)
            stash(scored)
            accumulate(updates)
            return carry

        lax.fori_loop(1, n_old, old_body, 0)
        accumulate(absorb(old_items(n_old - 1)))

    heads = [None] * H_NSA
    for g in range(G):
        l_sel, l_win = l_ref[2 * g], l_ref[2 * g + 1]
        o_sel = acc_ref[2 * g] / jnp.where(l_sel > 0.0, l_sel, 1.0)
        o_win = acc_ref[2 * g + 1] / jnp.where(l_win > 0.0, l_win, 1.0)
        for r in range(R):
            h = R * g + r
            cols = slice(r * TQ, (r + 1) * TQ)
            heads[h] = (gates_t[3 * h:3 * h + 1, :] * o_cmps[g][:, cols]
                        + gates_t[3 * h + 1:3 * h + 2, :] * o_sel[:, cols]
                        + gates_t[3 * h + 2:3 * h + 3, :] * o_win[:, cols])
    for a in range(H_NSA // 2):
        pair = jnp.concatenate([heads[2 * a], heads[2 * a + 1]], axis=0)
        o_ref[:, a * LANES:(a + 1) * LANES] = pair.T.astype(o_ref.dtype)


def _nsa_attention(proj, q_rope, k_cmp, v_cmp_t, ks_r, vs_t, kw_r, vw_t, gates, overlap_t, B, S):
    nq = S // TQ
    nc = S // CMP_STRIDE
    nk = S // LANES
    G = NSA_KV_HEADS
    rt = H_NSA // G * TQ
    k_full = pl.BlockSpec((S, LANES), lambda b, i: (b, 0))
    vt_full = pl.BlockSpec((1, nk, LANES, LANES), lambda b, i: (b, 0, 0, 0))
    return pl.pallas_call(
        functools.partial(_nsa_kernel, S=S),
        grid=(B, nq),
        in_specs=[pl.BlockSpec((TQ, W_HEADS), lambda b, i: (b * nq + i, C_NQ // W_HEADS)),
                  pl.BlockSpec((TQ, W_HEADS), lambda b, i: (b * nq + i, 0)),
                  pl.BlockSpec((1, nc, LANES), lambda b, i: (b, 0, 0)),
                  pl.BlockSpec((1, LANES, nc), lambda b, i: (b, 0, 0)),
                  k_full, vt_full, k_full, vt_full,
                  pl.BlockSpec((TQ, LANES), lambda b, i: (b * nq + i, 0)),
                  pl.BlockSpec(overlap_t.shape, lambda b, i: (0, 0))],
        out_specs=pl.BlockSpec((TQ, W_HEADS), lambda b, i: (b * nq + i, 0)),
        out_shape=jax.ShapeDtypeStruct((B * S, W_HEADS), BF16),
        scratch_shapes=[pltpu.VMEM((2 * G, HEAD_DIM, rt), F32), pltpu.VMEM((2 * G, 1, rt), F32),
                        pltpu.VMEM((2 * G, 1, rt), F32), pltpu.VMEM((2 * G, 2 * LANES, rt), F32),
                        pltpu.VMEM((2 * G, 1, rt), F32)],
        compiler_params=_cparams(("parallel", "arbitrary")),
    )(proj, q_rope, k_cmp, v_cmp_t, ks_r, vs_t, kw_r, vw_t, gates, overlap_t)


def _merge_kernel(x_ref, gpre_ref, wm_ref, osb_ref, onsa_ref, ofox_ref,
                  wsb_ref, wnsa_ref, wfox_ref, wout_ref, g_ref, o_ref):
    x = x_ref[...]
    D = x.shape[1]
    h = _rms(x, gpre_ref[...]).astype(BF16)
    y = None
    for c, (b_ref, w_ref) in enumerate(((osb_ref, wsb_ref), (onsa_ref, wnsa_ref), (ofox_ref, wfox_ref))):
        gate = jax.nn.sigmoid(_dot(h, wm_ref[:, c * D:(c + 1) * D]))
        term = gate * _dot(b_ref[...], w_ref[...])
        y = term if y is None else y + term
    z = _dot(y.astype(BF16), wout_ref[...])
    o_ref[...] = x + _rms(z, g_ref[...])


def _merge(x, g_pre, w_merge, o_sb, o_nsa, o_fox, w_sb, w_nsa, w_fox, w_out, g, tm):
    T, D = x.shape
    row = lambda w: pl.BlockSpec((tm, w), lambda i: (i, 0))
    full = lambda a: pl.BlockSpec(a.shape, lambda i: (0, 0))
    return pl.pallas_call(
        _merge_kernel,
        grid=(T // tm,),
        in_specs=[row(D), full(g_pre), full(w_merge), row(W_HEADS), row(W_HEADS), row(W_HEADS),
                  full(w_sb), full(w_nsa), full(w_fox), full(w_out), full(g)],
        out_specs=row(D),
        out_shape=jax.ShapeDtypeStruct((T, D), F32),
        compiler_params=_cparams(("parallel",)),
    )(x, g_pre, w_merge, o_sb, o_nsa, o_fox, w_sb, w_nsa, w_fox, w_out, g)


def _mem_kv_kernel(mem_ref, g_ref, wk_ref, wv_ref, k_ref, vt_ref):
    mn = _rms(mem_ref[0], g_ref[...]).astype(BF16)
    k_ref[0] = _dot(mn, wk_ref[...]).astype(BF16)
    vt_ref[0] = _dot(mn, wv_ref[...]).T.astype(BF16)


def _mem_kv(mem, g, wk, wv):
    B, M, D = mem.shape
    full = lambda a: pl.BlockSpec(a.shape, lambda b: (0, 0))
    return pl.pallas_call(
        _mem_kv_kernel,
        grid=(B,),
        in_specs=[pl.BlockSpec((1, M, D), lambda b: (b, 0, 0)), full(g), full(wk), full(wv)],
        out_specs=[pl.BlockSpec((1, M, W_MEM), lambda b: (b, 0, 0)),
                   pl.BlockSpec((1, W_MEM, M), lambda b: (b, 0, 0))],
        out_shape=[jax.ShapeDtypeStruct((B, M, W_MEM), BF16), jax.ShapeDtypeStruct((B, W_MEM, M), BF16)],
        compiler_params=_cparams(("parallel",)),
    )(mem, g, wk, wv)


def _mem_attn_kernel(x_ref, g_ref, wq_ref, k_ref, vt_ref, wo_ref, gp_ref, o_ref):
    x = x_ref[...]
    q = _dot(_rms(x, g_ref[...]).astype(BF16), wq_ref[...]) * SCALE
    k = k_ref[0]
    vt = vt_ref[0]
    lane = lax.broadcasted_iota(jnp.int32, q.shape, 1)
    heads = range(H_MEM)
    qh = [jnp.where(jnp.logical_and(lane >= h * HEAD_DIM, lane < (h + 1) * HEAD_DIM), q, 0.0).astype(BF16)
          for h in heads]
    ss = [_dot_nt(k, qh[h]) for h in heads]
    es = [jnp.exp(s - jnp.max(s, axis=0, keepdims=True)) for s in ss]
    ps = [(e / jnp.sum(e, axis=0, keepdims=True)).astype(BF16) for e in es]
    o_t = jnp.concatenate([_dot(vt[h * HEAD_DIM:(h + 1) * HEAD_DIM, :], ps[h]) for h in heads], axis=0)
    y = _dot(o_t.T.astype(BF16), wo_ref[...])
    o_ref[...] = x + _rms(y, gp_ref[...])


def _mem_attn(x, g_pre, wq, k, v_t, wo, g_post, B, S, tm):
    T, D = x.shape
    M = k.shape[1]
    nt = S // tm
    full = lambda a: pl.BlockSpec(a.shape, lambda b, i: (0, 0))
    row = pl.BlockSpec((tm, D), lambda b, i: (b * nt + i, 0))
    return pl.pallas_call(
        _mem_attn_kernel,
        grid=(B, nt),
        in_specs=[row, full(g_pre), full(wq), pl.BlockSpec((1, M, W_MEM), lambda b, i: (b, 0, 0)),
                  pl.BlockSpec((1, W_MEM, M), lambda b, i: (b, 0, 0)), full(wo), full(g_post)],
        out_specs=row,
        out_shape=jax.ShapeDtypeStruct((T, D), F32),
        compiler_params=_cparams(("parallel", "parallel")),
    )(x, g_pre, wq, k, v_t, wo, g_post)


def _ffn_kernel(x_ref, g_ref, wg_ref, wu_ref, wd_ref, gp_ref, o_ref, *, tf):
    x = x_ref[...]
    h = _rms(x, g_ref[...]).astype(BF16)
    y = None
    for c in range(wg_ref.shape[1] // tf):
        cols = slice(c * tf, (c + 1) * tf)
        a = _dot(h, wg_ref[:, cols])
        u = _dot(h, wu_ref[:, cols])
        part = _dot((a * jax.nn.sigmoid(a) * u).astype(BF16), wd_ref[cols, :])
        y = part if y is None else y + part
    o_ref[...] = x + _rms(y, gp_ref[...])


def _ffn(x, g_pre, wg, wu, wd, g_post, tm, tf):
    T, D = x.shape
    row = pl.BlockSpec((tm, D), lambda i: (i, 0))
    vec = pl.BlockSpec((1, D), lambda i: (0, 0))
    resident = lambda a: pl.BlockSpec(a.shape, lambda i: (0, 0), pipeline_mode=pl.Buffered(1))
    return pl.pallas_call(
        functools.partial(_ffn_kernel, tf=tf),
        grid=(T // tm,),
        in_specs=[row, vec, resident(wg), resident(wu), resident(wd), vec],
        out_specs=row,
        out_shape=jax.ShapeDtypeStruct((T, D), F32),
        compiler_params=_cparams(("parallel",)),
    )(x, g_pre, wg, wu, wd, g_post)


def _scan_matrix():
    j = np.arange(LANES)
    later = (j[:, None] > j[None, :]).astype(np.float32)
    u = np.concatenate([later, np.ones((LANES, LANES), np.float32)], axis=1)
    return jnp.asarray(np.concatenate([u, u], axis=0), dtype=BF16)


def _overlap_t(S):
    nc, ns = S // CMP_STRIDE, S // SEL_BLOCK
    c0 = np.arange(nc) * CMP_STRIDE
    n0 = np.arange(ns) * SEL_BLOCK
    ov = (c0[None, :] < n0[:, None] + SEL_BLOCK) & (c0[None, :] + CMP_LEN > n0[:, None])
    ov = ov & (np.arange(nc)[None, :] < nc - 1)
    ov = ov.astype(np.float32)
    return jnp.asarray(np.concatenate([ov, ov], axis=1), dtype=BF16)


FOX_F_LANE = 24
N_PARTS = 3


def _fox_bias_placement():
    n_pair = H_FOX // 2
    pa = np.zeros((N_PARTS * LANES, n_pair * 2 * LANES), np.float32)
    pb = np.zeros((N_PARTS * LANES, n_pair * LANES), np.float32)
    oa = np.zeros((1, n_pair * 2 * LANES), np.float32)
    ob = np.zeros((1, n_pair * LANES), np.float32)
    for p in range(n_pair):
        for e in range(2):
            src = FOX_F_LANE + 2 * p + e
            for x in range(N_PARTS):
                pa[x * LANES + src, (2 * p + e) * LANES + 8 * e + x] = 1.0
                oa[0, (2 * p + e) * LANES + 8 * e + N_PARTS + x] = 1.0
                pb[x * LANES + src, p * LANES + 8 * e + N_PARTS + x] = -1.0
                ob[0, p * LANES + 8 * e + x] = 1.0
    return (jnp.asarray(pa, dtype=BF16), jnp.asarray(pb, dtype=BF16), jnp.asarray(oa), jnp.asarray(ob))


def _rope_rows():
    half = ROPE_DIM // 2
    inv_freq = ROPE_THETA ** (-jnp.arange(half, dtype=F32) / half)
    d = np.arange(LANES) % HEAD_DIM
    rot = d < ROPE_DIM
    freq = jnp.where(jnp.asarray(rot), inv_freq[jnp.asarray(d % half)], 0.0)
    s_up = jnp.asarray(((d >= half) & rot).astype(np.float32))
    s_dn = jnp.asarray(-(d < half).astype(np.float32))
    rows = jnp.stack([freq, s_up, s_dn] + [jnp.zeros((LANES,), F32)] * 5)
    return rows.astype(F32)


def _reorder_w_in(w):
    pad = jnp.zeros((w.shape[0], MISC_W - (_O_FOX_Q - _O_NSA_G) - (_O_MERGE - _O_FOX_F)), w.dtype)
    return jnp.concatenate([
        w[:, :_O_NSA_G - 6 * LANES],
        w[:, _O_FOX_Q:_O_FOX_F],
        w[:, _O_NSA_G - 6 * LANES:_O_NSA_G],
        w[:, _O_NSA_G:_O_FOX_Q],
        w[:, _O_FOX_F:_O_MERGE],
        pad], axis=1)


def _row_tile(n, target):
    t = min(n, target)
    while n % t:
        t //= 2
    return t


def kernel(x, mem, positions, g_pre_mix, g_post_mix, g_pre_mem, g_mem, g_post_mem, g_pre_ffn, g_post_ffn,
           w_in, b_fox_f, cmp_pe_k, cmp_w1_k, cmp_b1_k, cmp_w2_k, cmp_pe_v, cmp_w1_v, cmp_b1_v, cmp_w2_v,
           w_up_sb, w_up_nsa, w_up_fox, w_out, w_mem_q, w_mem_k, w_mem_v, w_mem_o,
           w_ffn_gate, w_ffn_up, w_ffn_down):
    B, S, D = x.shape
    T = B * S
    depth = w_in.shape[0]
    u2 = _scan_matrix()
    overlap_t = _overlap_t(S)
    rope_c = _rope_rows()
    place = _fox_bias_placement()
    pos3 = positions.reshape(B, S, 1)
    vec = lambda g: g.reshape(1, -1)
    tm_mid = _row_tile(T, 512)
    d_ff = w_ffn_gate.shape[2]
    tf = 2 * LANES if d_ff % (2 * LANES) == 0 else d_ff

    w_in_b = w_in.astype(BF16)
    xf = x.reshape(T, D)
    for l in range(depth):
        proj = _norm_matmul(xf, vec(g_pre_mix[l]), _reorder_w_in(w_in_b[l]), tm_mid, 4 * LANES)

        o_sb = _sb_attention(proj, u2, B, S)

        bias_row = jnp.zeros((1, LANES), F32).at[0, FOX_F_LANE:FOX_F_LANE + H_FOX].set(b_fox_f[l])
        gates, fox_a, fox_b, q_rope, ks_r, kw_r, v_fox_t, vs_t, vw_t = _prep(
            proj, bias_row, pos3, rope_c, place, B, S)
        o_fox = _fox_attention(proj, v_fox_t, fox_a, fox_b, B, S)

        k_cmp, v_cmp_t = _compress(
            proj, *_compress_weights(cmp_pe_k[l], cmp_w1_k[l], cmp_b1_k[l], cmp_w2_k[l]),
            *_compress_weights(cmp_pe_v[l], cmp_w1_v[l], cmp_b1_v[l], cmp_w2_v[l]), B, S)
        o_nsa = _nsa_attention(proj, q_rope, k_cmp, v_cmp_t, ks_r, vs_t, kw_r, vw_t, gates, overlap_t, B, S)

        xf = _merge(xf, vec(g_pre_mix[l]), w_in_b[l][:, _O_MERGE:_O_END], o_sb, o_nsa, o_fox,
                    w_up_sb[l].astype(BF16), w_up_nsa[l].astype(BF16),
                    w_up_fox[l].astype(BF16), w_out[l].astype(BF16), vec(g_post_mix[l]), tm_mid)

        k_mem, v_mem = _mem_kv(mem, vec(g_mem[l]), w_mem_k[l].astype(BF16), w_mem_v[l].astype(BF16))
        xf = _mem_attn(xf, vec(g_pre_mem[l]), w_mem_q[l].astype(BF16), k_mem, v_mem,
                       w_mem_o[l].astype(BF16), vec(g_post_mem[l]), B, S, _row_tile(S, 512))

        xf = _ffn(xf, vec(g_pre_ffn[l]), w_ffn_gate[l].astype(BF16), w_ffn_up[l].astype(BF16),
                  w_ffn_down[l].astype(BF16), vec(g_post_ffn[l]), tm_mid, tf)
    return xf.reshape(B, S, D)
```

```python
import functools

import numpy as np
import jax
import jax.numpy as jnp
from jax import lax
from jax.experimental import pallas as pl
from jax.experimental.pallas import tpu as pltpu

F32 = jnp.float32
BF16 = jnp.bfloat16

D_MODEL = 1024
HEAD_DIM = 64
H_SB = 8
H_NSA = 8
NSA_KV_HEADS = 2
H_FOX = 8
H_MEM = 4
N_BRANCH = 3
ROPE_THETA = 500000.0
ROPE_DIM = HEAD_DIM // 4
CMP_STRIDE = 16
CMP_LEN = 2 * CMP_STRIDE
CMP_HIDDEN = 256
SEL_BLOCK = 64
SEL_SHIFT = SEL_BLOCK.bit_length() - 1
SEL_TOPK = 8
WINDOW = 512
W_HEADS = 8 * HEAD_DIM
W_MEM = H_MEM * HEAD_DIM
EPS = 1e-6
SCALE = HEAD_DIM ** -0.5
NEG = -1e30
LOG_ZERO = -104.0
LOG2E = 1.4426950408889634

LANES = 128
TQ = 128
TQA = 256
TQF = 512

C_SBQ, C_SBK, C_SBV = 0, 512, 1024
C_NQ = 1536
C_FQ, C_FK, C_FV = 2048, 2560, 3072
C_NKV = 3584
C_MISC = 4352
MISC_W = 256
N_IN = 4608
_O_NSA_G, _O_FOX_Q, _O_FOX_F, _O_MERGE, _O_END = 2816, 2840, 4376, 4384, 7456

VMEM_LIMIT = 56 * 1024 * 1024


def _cparams(sem):
    return pltpu.CompilerParams(dimension_semantics=sem, vmem_limit_bytes=VMEM_LIMIT)


def _dot(a, b):
    return jnp.dot(a, b, preferred_element_type=F32)


def _dot_nt(a, b):
    return lax.dot_general(a, b, (((1,), (1,)), ((), ())), preferred_element_type=F32)


def _rms(x, g):
    ms = jnp.mean(x * x, axis=-1, keepdims=True)
    return x * lax.rsqrt(ms + EPS) * g


def _split_bf16(x, axis=1):
    hi = x.astype(BF16)
    lo = (x - hi.astype(F32)).astype(BF16)
    return jnp.concatenate([hi, lo], axis=axis)


def _norm_matmul_kernel(x_ref, g_ref, w_ref, o_ref, *, tn):
    h = _rms(x_ref[...], g_ref[...]).astype(BF16)
    for c in range(w_ref.shape[1] // tn):
        o_ref[:, c * tn:(c + 1) * tn] = _dot(h, w_ref[:, c * tn:(c + 1) * tn])


def _norm_matmul(x, g, w, tm, tn):
    T, D = x.shape
    N = w.shape[1]
    return pl.pallas_call(
        functools.partial(_norm_matmul_kernel, tn=tn),
        grid=(T // tm,),
        in_specs=[pl.BlockSpec((tm, D), lambda i: (i, 0)),
                  pl.BlockSpec((1, D), lambda i: (0, 0)),
                  pl.BlockSpec((D, N), lambda i: (0, 0), pipeline_mode=pl.Buffered(1))],
        out_specs=pl.BlockSpec((tm, N), lambda i: (i, 0)),
        out_shape=jax.ShapeDtypeStruct((T, N), F32),
        compiler_params=_cparams(("parallel",)),
    )(x, g, w)


def _stack_pair(q, low):
    return jnp.concatenate([jnp.where(low, q, 0.0), jnp.where(low, 0.0, q)], axis=0).astype(BF16)


def _sb_kernel(q_ref, k_ref, v_ref, u_ref, o_ref, acc_ref, cs_ref, arg_ref, rs_ref):
    i = pl.program_id(2)
    rt = 2 * TQA
    n_grp = TQA // LANES
    lane = lax.broadcasted_iota(jnp.int32, (rt, LANES), 1)
    t_q = (lax.broadcasted_iota(jnp.int32, (rt, LANES), 0) & (TQA - 1)) + i * TQA
    low = lax.broadcasted_iota(jnp.int32, (TQA, LANES), 1) < HEAD_DIM
    qs = _stack_pair(q_ref[...] * SCALE, low)

    def offsets(kb_first):
        return [pl.multiple_of((kb_first - d) * LANES, LANES) for d in range(n_grp)]

    def logits(offs):
        return [_dot_nt(qs, k_ref[pl.ds(off, LANES), :].astype(BF16)) for off in offs]

    def scan(offs, zs, on_diagonal):
        drops, log_betas, stricts = [], [], []
        for off, z in zip(offs, zs):
            drop = jnp.maximum(z, 0.0) + jnp.log(1.0 + jnp.exp2(jnp.abs(z) * -LOG2E))
            log_betas.append(z - drop)
            if on_diagonal:
                stricts.append((lane + off) < t_q)
                drop = jnp.where(stricts[-1], drop, 0.0)
            drops.append(drop)
        c2s = [_dot(_split_bf16(drop), u_ref[...]) for drop in drops]
        args = [log_beta - c2[:, :LANES] for log_beta, c2 in zip(log_betas, c2s)]
        if on_diagonal:
            args = [jnp.where(strict, arg, NEG) for strict, arg in zip(stricts, args)]
        return args, [c2[:, LANES:] for c2 in c2s]

    def stash(args, row_sums):
        for d in range(n_grp):
            arg_ref[d] = args[d]
            rs_ref[d] = row_sums[d]

    def weights(offs, args, row_sums):
        cs = cs_ref[...]
        pvs = []
        for d, off in enumerate(offs):
            w = jnp.exp(args[d] - cs)
            pvs.append(_dot(w.astype(BF16), v_ref[pl.ds(off, LANES), :].astype(BF16)))
            cs = cs + row_sums[d]
        return cs, pvs

    def accumulate(cs, pvs):
        acc = acc_ref[...]
        for pv in pvs:
            acc = acc + pv
        acc_ref[...] = acc
        cs_ref[...] = cs

    acc_ref[...] = jnp.zeros_like(acc_ref)
    cs_ref[...] = jnp.zeros_like(cs_ref)
    def reaches_past(cs, row_sums):
        for rs in row_sums:
            cs = cs + rs
        return (jnp.min(cs) <= -LOG_ZERO).astype(jnp.int32)

    first = offsets(n_grp * i + n_grp - 1)
    first_args, first_sums = scan(first, logits(first), True)
    stash(first_args, first_sums)

    def cond(carry):
        j, alive = carry
        return jnp.logical_and(j < i, alive == 1)

    def body(carry):
        j, _ = carry
        cur = offsets(n_grp * (i - j) + n_grp - 1)
        nxt = offsets(n_grp * (i - j) - 1)
        args = [arg_ref[d] for d in range(n_grp)]
        row_sums = [rs_ref[d] for d in range(n_grp)]
        zs = logits(nxt)
        cs, pvs = weights(cur, args, row_sums)
        nxt_args, nxt_sums = scan(nxt, zs, False)
        alive = reaches_past(cs, nxt_sums)
        accumulate(cs, pvs)
        stash(nxt_args, nxt_sums)
        return j + 1, alive

    j_last, _ = lax.while_loop(cond, body, (jnp.int32(0), reaches_past(cs_ref[...], first_sums)))
    last = offsets(n_grp * (i - j_last) + n_grp - 1)
    accumulate(*weights(last, [arg_ref[d] for d in range(n_grp)], [rs_ref[d] for d in range(n_grp)]))
    acc = acc_ref[...]
    o_ref[...] = jnp.where(low, acc[:TQA], acc[TQA:]).astype(o_ref.dtype)


def _sb_attention(proj, u2, B, S):
    nq = S // TQA
    cq, ck, cv = C_SBQ // LANES, C_SBK // LANES, C_SBV // LANES
    return pl.pallas_call(
        _sb_kernel,
        grid=(B, H_SB // 2, nq),
        in_specs=[pl.BlockSpec((TQA, LANES), lambda b, p, i: (b * nq + i, cq + p)),
                  pl.BlockSpec((S, LANES), lambda b, p, i: (b, ck + p)),
                  pl.BlockSpec((S, LANES), lambda b, p, i: (b, cv + p)),
                  pl.BlockSpec((2 * LANES, 2 * LANES), lambda b, p, i: (0, 0))],
        out_specs=pl.BlockSpec((TQA, LANES), lambda b, p, i: (b * nq + i, p)),
        out_shape=jax.ShapeDtypeStruct((B * S, W_HEADS), BF16),
        scratch_shapes=[pltpu.VMEM((2 * TQA, LANES), F32), pltpu.VMEM((2 * TQA, LANES), F32),
                        pltpu.VMEM((TQA // LANES, 2 * TQA, LANES), F32),
                        pltpu.VMEM((TQA // LANES, 2 * TQA, LANES), F32)],
        compiler_params=_cparams(("parallel", "parallel", "arbitrary")),
    )(proj, proj, proj, u2)


def _fox_kernel(q_ref, k_ref, vt_ref, fa_ref, fb_ref, o_ref, acc_ref, m_ref, l_ref, s_ref, cmax_ref):
    i = pl.program_id(2)
    rt = 2 * TQF
    n_diag = TQF // TQA
    low = lax.broadcasted_iota(jnp.int32, (TQF, LANES), 1) < HEAD_DIM
    qs = _stack_pair(q_ref[...] * SCALE, low)
    qa = jnp.concatenate([qs, jnp.concatenate([fa_ref[:, :LANES], fa_ref[:, LANES:]], axis=0)], axis=1)
    acc_ref[...] = jnp.zeros_like(acc_ref)
    l_ref[...] = jnp.zeros_like(l_ref)
    m_ref[...] = jnp.full(m_ref.shape, NEG, F32)

    def scores(kb, on_diagonal):
        off = pl.multiple_of(kb * TQA, TQA)
        kbias = jnp.concatenate([k_ref[pl.ds(off, TQA), :].astype(BF16), fb_ref[pl.ds(off, TQA), :]], axis=1)
        s = _dot_nt(kbias, qa)
        if on_diagonal:
            key_pos = lax.broadcasted_iota(jnp.int32, (TQA, rt), 0) + (kb - n_diag * i) * TQA
            q_pos = lax.broadcasted_iota(jnp.int32, (TQA, rt), 1) & (TQF - 1)
            s = jnp.where(key_pos <= q_pos, s, NEG)
        return s

    def stash(s):
        s_ref[...] = s
        cmax_ref[...] = jnp.max(s, axis=0, keepdims=True)

    def absorb(kb, s, cmax):
        m_old = m_ref[...]
        m_new = jnp.maximum(m_old, cmax)
        p = jnp.exp(s - m_new)
        alpha = jnp.exp(m_old - m_new)
        m_ref[...] = m_new
        l_ref[...] = alpha * l_ref[...] + jnp.sum(p, axis=0, keepdims=True)
        p = p.astype(BF16)
        vt = vt_ref[0, 0, kb]
        return alpha, [_dot(vt[e * HEAD_DIM:(e + 1) * HEAD_DIM, :], p[:, e * TQF:(e + 1) * TQF]) for e in range(2)]

    def accumulate(alpha, pvs):
        for e in range(2):
            acc_ref[e] = alpha[:, e * TQF:(e + 1) * TQF] * acc_ref[e] + pvs[e]

    d0 = n_diag * i
    stash(scores(d0, True))
    for d in range(1, n_diag):
        s_cur, cmax = s_ref[...], cmax_ref[...]
        s_next = scores(d0 + d, True)
        alpha, pvs = absorb(d0 + d - 1, s_cur, cmax)
        stash(s_next)
        accumulate(alpha, pvs)

    def body(j, carry):
        kb = jnp.where(j == 0, d0 + n_diag - 1, d0 - j)
        s_cur, cmax = s_ref[...], cmax_ref[...]
        s_next = scores(d0 - 1 - j, False)
        alpha, pvs = absorb(kb, s_cur, cmax)
        stash(s_next)
        accumulate(alpha, pvs)
        return carry

    lax.fori_loop(0, d0, body, 0)
    accumulate(*absorb(jnp.where(i == 0, n_diag - 1, 0), s_ref[...], cmax_ref[...]))
    l = l_ref[...]
    den = jnp.where(l > 0.0, l, 1.0)
    o_t = jnp.concatenate([acc_ref[0] / den[:, :TQF], acc_ref[1] / den[:, TQF:]], axis=0)
    o_ref[...] = o_t.T.astype(o_ref.dtype)


def _fox_attention(proj, v_t, fox_a, fox_b, B, S):
    nq = S // TQF
    cq, ck = C_FQ // LANES, C_FK // LANES
    return pl.pallas_call(
        _fox_kernel,
        grid=(B, H_FOX // 2, nq),
        in_specs=[pl.BlockSpec((TQF, LANES), lambda b, p, i: (b * nq + i, cq + p)),
                  pl.BlockSpec((S, LANES), lambda b, p, i: (b, ck + p)),
                  pl.BlockSpec((1, 1, S // TQA, LANES, TQA), lambda b, p, i: (b, p, 0, 0, 0)),
                  pl.BlockSpec((TQF, 2 * LANES), lambda b, p, i: (b * nq + i, p)),
                  pl.BlockSpec((S, LANES), lambda b, p, i: (b, p))],
        out_specs=pl.BlockSpec((TQF, LANES), lambda b, p, i: (b * nq + i, p)),
        out_shape=jax.ShapeDtypeStruct((B * S, W_HEADS), BF16),
        scratch_shapes=[pltpu.VMEM((2, HEAD_DIM, TQF), F32), pltpu.VMEM((1, 2 * TQF), F32),
                        pltpu.VMEM((1, 2 * TQF), F32), pltpu.VMEM((TQA, 2 * TQF), F32),
                        pltpu.VMEM((1, 2 * TQF), F32)],
        compiler_params=_cparams(("parallel", "parallel", "arbitrary")),
    )(proj, proj, v_t, fox_a, fox_b)


def _rope(x, cos, s_up, s_dn):
    return x * cos + pltpu.roll(x, ROPE_DIM // 2, axis=1) * s_up + pltpu.roll(x, LANES - ROPE_DIM // 2, axis=1) * s_dn


def _prep_kernel(misc_ref, bias_ref, pos_ref, rc_ref, rf_ref, re_ref, q_ref, ks_ref, kw_ref, pa_ref, pb_ref, oa_ref, ob_ref,
                 vf_ref, vs_ref, vw_ref,
                 gates_ref, fa_ref, fb_ref, qr_ref, ksr_ref, kwr_ref, vft_ref, vst_ref, vwt_ref, carry_ref, *, ts):
    @pl.when(pl.program_id(1) == 0)
    def _():
        carry_ref[...] = jnp.zeros_like(carry_ref)

    logits = misc_ref[...]
    gates_ref[...] = jax.nn.sigmoid(logits)
    zf = logits + bias_ref[...]
    x = jnp.minimum(zf, 0.0) - jnp.log(1.0 + jnp.exp(-jnp.abs(zf)))
    row = lax.broadcasted_iota(jnp.int32, (ts, LANES), 0)
    sh = 1
    while sh < ts:
        x = x + jnp.where(row >= sh, pltpu.roll(x, sh, axis=0), 0.0)
        sh *= 2
    x = x + carry_ref[...]
    carry_ref[...] = x[ts - 1:ts, :]
    hi = x.astype(BF16)
    r1 = x - hi.astype(F32)
    mid = r1.astype(BF16)
    lo = (r1 - mid.astype(F32)).astype(BF16)
    parts = jnp.concatenate([hi, mid, lo], axis=1)
    fa_ref[...] = (_dot(parts, pa_ref[...]) + oa_ref[...]).astype(BF16)
    fb_ref[...] = (_dot(parts, pb_ref[...]) + ob_ref[...]).astype(BF16)

    ang = rf_ref[...] * pos_ref[0].astype(F32)
    parts = []
    for t in (jnp.cos(ang), jnp.sin(ang)):
        hi = t.astype(BF16)
        r1 = t - hi.astype(F32)
        mid = r1.astype(BF16)
        parts += [hi, mid, (r1 - mid.astype(F32)).astype(BF16)]
    spread = lax.dot_general(jnp.concatenate(parts, axis=0), re_ref[...], (((0,), (0,)), ((), ())),
                             preferred_element_type=F32)
    cos = spread[:, :LANES] + rc_ref[0:1, :]
    s_up = spread[:, LANES:2 * LANES]
    s_dn = spread[:, 2 * LANES:]
    for a in range(W_HEADS // LANES):
        qa = q_ref[:, a * LANES:(a + 1) * LANES]
        qr_ref[:, a * LANES:(a + 1) * LANES] = _rope(qa, cos, s_up, s_dn) * SCALE
    ksr_ref[...] = _rope(ks_ref[...], cos, s_up, s_dn).astype(BF16)
    kwr_ref[...] = _rope(kw_ref[...], cos, s_up, s_dn).astype(BF16)

    for p in range(H_FOX // 2):
        for u in range(ts // TQA):
            vft_ref[0, p, u] = vf_ref[u * TQA:(u + 1) * TQA, p * LANES:(p + 1) * LANES].T.astype(BF16)
    for u in range(ts // LANES):
        vst_ref[0, u] = vs_ref[u * LANES:(u + 1) * LANES, :].T.astype(BF16)
        vwt_ref[0, u] = vw_ref[u * LANES:(u + 1) * LANES, :].T.astype(BF16)


def _prep(proj, bias_row, pos3, rope_c, rope_f, rope_e, place, B, S):
    ts = min(512, S)
    nt = S // ts
    T = B * S
    row_blk = lambda w: pl.BlockSpec((ts, w), lambda b, t: (b * nt + t, 0))
    full = lambda a: pl.BlockSpec(a.shape, lambda b, t: (0, 0))
    wa, wb = place[0].shape[1], place[1].shape[1]
    return pl.pallas_call(
        functools.partial(_prep_kernel, ts=ts),
        grid=(B, nt),
        in_specs=[pl.BlockSpec((ts, LANES), lambda b, t: (b * nt + t, C_MISC // LANES)),
                  pl.BlockSpec((1, LANES), lambda b, t: (0, 0)),
                  pl.BlockSpec((1, 1, ts), lambda b, t: (b, 0, t)),
                  pl.BlockSpec((8, LANES), lambda b, t: (0, 0)),
                  full(rope_f), full(rope_e),
                  pl.BlockSpec((ts, W_HEADS), lambda b, t: (b * nt + t, C_NQ // W_HEADS)),
                  pl.BlockSpec((ts, LANES), lambda b, t: (b * nt + t, C_NKV // LANES + 2)),
                  pl.BlockSpec((ts, LANES), lambda b, t: (b * nt + t, C_NKV // LANES + 4)),
                  full(place[0]), full(place[1]), full(place[2]), full(place[3]),
                  pl.BlockSpec((ts, W_HEADS), lambda b, t: (b * nt + t, C_FV // W_HEADS)),
                  pl.BlockSpec((ts, LANES), lambda b, t: (b * nt + t, C_NKV // LANES + 3)),
                  pl.BlockSpec((ts, LANES), lambda b, t: (b * nt + t, C_NKV // LANES + 5))],
        out_specs=[row_blk(LANES), row_blk(wa), row_blk(wb), row_blk(W_HEADS), row_blk(LANES), row_blk(LANES),
                   pl.BlockSpec((1, H_FOX // 2, ts // TQA, LANES, TQA), lambda b, t: (b, 0, t, 0, 0)),
                   pl.BlockSpec((1, ts // LANES, LANES, LANES), lambda b, t: (b, t, 0, 0)),
                   pl.BlockSpec((1, ts // LANES, LANES, LANES), lambda b, t: (b, t, 0, 0))],
        out_shape=[jax.ShapeDtypeStruct((T, LANES), F32),
                   jax.ShapeDtypeStruct((T, wa), BF16),
                   jax.ShapeDtypeStruct((T, wb), BF16),
                   jax.ShapeDtypeStruct((T, W_HEADS), F32),
                   jax.ShapeDtypeStruct((T, LANES), BF16),
                   jax.ShapeDtypeStruct((T, LANES), BF16),
                   jax.ShapeDtypeStruct((B, H_FOX // 2, S // TQA, LANES, TQA), BF16),
                   jax.ShapeDtypeStruct((B, S // LANES, LANES, LANES), BF16),
                   jax.ShapeDtypeStruct((B, S // LANES, LANES, LANES), BF16)],
        scratch_shapes=[pltpu.VMEM((1, LANES), F32)],
        compiler_params=_cparams(("parallel", "arbitrary")),
    )(proj, bias_row, pos3, rope_c, rope_f, rope_e, proj, proj, proj, *place, proj, proj, proj)


def _compress_kernel(kc_ref, vc_ref, pek_ref, w1k_ref, b1k_ref, w2k_ref,
                     pev_ref, w1v_ref, b1v_ref, w2v_ref, ok_ref, ov_ref, *, nc):
    for x_ref, pe_ref, w1_ref, b1_ref, w2_ref, o_ref in (
            (kc_ref, pek_ref, w1k_ref, b1k_ref, w2k_ref, ok_ref),
            (vc_ref, pev_ref, w1v_ref, b1v_ref, w2v_ref, ov_ref)):
        h_first = jnp.zeros((nc, NSA_KV_HEADS * CMP_HIDDEN), F32)
        h_second = jnp.zeros((nc, NSA_KV_HEADS * CMP_HIDDEN), F32)
        for l in range(CMP_STRIDE):
            x = x_ref[pl.ds(l, nc, stride=CMP_STRIDE), :]
            h_first = h_first + _dot((x + pe_ref[l:l + 1, :]).astype(BF16), w1_ref[l])
            h_second = h_second + _dot((x + pe_ref[CMP_STRIDE + l:CMP_STRIDE + l + 1, :]).astype(BF16),
                                       w1_ref[CMP_STRIDE + l])
        h = h_first + pltpu.roll(h_second, nc - 1, axis=0) + b1_ref[...]
        a = h * jax.nn.sigmoid(h)
        out = _dot(a.astype(BF16), w2_ref[...])
        o_ref[0] = (out.T if o_ref is ov_ref else out).astype(o_ref.dtype)


def _compress(proj, pek, w1k, b1k, w2k, pev, w1v, b1v, w2v, B, S):
    nc = S // CMP_STRIDE
    full = lambda a: pl.BlockSpec(a.shape, lambda b: (0,) * a.ndim)
    out_spec = pl.BlockSpec((1, nc, LANES), lambda b: (b, 0, 0))
    out_spec_t = pl.BlockSpec((1, LANES, nc), lambda b: (b, 0, 0))
    return pl.pallas_call(
        functools.partial(_compress_kernel, nc=nc),
        grid=(B,),
        in_specs=[pl.BlockSpec((S, LANES), lambda b: (b, C_NKV // LANES)),
                  pl.BlockSpec((S, LANES), lambda b: (b, C_NKV // LANES + 1)),
                  full(pek), full(w1k), full(b1k), full(w2k),
                  full(pev), full(w1v), full(b1v), full(w2v)],
        out_specs=[out_spec, out_spec_t],
        out_shape=[jax.ShapeDtypeStruct((B, nc, LANES), BF16), jax.ShapeDtypeStruct((B, LANES, nc), BF16)],
        compiler_params=_cparams(("parallel",)),
    )(proj, proj, pek, w1k, b1k, w2k, pev, w1v, b1v, w2v)


def _compress_weights(pe, w1, b1, w2):
    G = NSA_KV_HEADS
    pe2 = jnp.tile(pe, (1, G))
    w1l = w1.astype(BF16).reshape(CMP_LEN, HEAD_DIM, CMP_HIDDEN)
    w2b = w2.astype(BF16)
    z1 = jnp.zeros_like(w1l)
    z2 = jnp.zeros_like(w2b)
    w1_bd = jnp.concatenate([jnp.concatenate([w1l, z1], axis=2), jnp.concatenate([z1, w1l], axis=2)], axis=1)
    w2_bd = jnp.concatenate([jnp.concatenate([w2b, z2], axis=1), jnp.concatenate([z2, w2b], axis=1)], axis=0)
    return pe2, w1_bd, jnp.tile(b1.reshape(1, -1), (1, G)), w2_bd


def _nsa_kernel(qn_ref, qr_ref, kc_ref, vct_ref, ks_ref, vst_ref, kw_ref, vwt_ref, gt_ref, ot_ref,
                o_ref, acc_ref, m_ref, l_ref, s_ref, cmax_ref, *, S):
    i = pl.program_id(1)
    nc = S // CMP_STRIDE
    ns = S // SEL_BLOCK
    k_top = min(SEL_TOPK, ns)
    G = NSA_KV_HEADS
    R = H_NSA // G
    RT = R * TQ
    HD = HEAD_DIM
    low = lax.broadcasted_iota(jnp.int32, (TQ, LANES), 1) < HD
    gates_t = gt_ref[...].T
    qn = qn_ref[...] * SCALE
    qr = qr_ref[...]

    def tiled(mask, s, fill):
        return jnp.concatenate([jnp.where(mask, s[:, r * TQ:(r + 1) * TQ], fill) for r in range(R)], axis=1)

    def stack(qfull, g):
        in_g = low if g == 0 else jnp.logical_not(low)
        parts = []
        for r in range(R):
            a, b = divmod(R * g + r, 2)
            blk = qfull[:, a * LANES:(a + 1) * LANES]
            if b != g:
                blk = pltpu.roll(blk, HD, axis=1)
            parts.append(jnp.where(in_g, blk, 0.0))
        return jnp.concatenate(parts, axis=0).astype(BF16)

    def rows_of(g, x):
        return x[g * HD:(g + 1) * HD, :]


    qn_s = [stack(qn, g) for g in range(G)]
    c_valid = (lax.broadcasted_iota(jnp.int32, (nc, TQ), 0) * CMP_STRIDE + (CMP_LEN - 1)
               <= lax.broadcasted_iota(jnp.int32, (nc, TQ), 1) + i * TQ)
    scs = [tiled(c_valid, _dot_nt(kc_ref[0], qn_s[g]), NEG) for g in range(G)]
    es = [tiled(c_valid, jnp.exp(sc - jnp.max(sc, axis=0, keepdims=True)), 0.0) for sc in scs]
    dens = [jnp.sum(e, axis=0, keepdims=True) for e in es]
    p_cmps = [e * (1.0 / jnp.where(den > 0.0, den, 1.0)) for e, den in zip(es, dens)]
    vct = vct_ref[0]
    o_cmps = [_dot(rows_of(g, vct), p_cmps[g].astype(BF16)) for g in range(G)]

    p_slcs = []
    for p in p_cmps:
        p_sum = p[:, 0:TQ]
        for r in range(1, R):
            p_sum = p_sum + p[:, r * TQ:(r + 1) * TQ]
        p_slcs.append(_dot(ot_ref[...], _split_bf16(p_sum, 0)))
    n_idx = lax.broadcasted_iota(jnp.int32, (ns, TQ), 0)
    t_s = lax.broadcasted_iota(jnp.int32, (ns, TQ), 1) + i * TQ
    forced = jnp.logical_or(n_idx == 0, n_idx == (t_s >> SEL_SHIFT))
    s_valid = n_idx * SEL_BLOCK <= t_s
    scores = [jnp.where(forced, 1e4, jnp.where(s_valid, p_slc, -1.0)) for p_slc in p_slcs]
    ranks = [jnp.zeros((ns, TQ), F32) for _ in range(G)]
    for m_i in range(ns):
        for g in range(G):
            sm = scores[g][m_i:m_i + 1, :]
            ge = jnp.where(sm >= scores[g], 1.0, 0.0)
            gt = jnp.where(sm > scores[g], 1.0, 0.0)
            ranks[g] = ranks[g] + jnp.where(n_idx > m_i, ge, gt)
    sels = []
    for g in range(G):
        sel = jnp.where(ranks[g] < k_top, 1.0, 0.0)
        if ns < LANES:
            sel = jnp.concatenate([sel, jnp.zeros((LANES - ns, TQ), F32)], axis=0)
        sels.append(sel.astype(BF16))

    qr_s = [stack(qr, g) for g in range(G)]
    m_ref[...] = jnp.full(m_ref.shape, NEG, F32)
    l_ref[...] = jnp.zeros_like(l_ref)
    acc_ref[...] = jnp.zeros_like(acc_ref)

    def stash(scored):
        for c, sc in scored:
            s_ref[c, 0:sc.shape[0], :] = sc
            cmax_ref[c] = jnp.max(sc, axis=0, keepdims=True)

    def absorb(items):
        stats = []
        for c, _, cmax, _ in items:
            m_old = m_ref[c]
            stats.append((m_old, jnp.maximum(m_old, cmax)))
        ps = [jnp.exp(sc - m_new) for (_, sc, _, _), (_, m_new) in zip(items, stats)]
        out = []
        for (c, _, _, vt), (m_old, m_new), p in zip(items, stats, ps):
            alpha = jnp.exp(m_old - m_new)
            m_ref[c] = m_new
            l_ref[c] = alpha * l_ref[c] + jnp.sum(p, axis=0, keepdims=True)
            out.append((c, alpha, _dot(vt, p.astype(BF16))))
        return out

    def accumulate(updates):
        for c, alpha, pv in updates:
            acc_ref[c] = alpha * acc_ref[c] + pv

    def chosen(g, off, width):
        blk = (lax.broadcasted_iota(jnp.int32, (width, LANES), 0) + off) >> SEL_SHIFT
        expand = jnp.where(blk == lax.broadcasted_iota(jnp.int32, (width, LANES), 1), 1.0, 0.0).astype(BF16)
        return _dot(expand, sels[g]) > 0.5

    n_win = jnp.minimum(i, (WINDOW + TQ - 2) // LANES) + 1
    n_w = n_win + ((i + 1 - n_win) & 1)
    n_old = (i + 1 - n_w) // 2

    def win_scores(j):
        off = pl.multiple_of((i - j) * LANES, LANES)
        s_pos = lax.broadcasted_iota(jnp.int32, (LANES, TQ), 0) + off
        t_q = lax.broadcasted_iota(jnp.int32, (LANES, TQ), 1) + i * TQ
        causal = s_pos <= t_q
        band = jnp.logical_and(causal, t_q - s_pos < WINDOW)
        ks_t, kw_t = ks_ref[pl.ds(off, LANES), :], kw_ref[pl.ds(off, LANES), :]
        scored = []
        for g in range(G):
            pick = jnp.logical_and(chosen(g, off, LANES), causal)
            scored.append((2 * g, tiled(pick, _dot_nt(ks_t, qr_s[g]), NEG)))
            scored.append((2 * g + 1, tiled(band, _dot_nt(kw_t, qr_s[g]), NEG)))
        return scored

    def win_items(j):
        vs_t, vw_t = vst_ref[0, i - j], vwt_ref[0, i - j]
        items = []
        for g in range(G):
            items.append((2 * g, s_ref[2 * g, 0:LANES, :], cmax_ref[2 * g], rows_of(g, vs_t)))
            items.append((2 * g + 1, s_ref[2 * g + 1, 0:LANES, :], cmax_ref[2 * g + 1], rows_of(g, vw_t)))
        return items

    def old_first_tile(j):
        return i - n_w - 2 * j - 1

    def old_scores(j):
        off = pl.multiple_of(old_first_tile(j) * LANES, LANES)
        ks_t = ks_ref[pl.ds(off, 2 * LANES), :]
        return [(2 * g, tiled(chosen(g, off, 2 * LANES), _dot_nt(ks_t, qr_s[g]), NEG)) for g in range(G)]

    def old_items(j):
        kb = old_first_tile(j)
        vs_t = jnp.concatenate([vst_ref[0, kb], vst_ref[0, kb + 1]], axis=1)
        return [(2 * g, s_ref[2 * g], cmax_ref[2 * g], rows_of(g, vs_t)) for g in range(G)]

    stash(win_scores(0))

    def win_body(j, carry):
        items = win_items(j - 1)
        scored = win_scores(j)
        updates = absorb(items)
        stash(scored)
        accumulate(updates)
        return carry

    lax.fori_loop(1, n_w, win_body, 0)

    @pl.when(n_old == 0)
    def _():
        accumulate(absorb(win_items(n_w - 1)))

    @pl.when(n_old > 0)
    def _():
        items = win_items(n_w - 1)
        scored = old_scores(0)
        updates = absorb(items)
        stash(scored)
        accumulate(updates)

        def old_body(j, carry):
            items = old_items(j - 1)
            scored = old_scores(j)
            updates = absorb(items)
            stash(scored)
            accumulate(updates)
            return carry

        lax.fori_loop(1, n_old, old_body, 0)
        accumulate(absorb(old_items(n_old - 1)))

    heads = [None] * H_NSA
    for g in range(G):
        l_sel, l_win = l_ref[2 * g], l_ref[2 * g + 1]
        o_sel = acc_ref[2 * g] * (1.0 / jnp.where(l_sel > 0.0, l_sel, 1.0))
        o_win = acc_ref[2 * g + 1] * (1.0 / jnp.where(l_win > 0.0, l_win, 1.0))
        for r in range(R):
            h = R * g + r
            cols = slice(r * TQ, (r + 1) * TQ)
            heads[h] = (gates_t[3 * h:3 * h + 1, :] * o_cmps[g][:, cols]
                        + gates_t[3 * h + 1:3 * h + 2, :] * o_sel[:, cols]
                        + gates_t[3 * h + 2:3 * h + 3, :] * o_win[:, cols])
    for a in range(H_NSA // 2):
        pair = jnp.concatenate([heads[2 * a], heads[2 * a + 1]], axis=0)
        o_ref[:, a * LANES:(a + 1) * LANES] = pair.T.astype(o_ref.dtype)


def _nsa_attention(proj, q_rope, k_cmp, v_cmp_t, ks_r, vs_t, kw_r, vw_t, gates, overlap_t, B, S):
    nq = S // TQ
    nc = S // CMP_STRIDE
    nk = S // LANES
    G = NSA_KV_HEADS
    rt = H_NSA // G * TQ
    k_full = pl.BlockSpec((S, LANES), lambda b, i: (b, 0))
    vt_full = pl.BlockSpec((1, nk, LANES, LANES), lambda b, i: (b, 0, 0, 0))
    return pl.pallas_call(
        functools.partial(_nsa_kernel, S=S),
        grid=(B, nq),
        in_specs=[pl.BlockSpec((TQ, W_HEADS), lambda b, i: (b * nq + i, C_NQ // W_HEADS)),
                  pl.BlockSpec((TQ, W_HEADS), lambda b, i: (b * nq + i, 0)),
                  pl.BlockSpec((1, nc, LANES), lambda b, i: (b, 0, 0)),
                  pl.BlockSpec((1, LANES, nc), lambda b, i: (b, 0, 0)),
                  k_full, vt_full, k_full, vt_full,
                  pl.BlockSpec((TQ, LANES), lambda b, i: (b * nq + i, 0)),
                  pl.BlockSpec(overlap_t.shape, lambda b, i: (0, 0))],
        out_specs=pl.BlockSpec((TQ, W_HEADS), lambda b, i: (b * nq + i, 0)),
        out_shape=jax.ShapeDtypeStruct((B * S, W_HEADS), BF16),
        scratch_shapes=[pltpu.VMEM((2 * G, HEAD_DIM, rt), F32), pltpu.VMEM((2 * G, 1, rt), F32),
                        pltpu.VMEM((2 * G, 1, rt), F32), pltpu.VMEM((2 * G, 2 * LANES, rt), F32),
                        pltpu.VMEM((2 * G, 1, rt), F32)],
        compiler_params=_cparams(("parallel", "arbitrary")),
    )(proj, q_rope, k_cmp, v_cmp_t, ks_r, vs_t, kw_r, vw_t, gates, overlap_t)


def _merge_kernel(x_ref, gpre_ref, wm_ref, osb_ref, onsa_ref, ofox_ref,
                  wsb_ref, wnsa_ref, wfox_ref, wout_ref, g_ref, o_ref):
    x = x_ref[...]
    D = x.shape[1]
    h = _rms(x, gpre_ref[...]).astype(BF16)
    y = None
    for c, (b_ref, w_ref) in enumerate(((osb_ref, wsb_ref), (onsa_ref, wnsa_ref), (ofox_ref, wfox_ref))):
        gate = jax.nn.sigmoid(_dot(h, wm_ref[:, c * D:(c + 1) * D]))
        term = gate * _dot(b_ref[...], w_ref[...])
        y = term if y is None else y + term
    z = _dot(y.astype(BF16), wout_ref[...])
    o_ref[...] = x + _rms(z, g_ref[...])


def _merge(x, g_pre, w_merge, o_sb, o_nsa, o_fox, w_sb, w_nsa, w_fox, w_out, g, tm):
    T, D = x.shape
    row = lambda w: pl.BlockSpec((tm, w), lambda i: (i, 0))
    full = lambda a: pl.BlockSpec(a.shape, lambda i: (0, 0))
    return pl.pallas_call(
        _merge_kernel,
        grid=(T // tm,),
        in_specs=[row(D), full(g_pre), full(w_merge), row(W_HEADS), row(W_HEADS), row(W_HEADS),
                  full(w_sb), full(w_nsa), full(w_fox), full(w_out), full(g)],
        out_specs=row(D),
        out_shape=jax.ShapeDtypeStruct((T, D), F32),
        compiler_params=_cparams(("parallel",)),
    )(x, g_pre, w_merge, o_sb, o_nsa, o_fox, w_sb, w_nsa, w_fox, w_out, g)


def _mem_kv_kernel(mem_ref, g_ref, wk_ref, wv_ref, k_ref, vt_ref):
    mn = _rms(mem_ref[0], g_ref[...]).astype(BF16)
    k_ref[0] = _dot(mn, wk_ref[...]).astype(BF16)
    vt_ref[0] = _dot(mn, wv_ref[...]).T.astype(BF16)


def _mem_kv(mem, g, wk, wv):
    B, M, D = mem.shape
    full = lambda a: pl.BlockSpec(a.shape, lambda b: (0, 0))
    return pl.pallas_call(
        _mem_kv_kernel,
        grid=(B,),
        in_specs=[pl.BlockSpec((1, M, D), lambda b: (b, 0, 0)), full(g), full(wk), full(wv)],
        out_specs=[pl.BlockSpec((1, M, W_MEM), lambda b: (b, 0, 0)),
                   pl.BlockSpec((1, W_MEM, M), lambda b: (b, 0, 0))],
        out_shape=[jax.ShapeDtypeStruct((B, M, W_MEM), BF16), jax.ShapeDtypeStruct((B, W_MEM, M), BF16)],
        compiler_params=_cparams(("parallel",)),
    )(mem, g, wk, wv)


def _mem_attn_kernel(x_ref, g_ref, wq_ref, k_ref, vt_ref, wo_ref, gp_ref, o_ref):
    x = x_ref[...]
    q = _dot(_rms(x, g_ref[...]).astype(BF16), wq_ref[...]) * SCALE
    k = k_ref[0]
    vt = vt_ref[0]
    lane = lax.broadcasted_iota(jnp.int32, q.shape, 1)
    heads = range(H_MEM)
    qh = [jnp.where(jnp.logical_and(lane >= h * HEAD_DIM, lane < (h + 1) * HEAD_DIM), q, 0.0).astype(BF16)
          for h in heads]
    ss = [_dot_nt(k, qh[h]) for h in heads]
    es = [jnp.exp(s - jnp.max(s, axis=0, keepdims=True)) for s in ss]
    ps = [(e / jnp.sum(e, axis=0, keepdims=True)).astype(BF16) for e in es]
    o_t = jnp.concatenate([_dot(vt[h * HEAD_DIM:(h + 1) * HEAD_DIM, :], ps[h]) for h in heads], axis=0)
    y = _dot(o_t.T.astype(BF16), wo_ref[...])
    o_ref[...] = x + _rms(y, gp_ref[...])


def _mem_attn(x, g_pre, wq, k, v_t, wo, g_post, B, S, tm):
    T, D = x.shape
    M = k.shape[1]
    nt = S // tm
    full = lambda a: pl.BlockSpec(a.shape, lambda b, i: (0, 0))
    row = pl.BlockSpec((tm, D), lambda b, i: (b * nt + i, 0))
    return pl.pallas_call(
        _mem_attn_kernel,
        grid=(B, nt),
        in_specs=[row, full(g_pre), full(wq), pl.BlockSpec((1, M, W_MEM), lambda b, i: (b, 0, 0)),
                  pl.BlockSpec((1, W_MEM, M), lambda b, i: (b, 0, 0)), full(wo), full(g_post)],
        out_specs=row,
        out_shape=jax.ShapeDtypeStruct((T, D), F32),
        compiler_params=_cparams(("parallel", "parallel")),
    )(x, g_pre, wq, k, v_t, wo, g_post)


def _ffn_kernel(x_ref, g_ref, wg_ref, wu_ref, wd_ref, gp_ref, o_ref, *, tf):
    x = x_ref[...]
    h = _rms(x, g_ref[...]).astype(BF16)
    y = None
    for c in range(wg_ref.shape[1] // tf):
        cols = slice(c * tf, (c + 1) * tf)
        a = _dot(h, wg_ref[:, cols])
        u = _dot(h, wu_ref[:, cols])
        part = _dot((a * jax.nn.sigmoid(a) * u).astype(BF16), wd_ref[cols, :])
        y = part if y is None else y + part
    o_ref[...] = x + _rms(y, gp_ref[...])


def _ffn(x, g_pre, wg, wu, wd, g_post, tm, tf):
    T, D = x.shape
    row = pl.BlockSpec((tm, D), lambda i: (i, 0))
    vec = pl.BlockSpec((1, D), lambda i: (0, 0))
    resident = lambda a: pl.BlockSpec(a.shape, lambda i: (0, 0), pipeline_mode=pl.Buffered(1))
    return pl.pallas_call(
        functools.partial(_ffn_kernel, tf=tf),
        grid=(T // tm,),
        in_specs=[row, vec, resident(wg), resident(wu), resident(wd), vec],
        out_specs=row,
        out_shape=jax.ShapeDtypeStruct((T, D), F32),
        compiler_params=_cparams(("parallel",)),
    )(x, g_pre, wg, wu, wd, g_post)


def _scan_matrix():
    j = np.arange(LANES)
    later = (j[:, None] > j[None, :]).astype(np.float32)
    u = np.concatenate([later, np.ones((LANES, LANES), np.float32)], axis=1)
    return jnp.asarray(np.concatenate([u, u], axis=0), dtype=BF16)


def _overlap_t(S):
    nc, ns = S // CMP_STRIDE, S // SEL_BLOCK
    c0 = np.arange(nc) * CMP_STRIDE
    n0 = np.arange(ns) * SEL_BLOCK
    ov = (c0[None, :] < n0[:, None] + SEL_BLOCK) & (c0[None, :] + CMP_LEN > n0[:, None])
    ov = ov & (np.arange(nc)[None, :] < nc - 1)
    ov = ov.astype(np.float32)
    return jnp.asarray(np.concatenate([ov, ov], axis=1), dtype=BF16)


FOX_F_LANE = 24
N_PARTS = 3


def _fox_bias_placement():
    n_pair = H_FOX // 2
    pa = np.zeros((N_PARTS * LANES, n_pair * 2 * LANES), np.float32)
    pb = np.zeros((N_PARTS * LANES, n_pair * LANES), np.float32)
    oa = np.zeros((1, n_pair * 2 * LANES), np.float32)
    ob = np.zeros((1, n_pair * LANES), np.float32)
    for p in range(n_pair):
        for e in range(2):
            src = FOX_F_LANE + 2 * p + e
            for x in range(N_PARTS):
                pa[x * LANES + src, (2 * p + e) * LANES + 8 * e + x] = 1.0
                oa[0, (2 * p + e) * LANES + 8 * e + N_PARTS + x] = 1.0
                pb[x * LANES + src, p * LANES + 8 * e + N_PARTS + x] = -1.0
                ob[0, p * LANES + 8 * e + x] = 1.0
    return (jnp.asarray(pa, dtype=BF16), jnp.asarray(pb, dtype=BF16), jnp.asarray(oa), jnp.asarray(ob))


def _rope_tables():
    half = ROPE_DIM // 2
    inv_freq = ROPE_THETA ** (-jnp.arange(half, dtype=F32) / half)
    d = np.arange(LANES) % HEAD_DIM
    rot = d < ROPE_DIM
    one_hot = (np.arange(half)[:, None] == (d % half)[None, :]) & rot[None, :]
    to_cos = one_hot.astype(np.float32)
    to_up = (one_hot & (d >= half)[None, :]).astype(np.float32)
    to_dn = -(one_hot & (d < half)[None, :]).astype(np.float32)
    zero = np.zeros_like(to_cos)
    cos_rows = np.concatenate([to_cos, zero, zero], axis=1)
    sin_rows = np.concatenate([zero, to_up, to_dn], axis=1)
    spread = np.concatenate([cos_rows] * N_PARTS + [sin_rows] * N_PARTS, axis=0)
    base = np.zeros((8, LANES), np.float32)
    base[0] = (~rot).astype(np.float32)
    return jnp.asarray(base), inv_freq.reshape(half, 1), jnp.asarray(spread, dtype=BF16)


def _reorder_w_in(w):
    pad = jnp.zeros((w.shape[0], MISC_W - (_O_FOX_Q - _O_NSA_G) - (_O_MERGE - _O_FOX_F)), w.dtype)
    return jnp.concatenate([
        w[:, :_O_NSA_G - 6 * LANES],
        w[:, _O_FOX_Q:_O_FOX_F],
        w[:, _O_NSA_G - 6 * LANES:_O_NSA_G],
        w[:, _O_NSA_G:_O_FOX_Q],
        w[:, _O_FOX_F:_O_MERGE],
        pad], axis=1)


def _row_tile(n, target):
    t = min(n, target)
    while n % t:
        t //= 2
    return t


def kernel(x, mem, positions, g_pre_mix, g_post_mix, g_pre_mem, g_mem, g_post_mem, g_pre_ffn, g_post_ffn,
           w_in, b_fox_f, cmp_pe_k, cmp_w1_k, cmp_b1_k, cmp_w2_k, cmp_pe_v, cmp_w1_v, cmp_b1_v, cmp_w2_v,
           w_up_sb, w_up_nsa, w_up_fox, w_out, w_mem_q, w_mem_k, w_mem_v, w_mem_o,
           w_ffn_gate, w_ffn_up, w_ffn_down):
    B, S, D = x.shape
    T = B * S
    depth = w_in.shape[0]
    u2 = _scan_matrix()
    overlap_t = _overlap_t(S)
    rope_c, rope_f, rope_e = _rope_tables()
    place = _fox_bias_placement()
    pos3 = positions.reshape(B, 1, S)
    vec = lambda g: g.reshape(1, -1)
    tm_mid = _row_tile(T, 512)
    d_ff = w_ffn_gate.shape[2]
    tf = 2 * LANES if d_ff % (2 * LANES) == 0 else d_ff

    w_in_b = w_in.astype(BF16)
    xf = x.reshape(T, D)
    for l in range(depth):
        proj = _norm_matmul(xf, vec(g_pre_mix[l]), _reorder_w_in(w_in_b[l]), tm_mid, 4 * LANES)

        o_sb = _sb_attention(proj, u2, B, S)

        bias_row = jnp.zeros((1, LANES), F32).at[0, FOX_F_LANE:FOX_F_LANE + H_FOX].set(b_fox_f[l])
        gates, fox_a, fox_b, q_rope, ks_r, kw_r, v_fox_t, vs_t, vw_t = _prep(
            proj, bias_row, pos3, rope_c, rope_f, rope_e, place, B, S)
        o_fox = _fox_attention(proj, v_fox_t, fox_a, fox_b, B, S)

        k_cmp, v_cmp_t = _compress(
            proj, *_compress_weights(cmp_pe_k[l], cmp_w1_k[l], cmp_b1_k[l], cmp_w2_k[l]),
            *_compress_weights(cmp_pe_v[l], cmp_w1_v[l], cmp_b1_v[l], cmp_w2_v[l]), B, S)
        o_nsa = _nsa_attention(proj, q_rope, k_cmp, v_cmp_t, ks_r, vs_t, kw_r, vw_t, gates, overlap_t, B, S)

        xf = _merge(xf, vec(g_pre_mix[l]), w_in_b[l][:, _O_MERGE:_O_END], o_sb, o_nsa, o_fox,
                    w_up_sb[l].astype(BF16), w_up_nsa[l].astype(BF16),
                    w_up_fox[l].astype(BF16), w_out[l].astype(BF16), vec(g_post_mix[l]), tm_mid)

        k_mem, v_mem = _mem_kv(mem, vec(g_mem[l]), w_mem_k[l].astype(BF16), w_mem_v[l].astype(BF16))
        xf = _mem_attn(xf, vec(g_pre_mem[l]), w_mem_q[l].astype(BF16), k_mem, v_mem,
                       w_mem_o[l].astype(BF16), vec(g_post_mem[l]), B, S, _row_tile(S, 512))

        xf = _ffn(xf, vec(g_pre_ffn[l]), w_ffn_gate[l].astype(BF16), w_ffn_up[l].astype(BF16),
                  w_ffn_down[l].astype(BF16), vec(g_post_ffn[l]), tm_mid, tf)
    return xf.reshape(B, S, D)
```

```python
import functools

import numpy as np
import jax
import jax.numpy as jnp
from jax import lax
from jax.experimental import pallas as pl
from jax.experimental.pallas import tpu as pltpu

F32 = jnp.float32
BF16 = jnp.bfloat16

D_MODEL = 1024
HEAD_DIM = 64
H_SB = 8
H_NSA = 8
NSA_KV_HEADS = 2
H_FOX = 8
H_MEM = 4
N_BRANCH = 3
ROPE_THETA = 500000.0
ROPE_DIM = HEAD_DIM // 4
CMP_STRIDE = 16
CMP_LEN = 2 * CMP_STRIDE
CMP_HIDDEN = 256
SEL_BLOCK = 64
SEL_SHIFT = SEL_BLOCK.bit_length() - 1
SEL_TOPK = 8
WINDOW = 512
W_HEADS = 8 * HEAD_DIM
W_MEM = H_MEM * HEAD_DIM
EPS = 1e-6
SCALE = HEAD_DIM ** -0.5
NEG = -1e30
LOG_ZERO = -104.0
LOG2E = 1.4426950408889634

LANES = 128
TQ = 128
TQA = 256
TQF = 512

C_SBQ, C_SBK, C_SBV = 0, 512, 1024
C_NQ = 1536
C_FQ, C_FK, C_FV = 2048, 2560, 3072
C_NKV = 3584
C_MISC = 4352
MISC_W = 256
N_IN = 4608
_O_NSA_G, _O_FOX_Q, _O_FOX_F, _O_MERGE, _O_END = 2816, 2840, 4376, 4384, 7456

VMEM_LIMIT = 56 * 1024 * 1024


def _cparams(sem):
    return pltpu.CompilerParams(dimension_semantics=sem, vmem_limit_bytes=VMEM_LIMIT)


def _dot(a, b):
    return jnp.dot(a, b, preferred_element_type=F32)


def _dot_nt(a, b):
    return lax.dot_general(a, b, (((1,), (1,)), ((), ())), preferred_element_type=F32)


def _rms(x, g):
    ms = jnp.mean(x * x, axis=-1, keepdims=True)
    return x * lax.rsqrt(ms + EPS) * g


def _split_bf16(x, axis=1):
    hi = x.astype(BF16)
    lo = (x - hi.astype(F32)).astype(BF16)
    return jnp.concatenate([hi, lo], axis=axis)


def _norm_matmul_kernel(x_ref, g_ref, w_ref, o_ref, *, tn):
    h = _rms(x_ref[...], g_ref[...]).astype(BF16)
    for c in range(w_ref.shape[1] // tn):
        o_ref[:, c * tn:(c + 1) * tn] = _dot(h, w_ref[:, c * tn:(c + 1) * tn])


def _norm_matmul(x, g, w, tm, tn):
    T, D = x.shape
    N = w.shape[1]
    return pl.pallas_call(
        functools.partial(_norm_matmul_kernel, tn=tn),
        grid=(T // tm,),
        in_specs=[pl.BlockSpec((tm, D), lambda i: (i, 0)),
                  pl.BlockSpec((1, D), lambda i: (0, 0)),
                  pl.BlockSpec((D, N), lambda i: (0, 0), pipeline_mode=pl.Buffered(1))],
        out_specs=pl.BlockSpec((tm, N), lambda i: (i, 0)),
        out_shape=jax.ShapeDtypeStruct((T, N), F32),
        compiler_params=_cparams(("parallel",)),
    )(x, g, w)


def _stack_pair(q, low):
    return jnp.concatenate([jnp.where(low, q, 0.0), jnp.where(low, 0.0, q)], axis=0).astype(BF16)


def _sb_kernel(q_ref, k_ref, v_ref, u_ref, o_ref, acc_ref, cs_ref, arg_ref, rs_ref):
    i = pl.program_id(2)
    rt = 2 * TQA
    n_grp = TQA // LANES
    lane = lax.broadcasted_iota(jnp.int32, (rt, LANES), 1)
    t_q = (lax.broadcasted_iota(jnp.int32, (rt, LANES), 0) & (TQA - 1)) + i * TQA
    low = lax.broadcasted_iota(jnp.int32, (TQA, LANES), 1) < HEAD_DIM
    qs = _stack_pair(q_ref[...] * SCALE, low)

    def offsets(kb_first):
        return [pl.multiple_of((kb_first - d) * LANES, LANES) for d in range(n_grp)]

    def logits(offs):
        return [_dot_nt(qs, k_ref[pl.ds(off, LANES), :].astype(BF16)) for off in offs]

    def scan(offs, zs, on_diagonal):
        drops, log_betas, stricts = [], [], []
        for off, z in zip(offs, zs):
            drop = jnp.maximum(z, 0.0) + jnp.log(1.0 + jnp.exp2(jnp.abs(z) * -LOG2E))
            log_betas.append(z - drop)
            if on_diagonal:
                stricts.append((lane + off) < t_q)
                drop = jnp.where(stricts[-1], drop, 0.0)
            drops.append(drop)
        c2s = [_dot(_split_bf16(drop), u_ref[...]) for drop in drops]
        args = [log_beta - c2[:, :LANES] for log_beta, c2 in zip(log_betas, c2s)]
        if on_diagonal:
            args = [jnp.where(strict, arg, NEG) for strict, arg in zip(stricts, args)]
        return args, [c2[:, LANES:] for c2 in c2s]

    def stash(args, row_sums):
        for d in range(n_grp):
            arg_ref[d] = args[d]
            rs_ref[d] = row_sums[d]

    def weights(offs, args, row_sums):
        cs = cs_ref[...]
        pvs = []
        for d, off in enumerate(offs):
            w = jnp.exp(args[d] - cs)
            pvs.append(_dot(w.astype(BF16), v_ref[pl.ds(off, LANES), :].astype(BF16)))
            cs = cs + row_sums[d]
        return cs, pvs

    def accumulate(cs, pvs):
        acc = acc_ref[...]
        for pv in pvs:
            acc = acc + pv
        acc_ref[...] = acc
        cs_ref[...] = cs

    acc_ref[...] = jnp.zeros_like(acc_ref)
    cs_ref[...] = jnp.zeros_like(cs_ref)
    def reaches_past(cs, row_sums):
        for rs in row_sums:
            cs = cs + rs
        return (jnp.min(cs) <= -LOG_ZERO).astype(jnp.int32)

    first = offsets(n_grp * i + n_grp - 1)
    first_args, first_sums = scan(first, logits(first), True)
    stash(first_args, first_sums)

    def cond(carry):
        j, alive = carry
        return jnp.logical_and(j < i, alive == 1)

    def body(carry):
        j, _ = carry
        cur = offsets(n_grp * (i - j) + n_grp - 1)
        nxt = offsets(n_grp * (i - j) - 1)
        args = [arg_ref[d] for d in range(n_grp)]
        row_sums = [rs_ref[d] for d in range(n_grp)]
        zs = logits(nxt)
        cs, pvs = weights(cur, args, row_sums)
        nxt_args, nxt_sums = scan(nxt, zs, False)
        alive = reaches_past(cs, nxt_sums)
        accumulate(cs, pvs)
        stash(nxt_args, nxt_sums)
        return j + 1, alive

    j_last, _ = lax.while_loop(cond, body, (jnp.int32(0), reaches_past(cs_ref[...], first_sums)))
    last = offsets(n_grp * (i - j_last) + n_grp - 1)
    accumulate(*weights(last, [arg_ref[d] for d in range(n_grp)], [rs_ref[d] for d in range(n_grp)]))
    acc = acc_ref[...]
    o_ref[...] = jnp.where(low, acc[:TQA], acc[TQA:]).astype(o_ref.dtype)


def _sb_attention(proj, u2, B, S):
    nq = S // TQA
    cq, ck, cv = C_SBQ // LANES, C_SBK // LANES, C_SBV // LANES
    return pl.pallas_call(
        _sb_kernel,
        grid=(B, H_SB // 2, nq),
        in_specs=[pl.BlockSpec((TQA, LANES), lambda b, p, i: (b * nq + i, cq + p)),
                  pl.BlockSpec((S, LANES), lambda b, p, i: (b, ck + p)),
                  pl.BlockSpec((S, LANES), lambda b, p, i: (b, cv + p)),
                  pl.BlockSpec((2 * LANES, 2 * LANES), lambda b, p, i: (0, 0))],
        out_specs=pl.BlockSpec((TQA, LANES), lambda b, p, i: (b * nq + i, p)),
        out_shape=jax.ShapeDtypeStruct((B * S, W_HEADS), BF16),
        scratch_shapes=[pltpu.VMEM((2 * TQA, LANES), F32), pltpu.VMEM((2 * TQA, LANES), F32),
                        pltpu.VMEM((TQA // LANES, 2 * TQA, LANES), F32),
                        pltpu.VMEM((TQA // LANES, 2 * TQA, LANES), F32)],
        compiler_params=_cparams(("parallel", "parallel", "arbitrary")),
    )(proj, proj, proj, u2)


def _fox_kernel(q_ref, k_ref, vt_ref, fa_ref, fb_ref, o_ref, acc_ref, m_ref, l_ref, s_ref, cmax_ref):
    i = pl.program_id(2)
    rt = 2 * TQF
    n_diag = TQF // TQA
    low = lax.broadcasted_iota(jnp.int32, (TQF, LANES), 1) < HEAD_DIM
    qs = _stack_pair(q_ref[...] * SCALE, low)
    qa = jnp.concatenate([qs, jnp.concatenate([fa_ref[:, :LANES], fa_ref[:, LANES:]], axis=0)], axis=1)
    acc_ref[...] = jnp.zeros_like(acc_ref)
    l_ref[...] = jnp.zeros_like(l_ref)
    m_ref[...] = jnp.full(m_ref.shape, NEG, F32)

    def scores(kb, on_diagonal):
        off = pl.multiple_of(kb * TQA, TQA)
        kbias = jnp.concatenate([k_ref[pl.ds(off, TQA), :].astype(BF16), fb_ref[pl.ds(off, TQA), :]], axis=1)
        s = _dot_nt(kbias, qa)
        if on_diagonal:
            key_pos = lax.broadcasted_iota(jnp.int32, (TQA, rt), 0) + (kb - n_diag * i) * TQA
            q_pos = lax.broadcasted_iota(jnp.int32, (TQA, rt), 1) & (TQF - 1)
            s = jnp.where(key_pos <= q_pos, s, NEG)
        return s

    def stash(s):
        s_ref[...] = s
        cmax_ref[...] = jnp.max(s, axis=0, keepdims=True)

    def absorb(kb, s, cmax):
        m_old = m_ref[...]
        m_new = jnp.maximum(m_old, cmax)
        p = jnp.exp(s - m_new)
        alpha = jnp.exp(m_old - m_new)
        m_ref[...] = m_new
        l_ref[...] = alpha * l_ref[...] + jnp.sum(p, axis=0, keepdims=True)
        p = p.astype(BF16)
        vt = vt_ref[0, 0, kb]
        return alpha, [_dot(vt[e * HEAD_DIM:(e + 1) * HEAD_DIM, :], p[:, e * TQF:(e + 1) * TQF]) for e in range(2)]

    def accumulate(alpha, pvs):
        for e in range(2):
            acc_ref[e] = alpha[:, e * TQF:(e + 1) * TQF] * acc_ref[e] + pvs[e]

    d0 = n_diag * i
    stash(scores(d0, True))
    for d in range(1, n_diag):
        s_cur, cmax = s_ref[...], cmax_ref[...]
        s_next = scores(d0 + d, True)
        alpha, pvs = absorb(d0 + d - 1, s_cur, cmax)
        stash(s_next)
        accumulate(alpha, pvs)

    def body(j, carry):
        kb = jnp.where(j == 0, d0 + n_diag - 1, d0 - j)
        s_cur, cmax = s_ref[...], cmax_ref[...]
        s_next = scores(d0 - 1 - j, False)
        alpha, pvs = absorb(kb, s_cur, cmax)
        stash(s_next)
        accumulate(alpha, pvs)
        return carry

    lax.fori_loop(0, d0, body, 0)
    accumulate(*absorb(jnp.where(i == 0, n_diag - 1, 0), s_ref[...], cmax_ref[...]))
    l = l_ref[...]
    den = jnp.where(l > 0.0, l, 1.0)
    o_t = jnp.concatenate([acc_ref[0] / den[:, :TQF], acc_ref[1] / den[:, TQF:]], axis=0)
    o_ref[...] = o_t.T.astype(o_ref.dtype)


def _fox_attention(proj, v_t, fox_a, fox_b, B, S):
    nq = S // TQF
    cq, ck = C_FQ // LANES, C_FK // LANES
    return pl.pallas_call(
        _fox_kernel,
        grid=(B, H_FOX // 2, nq),
        in_specs=[pl.BlockSpec((TQF, LANES), lambda b, p, i: (b * nq + i, cq + p)),
                  pl.BlockSpec((S, LANES), lambda b, p, i: (b, ck + p)),
                  pl.BlockSpec((1, 1, S // TQA, LANES, TQA), lambda b, p, i: (b, p, 0, 0, 0)),
                  pl.BlockSpec((TQF, 2 * LANES), lambda b, p, i: (b * nq + i, p)),
                  pl.BlockSpec((S, LANES), lambda b, p, i: (b, p))],
        out_specs=pl.BlockSpec((TQF, LANES), lambda b, p, i: (b * nq + i, p)),
        out_shape=jax.ShapeDtypeStruct((B * S, W_HEADS), BF16),
        scratch_shapes=[pltpu.VMEM((2, HEAD_DIM, TQF), F32), pltpu.VMEM((1, 2 * TQF), F32),
                        pltpu.VMEM((1, 2 * TQF), F32), pltpu.VMEM((TQA, 2 * TQF), F32),
                        pltpu.VMEM((1, 2 * TQF), F32)],
        compiler_params=_cparams(("parallel", "parallel", "arbitrary")),
    )(proj, proj, v_t, fox_a, fox_b)


def _rope(x, cos, s_up, s_dn):
    return x * cos + pltpu.roll(x, ROPE_DIM // 2, axis=1) * s_up + pltpu.roll(x, LANES - ROPE_DIM // 2, axis=1) * s_dn


def _prep_kernel(misc_ref, bias_ref, pos_ref, rc_ref, rf_ref, re_ref, q_ref, ks_ref, kw_ref, pa_ref, pb_ref, oa_ref, ob_ref,
                 vf_ref, vs_ref, vw_ref,
                 gates_ref, fa_ref, fb_ref, qr_ref, ksr_ref, kwr_ref, vft_ref, vst_ref, vwt_ref, carry_ref, *, ts):
    @pl.when(pl.program_id(1) == 0)
    def _():
        carry_ref[...] = jnp.zeros_like(carry_ref)

    logits = misc_ref[...]
    gates_ref[...] = jax.nn.sigmoid(logits)
    zf = logits + bias_ref[...]
    x = jnp.minimum(zf, 0.0) - jnp.log(1.0 + jnp.exp(-jnp.abs(zf)))
    row = lax.broadcasted_iota(jnp.int32, (ts, LANES), 0)
    sh = 1
    while sh < ts:
        x = x + jnp.where(row >= sh, pltpu.roll(x, sh, axis=0), 0.0)
        sh *= 2
    x = x + carry_ref[...]
    carry_ref[...] = x[ts - 1:ts, :]
    hi = x.astype(BF16)
    r1 = x - hi.astype(F32)
    mid = r1.astype(BF16)
    lo = (r1 - mid.astype(F32)).astype(BF16)
    parts = jnp.concatenate([hi, mid, lo], axis=1)
    fa_ref[...] = (_dot(parts, pa_ref[...]) + oa_ref[...]).astype(BF16)
    fb_ref[...] = (_dot(parts, pb_ref[...]) + ob_ref[...]).astype(BF16)

    ang = rf_ref[...] * pos_ref[0].astype(F32)
    parts = []
    for t in (jnp.cos(ang), jnp.sin(ang)):
        hi = t.astype(BF16)
        r1 = t - hi.astype(F32)
        mid = r1.astype(BF16)
        parts += [hi, mid, (r1 - mid.astype(F32)).astype(BF16)]
    spread = lax.dot_general(jnp.concatenate(parts, axis=0), re_ref[...], (((0,), (0,)), ((), ())),
                             preferred_element_type=F32)
    cos = spread[:, :LANES] + rc_ref[0:1, :]
    s_up = spread[:, LANES:2 * LANES]
    s_dn = spread[:, 2 * LANES:]
    for a in range(W_HEADS // LANES):
        qa = q_ref[:, a * LANES:(a + 1) * LANES]
        qr_ref[:, a * LANES:(a + 1) * LANES] = _rope(qa, cos, s_up, s_dn) * SCALE
    ksr_ref[...] = _rope(ks_ref[...], cos, s_up, s_dn).astype(BF16)
    kwr_ref[...] = _rope(kw_ref[...], cos, s_up, s_dn).astype(BF16)

    for p in range(H_FOX // 2):
        for u in range(ts // TQA):
            vft_ref[0, p, u] = vf_ref[u * TQA:(u + 1) * TQA, p * LANES:(p + 1) * LANES].T.astype(BF16)
    for u in range(ts // LANES):
        vst_ref[0, u] = vs_ref[u * LANES:(u + 1) * LANES, :].T.astype(BF16)
        vwt_ref[0, u] = vw_ref[u * LANES:(u + 1) * LANES, :].T.astype(BF16)


def _prep(proj, bias_row, pos3, rope_c, rope_f, rope_e, place, B, S):
    ts = min(512, S)
    nt = S // ts
    T = B * S
    row_blk = lambda w: pl.BlockSpec((ts, w), lambda b, t: (b * nt + t, 0))
    full = lambda a: pl.BlockSpec(a.shape, lambda b, t: (0, 0))
    wa, wb = place[0].shape[1], place[1].shape[1]
    return pl.pallas_call(
        functools.partial(_prep_kernel, ts=ts),
        grid=(B, nt),
        in_specs=[pl.BlockSpec((ts, LANES), lambda b, t: (b * nt + t, C_MISC // LANES)),
                  pl.BlockSpec((1, LANES), lambda b, t: (0, 0)),
                  pl.BlockSpec((1, 1, ts), lambda b, t: (b, 0, t)),
                  pl.BlockSpec((8, LANES), lambda b, t: (0, 0)),
                  full(rope_f), full(rope_e),
                  pl.BlockSpec((ts, W_HEADS), lambda b, t: (b * nt + t, C_NQ // W_HEADS)),
                  pl.BlockSpec((ts, LANES), lambda b, t: (b * nt + t, C_NKV // LANES + 2)),
                  pl.BlockSpec((ts, LANES), lambda b, t: (b * nt + t, C_NKV // LANES + 4)),
                  full(place[0]), full(place[1]), full(place[2]), full(place[3]),
                  pl.BlockSpec((ts, W_HEADS), lambda b, t: (b * nt + t, C_FV // W_HEADS)),
                  pl.BlockSpec((ts, LANES), lambda b, t: (b * nt + t, C_NKV // LANES + 3)),
                  pl.BlockSpec((ts, LANES), lambda b, t: (b * nt + t, C_NKV // LANES + 5))],
        out_specs=[row_blk(LANES), row_blk(wa), row_blk(wb), row_blk(W_HEADS), row_blk(LANES), row_blk(LANES),
                   pl.BlockSpec((1, H_FOX // 2, ts // TQA, LANES, TQA), lambda b, t: (b, 0, t, 0, 0)),
                   pl.BlockSpec((1, ts // LANES, LANES, LANES), lambda b, t: (b, t, 0, 0)),
                   pl.BlockSpec((1, ts // LANES, LANES, LANES), lambda b, t: (b, t, 0, 0))],
        out_shape=[jax.ShapeDtypeStruct((T, LANES), F32),
                   jax.ShapeDtypeStruct((T, wa), BF16),
                   jax.ShapeDtypeStruct((T, wb), BF16),
                   jax.ShapeDtypeStruct((T, W_HEADS), F32),
                   jax.ShapeDtypeStruct((T, LANES), BF16),
                   jax.ShapeDtypeStruct((T, LANES), BF16),
                   jax.ShapeDtypeStruct((B, H_FOX // 2, S // TQA, LANES, TQA), BF16),
                   jax.ShapeDtypeStruct((B, S // LANES, LANES, LANES), BF16),
                   jax.ShapeDtypeStruct((B, S // LANES, LANES, LANES), BF16)],
        scratch_shapes=[pltpu.VMEM((1, LANES), F32)],
        compiler_params=_cparams(("parallel", "arbitrary")),
    )(proj, bias_row, pos3, rope_c, rope_f, rope_e, proj, proj, proj, *place, proj, proj, proj)


def _compress_kernel(kc_ref, vc_ref, pek_ref, w1k_ref, b1k_ref, w2k_ref,
                     pev_ref, w1v_ref, b1v_ref, w2v_ref, ok_ref, ov_ref, *, nc):
    for x_ref, pe_ref, w1_ref, b1_ref, w2_ref, o_ref in (
            (kc_ref, pek_ref, w1k_ref, b1k_ref, w2k_ref, ok_ref),
            (vc_ref, pev_ref, w1v_ref, b1v_ref, w2v_ref, ov_ref)):
        h_first = jnp.zeros((nc, NSA_KV_HEADS * CMP_HIDDEN), F32)
        h_second = jnp.zeros((nc, NSA_KV_HEADS * CMP_HIDDEN), F32)
        for l in range(CMP_STRIDE):
            x = x_ref[pl.ds(l, nc, stride=CMP_STRIDE), :]
            h_first = h_first + _dot((x + pe_ref[l:l + 1, :]).astype(BF16), w1_ref[l])
            h_second = h_second + _dot((x + pe_ref[CMP_STRIDE + l:CMP_STRIDE + l + 1, :]).astype(BF16),
                                       w1_ref[CMP_STRIDE + l])
        h = h_first + pltpu.roll(h_second, nc - 1, axis=0) + b1_ref[...]
        a = h * jax.nn.sigmoid(h)
        out = _dot(a.astype(BF16), w2_ref[...])
        o_ref[0] = (out.T if o_ref is ov_ref else out).astype(o_ref.dtype)


def _compress(proj, pek, w1k, b1k, w2k, pev, w1v, b1v, w2v, B, S):
    nc = S // CMP_STRIDE
    full = lambda a: pl.BlockSpec(a.shape, lambda b: (0,) * a.ndim)
    out_spec = pl.BlockSpec((1, nc, LANES), lambda b: (b, 0, 0))
    out_spec_t = pl.BlockSpec((1, LANES, nc), lambda b: (b, 0, 0))
    return pl.pallas_call(
        functools.partial(_compress_kernel, nc=nc),
        grid=(B,),
        in_specs=[pl.BlockSpec((S, LANES), lambda b: (b, C_NKV // LANES)),
                  pl.BlockSpec((S, LANES), lambda b: (b, C_NKV // LANES + 1)),
                  full(pek), full(w1k), full(b1k), full(w2k),
                  full(pev), full(w1v), full(b1v), full(w2v)],
        out_specs=[out_spec, out_spec_t],
        out_shape=[jax.ShapeDtypeStruct((B, nc, LANES), BF16), jax.ShapeDtypeStruct((B, LANES, nc), BF16)],
        compiler_params=_cparams(("parallel",)),
    )(proj, proj, pek, w1k, b1k, w2k, pev, w1v, b1v, w2v)


def _compress_weights(pe, w1, b1, w2):
    G = NSA_KV_HEADS
    pe2 = jnp.tile(pe, (1, G))
    w1l = w1.astype(BF16).reshape(CMP_LEN, HEAD_DIM, CMP_HIDDEN)
    w2b = w2.astype(BF16)
    z1 = jnp.zeros_like(w1l)
    z2 = jnp.zeros_like(w2b)
    w1_bd = jnp.concatenate([jnp.concatenate([w1l, z1], axis=2), jnp.concatenate([z1, w1l], axis=2)], axis=1)
    w2_bd = jnp.concatenate([jnp.concatenate([w2b, z2], axis=1), jnp.concatenate([z2, w2b], axis=1)], axis=0)
    return pe2, w1_bd, jnp.tile(b1.reshape(1, -1), (1, G)), w2_bd


def _nsa_kernel(qn_ref, qr_ref, kc_ref, vct_ref, ks_ref, vst_ref, kw_ref, vwt_ref, gt_ref, ot_ref,
                o_ref, acc_ref, m_ref, l_ref, s_ref, cmax_ref, *, S):
    i = pl.program_id(1)
    nc = S // CMP_STRIDE
    ns = S // SEL_BLOCK
    k_top = min(SEL_TOPK, ns)
    G = NSA_KV_HEADS
    R = H_NSA // G
    RT = R * TQ
    HD = HEAD_DIM
    low = lax.broadcasted_iota(jnp.int32, (TQ, LANES), 1) < HD
    gates_t = gt_ref[...].T
    qn = qn_ref[...] * SCALE
    qr = qr_ref[...]

    def tiled(mask, s, fill):
        return jnp.concatenate([jnp.where(mask, s[:, r * TQ:(r + 1) * TQ], fill) for r in range(R)], axis=1)

    def stack(qfull, g):
        in_g = low if g == 0 else jnp.logical_not(low)
        parts = []
        for r in range(R):
            a, b = divmod(R * g + r, 2)
            blk = qfull[:, a * LANES:(a + 1) * LANES]
            if b != g:
                blk = pltpu.roll(blk, HD, axis=1)
            parts.append(jnp.where(in_g, blk, 0.0))
        return jnp.concatenate(parts, axis=0).astype(BF16)

    def rows_of(g, x):
        return x[g * HD:(g + 1) * HD, :]


    qn_s = [stack(qn, g) for g in range(G)]
    c_valid = (lax.broadcasted_iota(jnp.int32, (nc, TQ), 0) * CMP_STRIDE + (CMP_LEN - 1)
               <= lax.broadcasted_iota(jnp.int32, (nc, TQ), 1) + i * TQ)
    scs = [tiled(c_valid, _dot_nt(kc_ref[0], qn_s[g]), NEG) for g in range(G)]
    es = [tiled(c_valid, jnp.exp(sc - jnp.max(sc, axis=0, keepdims=True)), 0.0) for sc in scs]
    dens = [jnp.sum(e, axis=0, keepdims=True) for e in es]
    p_cmps = [e * (1.0 / jnp.where(den > 0.0, den, 1.0)) for e, den in zip(es, dens)]
    vct = vct_ref[0]
    o_cmps = [_dot(rows_of(g, vct), p_cmps[g].astype(BF16)) for g in range(G)]

    p_slcs = []
    for p in p_cmps:
        p_sum = p[:, 0:TQ]
        for r in range(1, R):
            p_sum = p_sum + p[:, r * TQ:(r + 1) * TQ]
        p_slcs.append(_dot(ot_ref[...], _split_bf16(p_sum, 0)))
    n_idx = lax.broadcasted_iota(jnp.int32, (ns, TQ), 0)
    t_s = lax.broadcasted_iota(jnp.int32, (ns, TQ), 1) + i * TQ
    forced = jnp.logical_or(n_idx == 0, n_idx == (t_s >> SEL_SHIFT))
    s_valid = n_idx * SEL_BLOCK <= t_s
    scores = [jnp.where(forced, 1e4, jnp.where(s_valid, p_slc, -1.0)) for p_slc in p_slcs]
    ranks = [jnp.zeros((ns, TQ), F32) for _ in range(G)]
    for m_i in range(ns):
        for g in range(G):
            sm = scores[g][m_i:m_i + 1, :]
            ge = jnp.where(sm >= scores[g], 1.0, 0.0)
            gt = jnp.where(sm > scores[g], 1.0, 0.0)
            ranks[g] = ranks[g] + jnp.where(n_idx > m_i, ge, gt)
    sels = []
    for g in range(G):
        sel = jnp.where(ranks[g] < k_top, 1.0, 0.0)
        if ns < LANES:
            sel = jnp.concatenate([sel, jnp.zeros((LANES - ns, TQ), F32)], axis=0)
        sels.append(sel.astype(BF16))

    qr_s = [stack(qr, g) for g in range(G)]
    m_ref[...] = jnp.full(m_ref.shape, NEG, F32)
    l_ref[...] = jnp.zeros_like(l_ref)
    acc_ref[...] = jnp.zeros_like(acc_ref)

    def stash(scored):
        for c, sc in scored:
            s_ref[c, 0:sc.shape[0], :] = sc
            cmax_ref[c] = jnp.max(sc, axis=0, keepdims=True)

    def absorb(items):
        stats = []
        for c, _, cmax, _ in items:
            m_old = m_ref[c]
            stats.append((m_old, jnp.maximum(m_old, cmax)))
        ps = [jnp.exp(sc - m_new) for (_, sc, _, _), (_, m_new) in zip(items, stats)]
        out = []
        for (c, _, _, vt), (m_old, m_new), p in zip(items, stats, ps):
            alpha = jnp.exp(m_old - m_new)
            m_ref[c] = m_new
            l_ref[c] = alpha * l_ref[c] + jnp.sum(p, axis=0, keepdims=True)
            out.append((c, alpha, _dot(vt, p.astype(BF16))))
        return out

    def accumulate(updates):
        for c, alpha, pv in updates:
            acc_ref[c] = alpha * acc_ref[c] + pv

    def chosen(g, off, width):
        blk = (lax.broadcasted_iota(jnp.int32, (width, LANES), 0) + off) >> SEL_SHIFT
        expand = jnp.where(blk == lax.broadcasted_iota(jnp.int32, (width, LANES), 1), 1.0, 0.0).astype(BF16)
        return _dot(expand, sels[g]) > 0.5

    n_win = jnp.minimum(i, (WINDOW + TQ - 2) // LANES) + 1
    n_w = n_win + ((i + 1 - n_win) & 1)
    n_old = (i + 1 - n_w) // 2

    def win_scores(j):
        off = pl.multiple_of((i - j) * LANES, LANES)
        s_pos = lax.broadcasted_iota(jnp.int32, (LANES, TQ), 0) + off
        t_q = lax.broadcasted_iota(jnp.int32, (LANES, TQ), 1) + i * TQ
        causal = s_pos <= t_q
        band = jnp.logical_and(causal, t_q - s_pos < WINDOW)
        ks_t, kw_t = ks_ref[pl.ds(off, LANES), :], kw_ref[pl.ds(off, LANES), :]
        scored = []
        for g in range(G):
            pick = jnp.logical_and(chosen(g, off, LANES), causal)
            scored.append((2 * g, tiled(pick, _dot_nt(ks_t, qr_s[g]), NEG)))
            scored.append((2 * g + 1, tiled(band, _dot_nt(kw_t, qr_s[g]), NEG)))
        return scored

    def win_items(j):
        vs_t, vw_t = vst_ref[0, i - j], vwt_ref[0, i - j]
        items = []
        for g in range(G):
            items.append((2 * g, s_ref[2 * g, 0:LANES, :], cmax_ref[2 * g], rows_of(g, vs_t)))
            items.append((2 * g + 1, s_ref[2 * g + 1, 0:LANES, :], cmax_ref[2 * g + 1], rows_of(g, vw_t)))
        return items

    def old_first_tile(j):
        return i - n_w - 2 * j - 1

    def old_scores(j):
        off = pl.multiple_of(old_first_tile(j) * LANES, LANES)
        ks_t = ks_ref[pl.ds(off, 2 * LANES), :]
        return [(2 * g, tiled(chosen(g, off, 2 * LANES), _dot_nt(ks_t, qr_s[g]), NEG)) for g in range(G)]

    def old_items(j):
        kb = old_first_tile(j)
        vs_t = jnp.concatenate([vst_ref[0, kb], vst_ref[0, kb + 1]], axis=1)
        return [(2 * g, s_ref[2 * g], cmax_ref[2 * g], rows_of(g, vs_t)) for g in range(G)]

    stash(win_scores(0))

    def win_body(j, carry):
        items = win_items(j - 1)
        scored = win_scores(j)
        updates = absorb(items)
        stash(scored)
        accumulate(updates)
        return carry

    lax.fori_loop(1, n_w, win_body, 0)

    @pl.when(n_old == 0)
    def _():
        accumulate(absorb(win_items(n_w - 1)))

    @pl.when(n_old > 0)
    def _():
        items = win_items(n_w - 1)
        scored = old_scores(0)
        updates = absorb(items)
        stash(scored)
        accumulate(updates)

        def old_body(j, carry):
            items = old_items(j - 1)
            scored = old_scores(j)
            updates = absorb(items)
            stash(scored)
            accumulate(updates)
            return carry

        lax.fori_loop(1, n_old, old_body, 0)
        accumulate(absorb(old_items(n_old - 1)))

    heads = [None] * H_NSA
    for g in range(G):
        l_sel, l_win = l_ref[2 * g], l_ref[2 * g + 1]
        o_sel = acc_ref[2 * g] * (1.0 / jnp.where(l_sel > 0.0, l_sel, 1.0))
        o_win = acc_ref[2 * g + 1] * (1.0 / jnp.where(l_win > 0.0, l_win, 1.0))
        for r in range(R):
            h = R * g + r
            cols = slice(r * TQ, (r + 1) * TQ)
            heads[h] = (gates_t[3 * h:3 * h + 1, :] * o_cmps[g][:, cols]
                        + gates_t[3 * h + 1:3 * h + 2, :] * o_sel[:, cols]
                        + gates_t[3 * h + 2:3 * h + 3, :] * o_win[:, cols])
    for a in range(H_NSA // 2):
        pair = jnp.concatenate([heads[2 * a], heads[2 * a + 1]], axis=0)
        o_ref[:, a * LANES:(a + 1) * LANES] = pair.T.astype(o_ref.dtype)


def _nsa_attention(proj, q_rope, k_cmp, v_cmp_t, ks_r, vs_t, kw_r, vw_t, gates, overlap_t, B, S):
    nq = S // TQ
    nc = S // CMP_STRIDE
    nk = S // LANES
    G = NSA_KV_HEADS
    rt = H_NSA // G * TQ
    k_full = pl.BlockSpec((S, LANES), lambda b, i: (b, 0))
    vt_full = pl.BlockSpec((1, nk, LANES, LANES), lambda b, i: (b, 0, 0, 0))
    return pl.pallas_call(
        functools.partial(_nsa_kernel, S=S),
        grid=(B, nq),
        in_specs=[pl.BlockSpec((TQ, W_HEADS), lambda b, i: (b * nq + i, C_NQ // W_HEADS)),
                  pl.BlockSpec((TQ, W_HEADS), lambda b, i: (b * nq + i, 0)),
                  pl.BlockSpec((1, nc, LANES), lambda b, i: (b, 0, 0)),
                  pl.BlockSpec((1, LANES, nc), lambda b, i: (b, 0, 0)),
                  k_full, vt_full, k_full, vt_full,
                  pl.BlockSpec((TQ, LANES), lambda b, i: (b * nq + i, 0)),
                  pl.BlockSpec(overlap_t.shape, lambda b, i: (0, 0))],
        out_specs=pl.BlockSpec((TQ, W_HEADS), lambda b, i: (b * nq + i, 0)),
        out_shape=jax.ShapeDtypeStruct((B * S, W_HEADS), BF16),
        scratch_shapes=[pltpu.VMEM((2 * G, HEAD_DIM, rt), F32), pltpu.VMEM((2 * G, 1, rt), F32),
                        pltpu.VMEM((2 * G, 1, rt), F32), pltpu.VMEM((2 * G, 2 * LANES, rt), F32),
                        pltpu.VMEM((2 * G, 1, rt), F32)],
        compiler_params=_cparams(("parallel", "arbitrary")),
    )(proj, q_rope, k_cmp, v_cmp_t, ks_r, vs_t, kw_r, vw_t, gates, overlap_t)


def _merge_kernel(x_ref, gpre_ref, wm_ref, osb_ref, onsa_ref, ofox_ref,
                  wsb_ref, wnsa_ref, wfox_ref, wout_ref, g_ref, o_ref):
    x = x_ref[...]
    D = x.shape[1]
    h = _rms(x, gpre_ref[...]).astype(BF16)
    y = None
    for c, (b_ref, w_ref) in enumerate(((osb_ref, wsb_ref), (onsa_ref, wnsa_ref), (ofox_ref, wfox_ref))):
        gate = jax.nn.sigmoid(_dot(h, wm_ref[:, c * D:(c + 1) * D]))
        term = gate * _dot(b_ref[...], w_ref[...])
        y = term if y is None else y + term
    z = _dot(y.astype(BF16), wout_ref[...])
    o_ref[...] = x + _rms(z, g_ref[...])


def _merge(x, g_pre, w_merge, o_sb, o_nsa, o_fox, w_sb, w_nsa, w_fox, w_out, g, tm):
    T, D = x.shape
    row = lambda w: pl.BlockSpec((tm, w), lambda i: (i, 0))
    full = lambda a: pl.BlockSpec(a.shape, lambda i: (0, 0))
    return pl.pallas_call(
        _merge_kernel,
        grid=(T // tm,),
        in_specs=[row(D), full(g_pre), full(w_merge), row(W_HEADS), row(W_HEADS), row(W_HEADS),
                  full(w_sb), full(w_nsa), full(w_fox), full(w_out), full(g)],
        out_specs=row(D),
        out_shape=jax.ShapeDtypeStruct((T, D), F32),
        compiler_params=_cparams(("parallel",)),
    )(x, g_pre, w_merge, o_sb, o_nsa, o_fox, w_sb, w_nsa, w_fox, w_out, g)


def _mem_kv_kernel(mem_ref, g_ref, wk_ref, wv_ref, k_ref, vt_ref):
    mn = _rms(mem_ref[0], g_ref[...]).astype(BF16)
    k_ref[0] = _dot(mn, wk_ref[...]).astype(BF16)
    vt_ref[0] = _dot(mn, wv_ref[...]).T.astype(BF16)


def _mem_kv(mem, g, wk, wv):
    B, M, D = mem.shape
    full = lambda a: pl.BlockSpec(a.shape, lambda b: (0, 0))
    return pl.pallas_call(
        _mem_kv_kernel,
        grid=(B,),
        in_specs=[pl.BlockSpec((1, M, D), lambda b: (b, 0, 0)), full(g), full(wk), full(wv)],
        out_specs=[pl.BlockSpec((1, M, W_MEM), lambda b: (b, 0, 0)),
                   pl.BlockSpec((1, W_MEM, M), lambda b: (b, 0, 0))],
        out_shape=[jax.ShapeDtypeStruct((B, M, W_MEM), BF16), jax.ShapeDtypeStruct((B, W_MEM, M), BF16)],
        compiler_params=_cparams(("parallel",)),
    )(mem, g, wk, wv)


def _mem_attn_kernel(x_ref, g_ref, wq_ref, k_ref, vt_ref, wo_ref, gp_ref, o_ref):
    x = x_ref[...]
    q = _dot(_rms(x, g_ref[...]).astype(BF16), wq_ref[...]) * SCALE
    k = k_ref[0]
    vt = vt_ref[0]
    lane = lax.broadcasted_iota(jnp.int32, q.shape, 1)
    heads = range(H_MEM)
    qh = [jnp.where(jnp.logical_and(lane >= h * HEAD_DIM, lane < (h + 1) * HEAD_DIM), q, 0.0).astype(BF16)
          for h in heads]
    ss = [_dot_nt(k, qh[h]) for h in heads]
    es = [jnp.exp(s - jnp.max(s, axis=0, keepdims=True)) for s in ss]
    ps = [(e / jnp.sum(e, axis=0, keepdims=True)).astype(BF16) for e in es]
    o_t = jnp.concatenate([_dot(vt[h * HEAD_DIM:(h + 1) * HEAD_DIM, :], ps[h]) for h in heads], axis=0)
    y = _dot(o_t.T.astype(BF16), wo_ref[...])
    o_ref[...] = x + _rms(y, gp_ref[...])


def _mem_attn(x, g_pre, wq, k, v_t, wo, g_post, B, S, tm):
    T, D = x.shape
    M = k.shape[1]
    nt = S // tm
    full = lambda a: pl.BlockSpec(a.shape, lambda b, i: (0, 0))
    row = pl.BlockSpec((tm, D), lambda b, i: (b * nt + i, 0))
    return pl.pallas_call(
        _mem_attn_kernel,
        grid=(B, nt),
        in_specs=[row, full(g_pre), full(wq), pl.BlockSpec((1, M, W_MEM), lambda b, i: (b, 0, 0)),
                  pl.BlockSpec((1, W_MEM, M), lambda b, i: (b, 0, 0)), full(wo), full(g_post)],
        out_specs=row,
        out_shape=jax.ShapeDtypeStruct((T, D), F32),
        compiler_params=_cparams(("parallel", "parallel")),
    )(x, g_pre, wq, k, v_t, wo, g_post)


def _ffn_kernel(x_ref, g_ref, wg_ref, wu_ref, wd_ref, gp_ref, o_ref, *, tf):
    x = x_ref[...]
    h = _rms(x, g_ref[...]).astype(BF16)
    y = None
    for c in range(wg_ref.shape[1] // tf):
        cols = slice(c * tf, (c + 1) * tf)
        a = _dot(h, wg_ref[:, cols])
        u = _dot(h, wu_ref[:, cols])
        part = _dot((a * jax.nn.sigmoid(a) * u).astype(BF16), wd_ref[cols, :])
        y = part if y is None else y + part
    o_ref[...] = x + _rms(y, gp_ref[...])


def _ffn(x, g_pre, wg, wu, wd, g_post, tm, tf):
    T, D = x.shape
    row = pl.BlockSpec((tm, D), lambda i: (i, 0))
    vec = pl.BlockSpec((1, D), lambda i: (0, 0))
    resident = lambda a: pl.BlockSpec(a.shape, lambda i: (0, 0), pipeline_mode=pl.Buffered(1))
    return pl.pallas_call(
        functools.partial(_ffn_kernel, tf=tf),
        grid=(T // tm,),
        in_specs=[row, vec, resident(wg), resident(wu), resident(wd), vec],
        out_specs=row,
        out_shape=jax.ShapeDtypeStruct((T, D), F32),
        compiler_params=_cparams(("parallel",)),
    )(x, g_pre, wg, wu, wd, g_post)


def _scan_matrix():
    j = np.arange(LANES)
    later = (j[:, None] > j[None, :]).astype(np.float32)
    u = np.concatenate([later, np.ones((LANES, LANES), np.float32)], axis=1)
    return jnp.asarray(np.concatenate([u, u], axis=0), dtype=BF16)


def _overlap_t(S):
    nc, ns = S // CMP_STRIDE, S // SEL_BLOCK
    c0 = np.arange(nc) * CMP_STRIDE
    n0 = np.arange(ns) * SEL_BLOCK
    ov = (c0[None, :] < n0[:, None] + SEL_BLOCK) & (c0[None, :] + CMP_LEN > n0[:, None])
    ov = ov & (np.arange(nc)[None, :] < nc - 1)
    ov = ov.astype(np.float32)
    return jnp.asarray(np.concatenate([ov, ov], axis=1), dtype=BF16)


FOX_F_LANE = 24
N_PARTS = 3


def _fox_bias_placement():
    n_pair = H_FOX // 2
    pa = np.zeros((N_PARTS * LANES, n_pair * 2 * LANES), np.float32)
    pb = np.zeros((N_PARTS * LANES, n_pair * LANES), np.float32)
    oa = np.zeros((1, n_pair * 2 * LANES), np.float32)
    ob = np.zeros((1, n_pair * LANES), np.float32)
    for p in range(n_pair):
        for e in range(2):
            src = FOX_F_LANE + 2 * p + e
            for x in range(N_PARTS):
                pa[x * LANES + src, (2 * p + e) * LANES + 8 * e + x] = 1.0
                oa[0, (2 * p + e) * LANES + 8 * e + N_PARTS + x] = 1.0
                pb[x * LANES + src, p * LANES + 8 * e + N_PARTS + x] = -1.0
                ob[0, p * LANES + 8 * e + x] = 1.0
    return (jnp.asarray(pa, dtype=BF16), jnp.asarray(pb, dtype=BF16), jnp.asarray(oa), jnp.asarray(ob))


def _rope_tables():
    half = ROPE_DIM // 2
    inv_freq = ROPE_THETA ** (-jnp.arange(half, dtype=F32) / half)
    d = np.arange(LANES) % HEAD_DIM
    rot = d < ROPE_DIM
    one_hot = (np.arange(half)[:, None] == (d % half)[None, :]) & rot[None, :]
    to_cos = one_hot.astype(np.float32)
    to_up = (one_hot & (d >= half)[None, :]).astype(np.float32)
    to_dn = -(one_hot & (d < half)[None, :]).astype(np.float32)
    zero = np.zeros_like(to_cos)
    cos_rows = np.concatenate([to_cos, zero, zero], axis=1)
    sin_rows = np.concatenate([zero, to_up, to_dn], axis=1)
    spread = np.concatenate([cos_rows] * N_PARTS + [sin_rows] * N_PARTS, axis=0)
    base = np.zeros((8, LANES), np.float32)
    base[0] = (~rot).astype(np.float32)
    return jnp.asarray(base), inv_freq.reshape(half, 1), jnp.asarray(spread, dtype=BF16)


def _reorder_w_in(w):
    pad = jnp.zeros((w.shape[0], MISC_W - (_O_FOX_Q - _O_NSA_G) - (_O_MERGE - _O_FOX_F)), w.dtype)
    return jnp.concatenate([
        w[:, :_O_NSA_G - 6 * LANES],
        w[:, _O_FOX_Q:_O_FOX_F],
        w[:, _O_NSA_G - 6 * LANES:_O_NSA_G],
        w[:, _O_NSA_G:_O_FOX_Q],
        w[:, _O_FOX_F:_O_MERGE],
        pad], axis=1)


def _row_tile(n, target):
    t = min(n, target)
    while n % t:
        t //= 2
    return t


def kernel(x, mem, positions, g_pre_mix, g_post_mix, g_pre_mem, g_mem, g_post_mem, g_pre_ffn, g_post_ffn,
           w_in, b_fox_f, cmp_pe_k, cmp_w1_k, cmp_b1_k, cmp_w2_k, cmp_pe_v, cmp_w1_v, cmp_b1_v, cmp_w2_v,
           w_up_sb, w_up_nsa, w_up_fox, w_out, w_mem_q, w_mem_k, w_mem_v, w_mem_o,
           w_ffn_gate, w_ffn_up, w_ffn_down):
    B, S, D = x.shape
    T = B * S
    depth = w_in.shape[0]
    u2 = _scan_matrix()
    overlap_t = _overlap_t(S)
    rope_c, rope_f, rope_e = _rope_tables()
    place = _fox_bias_placement()
    pos3 = positions.reshape(B, 1, S)
    vec = lambda g: g.reshape(1, -1)
    tm_mid = _row_tile(T, 512)
    d_ff = w_ffn_gate.shape[2]
    tf = 2 * LANES if d_ff % (2 * LANES) == 0 else d_ff

    w_in_b = w_in[:, :, :_O_MERGE].astype(BF16)
    xf = x.reshape(T, D)
    for l in range(depth):
        proj = _norm_matmul(xf, vec(g_pre_mix[l]), _reorder_w_in(w_in_b[l]), tm_mid, 4 * LANES)

        o_sb = _sb_attention(proj, u2, B, S)

        bias_row = jnp.zeros((1, LANES), F32).at[0, FOX_F_LANE:FOX_F_LANE + H_FOX].set(b_fox_f[l])
        gates, fox_a, fox_b, q_rope, ks_r, kw_r, v_fox_t, vs_t, vw_t = _prep(
            proj, bias_row, pos3, rope_c, rope_f, rope_e, place, B, S)
        o_fox = _fox_attention(proj, v_fox_t, fox_a, fox_b, B, S)

        k_cmp, v_cmp_t = _compress(
            proj, *_compress_weights(cmp_pe_k[l], cmp_w1_k[l], cmp_b1_k[l], cmp_w2_k[l]),
            *_compress_weights(cmp_pe_v[l], cmp_w1_v[l], cmp_b1_v[l], cmp_w2_v[l]), B, S)
        o_nsa = _nsa_attention(proj, q_rope, k_cmp, v_cmp_t, ks_r, vs_t, kw_r, vw_t, gates, overlap_t, B, S)

        xf = _merge(xf, vec(g_pre_mix[l]), w_in[l][:, _O_MERGE:_O_END].astype(BF16), o_sb, o_nsa, o_fox,
                    w_up_sb[l].astype(BF16), w_up_nsa[l].astype(BF16),
                    w_up_fox[l].astype(BF16), w_out[l].astype(BF16), vec(g_post_mix[l]), tm_mid)

        k_mem, v_mem = _mem_kv(mem, vec(g_mem[l]), w_mem_k[l].astype(BF16), w_mem_v[l].astype(BF16))
        xf = _mem_attn(xf, vec(g_pre_mem[l]), w_mem_q[l].astype(BF16), k_mem, v_mem,
                       w_mem_o[l].astype(BF16), vec(g_post_mem[l]), B, S, _row_tile(S, 512))

        xf = _ffn(xf, vec(g_pre_ffn[l]), w_ffn_gate[l].astype(BF16), w_ffn_up[l].astype(BF16),
                  w_ffn_down[l].astype(BF16), vec(g_post_ffn[l]), tm_mid, tf)
    return xf.reshape(B, S, D)
```

```python
import functools

import numpy as np
import jax
import jax.numpy as jnp
from jax import lax
from jax.experimental import pallas as pl
from jax.experimental.pallas import tpu as pltpu

F32 = jnp.float32
BF16 = jnp.bfloat16

D_MODEL = 1024
HEAD_DIM = 64
H_SB = 8
H_NSA = 8
NSA_KV_HEADS = 2
H_FOX = 8
H_MEM = 4
N_BRANCH = 3
ROPE_THETA = 500000.0
ROPE_DIM = HEAD_DIM // 4
CMP_STRIDE = 16
CMP_LEN = 2 * CMP_STRIDE
CMP_HIDDEN = 256
SEL_BLOCK = 64
SEL_SHIFT = SEL_BLOCK.bit_length() - 1
SEL_TOPK = 8
WINDOW = 512
W_HEADS = 8 * HEAD_DIM
W_MEM = H_MEM * HEAD_DIM
EPS = 1e-6
SCALE = HEAD_DIM ** -0.5
NEG = -1e30
LOG_ZERO = -104.0
LOG2E = 1.4426950408889634

LANES = 128
TQ = 128
TQA = 256
TQF = 512

C_SBQ, C_SBK, C_SBV = 0, 512, 1024
C_NQ = 1536
C_FQ, C_FK, C_FV = 2048, 2560, 3072
C_NKV = 3584
C_MISC = 4352
MISC_W = 256
N_IN = 4608
_O_NSA_G, _O_FOX_Q, _O_FOX_F, _O_MERGE, _O_END = 2816, 2840, 4376, 4384, 7456

VMEM_LIMIT = 56 * 1024 * 1024


def _cparams(sem):
    return pltpu.CompilerParams(dimension_semantics=sem, vmem_limit_bytes=VMEM_LIMIT)


def _dot(a, b):
    return jnp.dot(a, b, preferred_element_type=F32)


def _dot_nt(a, b):
    return lax.dot_general(a, b, (((1,), (1,)), ((), ())), preferred_element_type=F32)


def _rms(x, g):
    ms = jnp.mean(x * x, axis=-1, keepdims=True)
    return x * lax.rsqrt(ms + EPS) * g


def _split_bf16(x, axis=1):
    hi = x.astype(BF16)
    lo = (x - hi.astype(F32)).astype(BF16)
    return jnp.concatenate([hi, lo], axis=axis)


def _norm_matmul_kernel(x_ref, g_ref, w_ref, o_ref, *, tn):
    h = _rms(x_ref[...], g_ref[...]).astype(BF16)
    for c in range(w_ref.shape[0] // tn):
        o_ref[:, c * tn:(c + 1) * tn] = _dot_nt(h, w_ref[c * tn:(c + 1) * tn, :])


def _norm_matmul(x, g, w, tm, tn):
    T, D = x.shape
    N = w.shape[0]
    return pl.pallas_call(
        functools.partial(_norm_matmul_kernel, tn=tn),
        grid=(T // tm,),
        in_specs=[pl.BlockSpec((tm, D), lambda i: (i, 0)),
                  pl.BlockSpec((1, D), lambda i: (0, 0)),
                  pl.BlockSpec((N, D), lambda i: (0, 0), pipeline_mode=pl.Buffered(1))],
        out_specs=pl.BlockSpec((tm, N), lambda i: (i, 0)),
        out_shape=jax.ShapeDtypeStruct((T, N), F32),
        compiler_params=_cparams(("parallel",)),
    )(x, g, w)


def _stack_pair(q, low):
    return jnp.concatenate([jnp.where(low, q, 0.0), jnp.where(low, 0.0, q)], axis=0).astype(BF16)


def _sb_kernel(q_ref, k_ref, v_ref, u_ref, o_ref, acc_ref, cs_ref, arg_ref, rs_ref):
    i = pl.program_id(2)
    rt = 2 * TQA
    n_grp = TQA // LANES
    lane = lax.broadcasted_iota(jnp.int32, (rt, LANES), 1)
    t_q = (lax.broadcasted_iota(jnp.int32, (rt, LANES), 0) & (TQA - 1)) + i * TQA
    low = lax.broadcasted_iota(jnp.int32, (TQA, LANES), 1) < HEAD_DIM
    qs = _stack_pair(q_ref[...] * SCALE, low)

    def offsets(kb_first):
        return [pl.multiple_of((kb_first - d) * LANES, LANES) for d in range(n_grp)]

    def logits(offs):
        return [_dot_nt(qs, k_ref[pl.ds(off, LANES), :].astype(BF16)) for off in offs]

    def scan(offs, zs, on_diagonal):
        drops, log_betas, stricts = [], [], []
        for off, z in zip(offs, zs):
            drop = jnp.maximum(z, 0.0) + jnp.log(1.0 + jnp.exp2(jnp.abs(z) * -LOG2E))
            log_betas.append(z - drop)
            if on_diagonal:
                stricts.append((lane + off) < t_q)
                drop = jnp.where(stricts[-1], drop, 0.0)
            drops.append(drop)
        c2s = [_dot(_split_bf16(drop), u_ref[...]) for drop in drops]
        args = [log_beta - c2[:, :LANES] for log_beta, c2 in zip(log_betas, c2s)]
        if on_diagonal:
            args = [jnp.where(strict, arg, NEG) for strict, arg in zip(stricts, args)]
        return args, [c2[:, LANES:] for c2 in c2s]

    def stash(args, row_sums):
        for d in range(n_grp):
            arg_ref[d] = args[d]
            rs_ref[d] = row_sums[d]

    def weights(offs, args, row_sums):
        cs = cs_ref[...]
        pvs = []
        for d, off in enumerate(offs):
            w = jnp.exp(args[d] - cs)
            pvs.append(_dot(w.astype(BF16), v_ref[pl.ds(off, LANES), :].astype(BF16)))
            cs = cs + row_sums[d]
        return cs, pvs

    def accumulate(cs, pvs):
        acc = acc_ref[...]
        for pv in pvs:
            acc = acc + pv
        acc_ref[...] = acc
        cs_ref[...] = cs

    acc_ref[...] = jnp.zeros_like(acc_ref)
    cs_ref[...] = jnp.zeros_like(cs_ref)
    def reaches_past(cs, row_sums):
        for rs in row_sums:
            cs = cs + rs
        return (jnp.min(cs) <= -LOG_ZERO).astype(jnp.int32)

    first = offsets(n_grp * i + n_grp - 1)
    first_args, first_sums = scan(first, logits(first), True)
    stash(first_args, first_sums)

    def cond(carry):
        j, alive = carry
        return jnp.logical_and(j < i, alive == 1)

    def body(carry):
        j, _ = carry
        cur = offsets(n_grp * (i - j) + n_grp - 1)
        nxt = offsets(n_grp * (i - j) - 1)
        args = [arg_ref[d] for d in range(n_grp)]
        row_sums = [rs_ref[d] for d in range(n_grp)]
        zs = logits(nxt)
        cs, pvs = weights(cur, args, row_sums)
        nxt_args, nxt_sums = scan(nxt, zs, False)
        alive = reaches_past(cs, nxt_sums)
        accumulate(cs, pvs)
        stash(nxt_args, nxt_sums)
        return j + 1, alive

    j_last, _ = lax.while_loop(cond, body, (jnp.int32(0), reaches_past(cs_ref[...], first_sums)))
    last = offsets(n_grp * (i - j_last) + n_grp - 1)
    accumulate(*weights(last, [arg_ref[d] for d in range(n_grp)], [rs_ref[d] for d in range(n_grp)]))
    acc = acc_ref[...]
    o_ref[...] = jnp.where(low, acc[:TQA], acc[TQA:]).astype(o_ref.dtype)


def _sb_attention(proj, u2, B, S):
    nq = S // TQA
    cq, ck, cv = C_SBQ // LANES, C_SBK // LANES, C_SBV // LANES
    return pl.pallas_call(
        _sb_kernel,
        grid=(B, H_SB // 2, nq),
        in_specs=[pl.BlockSpec((TQA, LANES), lambda b, p, i: (b * nq + i, cq + p)),
                  pl.BlockSpec((S, LANES), lambda b, p, i: (b, ck + p)),
                  pl.BlockSpec((S, LANES), lambda b, p, i: (b, cv + p)),
                  pl.BlockSpec((2 * LANES, 2 * LANES), lambda b, p, i: (0, 0))],
        out_specs=pl.BlockSpec((TQA, LANES), lambda b, p, i: (b * nq + i, p)),
        out_shape=jax.ShapeDtypeStruct((B * S, W_HEADS), BF16),
        scratch_shapes=[pltpu.VMEM((2 * TQA, LANES), F32), pltpu.VMEM((2 * TQA, LANES), F32),
                        pltpu.VMEM((TQA // LANES, 2 * TQA, LANES), F32),
                        pltpu.VMEM((TQA // LANES, 2 * TQA, LANES), F32)],
        compiler_params=_cparams(("parallel", "parallel", "arbitrary")),
    )(proj, proj, proj, u2)


def _fox_kernel(q_ref, k_ref, vt_ref, fa_ref, fb_ref, o_ref, acc_ref, m_ref, l_ref, s_ref, cmax_ref):
    i = pl.program_id(2)
    rt = 2 * TQF
    n_diag = TQF // TQA
    low = lax.broadcasted_iota(jnp.int32, (TQF, LANES), 1) < HEAD_DIM
    qs = _stack_pair(q_ref[...] * SCALE, low)
    qa = jnp.concatenate([qs, jnp.concatenate([fa_ref[:, :LANES], fa_ref[:, LANES:]], axis=0)], axis=1)
    acc_ref[...] = jnp.zeros_like(acc_ref)
    l_ref[...] = jnp.zeros_like(l_ref)
    m_ref[...] = jnp.full(m_ref.shape, NEG, F32)

    def scores(kb, on_diagonal):
        off = pl.multiple_of(kb * TQA, TQA)
        kbias = jnp.concatenate([k_ref[pl.ds(off, TQA), :].astype(BF16), fb_ref[pl.ds(off, TQA), :]], axis=1)
        s = _dot_nt(kbias, qa)
        if on_diagonal:
            key_pos = lax.broadcasted_iota(jnp.int32, (TQA, rt), 0) + (kb - n_diag * i) * TQA
            q_pos = lax.broadcasted_iota(jnp.int32, (TQA, rt), 1) & (TQF - 1)
            s = jnp.where(key_pos <= q_pos, s, NEG)
        return s

    def stash(s):
        s_ref[...] = s
        cmax_ref[...] = jnp.max(s, axis=0, keepdims=True)

    def absorb(kb, s, cmax):
        m_old = m_ref[...]
        m_new = jnp.maximum(m_old, cmax)
        p = jnp.exp(s - m_new)
        alpha = jnp.exp(m_old - m_new)
        m_ref[...] = m_new
        l_ref[...] = alpha * l_ref[...] + jnp.sum(p, axis=0, keepdims=True)
        p = p.astype(BF16)
        vt = vt_ref[0, 0, kb]
        return alpha, [_dot(vt[e * HEAD_DIM:(e + 1) * HEAD_DIM, :], p[:, e * TQF:(e + 1) * TQF]) for e in range(2)]

    def accumulate(alpha, pvs):
        for e in range(2):
            acc_ref[e] = alpha[:, e * TQF:(e + 1) * TQF] * acc_ref[e] + pvs[e]

    d0 = n_diag * i
    stash(scores(d0, True))
    for d in range(1, n_diag):
        s_cur, cmax = s_ref[...], cmax_ref[...]
        s_next = scores(d0 + d, True)
        alpha, pvs = absorb(d0 + d - 1, s_cur, cmax)
        stash(s_next)
        accumulate(alpha, pvs)

    def body(j, carry):
        kb = jnp.where(j == 0, d0 + n_diag - 1, d0 - j)
        s_cur, cmax = s_ref[...], cmax_ref[...]
        s_next = scores(d0 - 1 - j, False)
        alpha, pvs = absorb(kb, s_cur, cmax)
        stash(s_next)
        accumulate(alpha, pvs)
        return carry

    lax.fori_loop(0, d0, body, 0)
    accumulate(*absorb(jnp.where(i == 0, n_diag - 1, 0), s_ref[...], cmax_ref[...]))
    l = l_ref[...]
    den = jnp.where(l > 0.0, l, 1.0)
    o_t = jnp.concatenate([acc_ref[0] / den[:, :TQF], acc_ref[1] / den[:, TQF:]], axis=0)
    o_ref[...] = o_t.T.astype(o_ref.dtype)


def _fox_attention(proj, v_t, fox_a, fox_b, B, S):
    nq = S // TQF
    cq, ck = C_FQ // LANES, C_FK // LANES
    return pl.pallas_call(
        _fox_kernel,
        grid=(B, H_FOX // 2, nq),
        in_specs=[pl.BlockSpec((TQF, LANES), lambda b, p, i: (b * nq + i, cq + p)),
                  pl.BlockSpec((S, LANES), lambda b, p, i: (b, ck + p)),
                  pl.BlockSpec((1, 1, S // TQA, LANES, TQA), lambda b, p, i: (b, p, 0, 0, 0)),
                  pl.BlockSpec((TQF, 2 * LANES), lambda b, p, i: (b * nq + i, p)),
                  pl.BlockSpec((S, LANES), lambda b, p, i: (b, p))],
        out_specs=pl.BlockSpec((TQF, LANES), lambda b, p, i: (b * nq + i, p)),
        out_shape=jax.ShapeDtypeStruct((B * S, W_HEADS), BF16),
        scratch_shapes=[pltpu.VMEM((2, HEAD_DIM, TQF), F32), pltpu.VMEM((1, 2 * TQF), F32),
                        pltpu.VMEM((1, 2 * TQF), F32), pltpu.VMEM((TQA, 2 * TQF), F32),
                        pltpu.VMEM((1, 2 * TQF), F32)],
        compiler_params=_cparams(("parallel", "parallel", "arbitrary")),
    )(proj, proj, v_t, fox_a, fox_b)


def _rope(x, cos, s_up, s_dn):
    return x * cos + pltpu.roll(x, ROPE_DIM // 2, axis=1) * s_up + pltpu.roll(x, LANES - ROPE_DIM // 2, axis=1) * s_dn


def _prep_kernel(misc_ref, bias_ref, pos_ref, rc_ref, rf_ref, re_ref, q_ref, ks_ref, kw_ref, pa_ref, pb_ref, oa_ref, ob_ref,
                 vf_ref, vs_ref, vw_ref,
                 gates_ref, fa_ref, fb_ref, qr_ref, ksr_ref, kwr_ref, vft_ref, vst_ref, vwt_ref, carry_ref, *, ts):
    @pl.when(pl.program_id(1) == 0)
    def _():
        carry_ref[...] = jnp.zeros_like(carry_ref)

    logits = misc_ref[...]
    gates_ref[...] = jax.nn.sigmoid(logits)
    zf = logits + bias_ref[...]
    x = jnp.minimum(zf, 0.0) - jnp.log(1.0 + jnp.exp(-jnp.abs(zf)))
    row = lax.broadcasted_iota(jnp.int32, (ts, LANES), 0)
    sh = 1
    while sh < ts:
        x = x + jnp.where(row >= sh, pltpu.roll(x, sh, axis=0), 0.0)
        sh *= 2
    x = x + carry_ref[...]
    carry_ref[...] = x[ts - 1:ts, :]
    hi = x.astype(BF16)
    r1 = x - hi.astype(F32)
    mid = r1.astype(BF16)
    lo = (r1 - mid.astype(F32)).astype(BF16)
    parts = jnp.concatenate([hi, mid, lo], axis=1)
    fa_ref[...] = (_dot(parts, pa_ref[...]) + oa_ref[...]).astype(BF16)
    fb_ref[...] = (_dot(parts, pb_ref[...]) + ob_ref[...]).astype(BF16)

    ang = rf_ref[...] * pos_ref[0].astype(F32)
    parts = []
    for t in (jnp.cos(ang), jnp.sin(ang)):
        hi = t.astype(BF16)
        r1 = t - hi.astype(F32)
        mid = r1.astype(BF16)
        parts += [hi, mid, (r1 - mid.astype(F32)).astype(BF16)]
    spread = lax.dot_general(jnp.concatenate(parts, axis=0), re_ref[...], (((0,), (0,)), ((), ())),
                             preferred_element_type=F32)
    cos = spread[:, :LANES] + rc_ref[0:1, :]
    s_up = spread[:, LANES:2 * LANES]
    s_dn = spread[:, 2 * LANES:]
    for a in range(W_HEADS // LANES):
        qa = q_ref[:, a * LANES:(a + 1) * LANES]
        qr_ref[:, a * LANES:(a + 1) * LANES] = _rope(qa, cos, s_up, s_dn) * SCALE
    ksr_ref[...] = _rope(ks_ref[...], cos, s_up, s_dn).astype(BF16)
    kwr_ref[...] = _rope(kw_ref[...], cos, s_up, s_dn).astype(BF16)

    for p in range(H_FOX // 2):
        for u in range(ts // TQA):
            vft_ref[0, p, u] = vf_ref[u * TQA:(u + 1) * TQA, p * LANES:(p + 1) * LANES].T.astype(BF16)
    for u in range(ts // LANES):
        vst_ref[0, u] = vs_ref[u * LANES:(u + 1) * LANES, :].T.astype(BF16)
        vwt_ref[0, u] = vw_ref[u * LANES:(u + 1) * LANES, :].T.astype(BF16)


def _prep(proj, bias_row, pos3, rope_c, rope_f, rope_e, place, B, S):
    ts = min(512, S)
    nt = S // ts
    T = B * S
    row_blk = lambda w: pl.BlockSpec((ts, w), lambda b, t: (b * nt + t, 0))
    full = lambda a: pl.BlockSpec(a.shape, lambda b, t: (0, 0))
    wa, wb = place[0].shape[1], place[1].shape[1]
    return pl.pallas_call(
        functools.partial(_prep_kernel, ts=ts),
        grid=(B, nt),
        in_specs=[pl.BlockSpec((ts, LANES), lambda b, t: (b * nt + t, C_MISC // LANES)),
                  pl.BlockSpec((1, LANES), lambda b, t: (0, 0)),
                  pl.BlockSpec((1, 1, ts), lambda b, t: (b, 0, t)),
                  pl.BlockSpec((8, LANES), lambda b, t: (0, 0)),
                  full(rope_f), full(rope_e),
                  pl.BlockSpec((ts, W_HEADS), lambda b, t: (b * nt + t, C_NQ // W_HEADS)),
                  pl.BlockSpec((ts, LANES), lambda b, t: (b * nt + t, C_NKV // LANES + 2)),
                  pl.BlockSpec((ts, LANES), lambda b, t: (b * nt + t, C_NKV // LANES + 4)),
                  full(place[0]), full(place[1]), full(place[2]), full(place[3]),
                  pl.BlockSpec((ts, W_HEADS), lambda b, t: (b * nt + t, C_FV // W_HEADS)),
                  pl.BlockSpec((ts, LANES), lambda b, t: (b * nt + t, C_NKV // LANES + 3)),
                  pl.BlockSpec((ts, LANES), lambda b, t: (b * nt + t, C_NKV // LANES + 5))],
        out_specs=[row_blk(LANES), row_blk(wa), row_blk(wb), row_blk(W_HEADS), row_blk(LANES), row_blk(LANES),
                   pl.BlockSpec((1, H_FOX // 2, ts // TQA, LANES, TQA), lambda b, t: (b, 0, t, 0, 0)),
                   pl.BlockSpec((1, ts // LANES, LANES, LANES), lambda b, t: (b, t, 0, 0)),
                   pl.BlockSpec((1, ts // LANES, LANES, LANES), lambda b, t: (b, t, 0, 0))],
        out_shape=[jax.ShapeDtypeStruct((T, LANES), F32),
                   jax.ShapeDtypeStruct((T, wa), BF16),
                   jax.ShapeDtypeStruct((T, wb), BF16),
                   jax.ShapeDtypeStruct((T, W_HEADS), F32),
                   jax.ShapeDtypeStruct((T, LANES), BF16),
                   jax.ShapeDtypeStruct((T, LANES), BF16),
                   jax.ShapeDtypeStruct((B, H_FOX // 2, S // TQA, LANES, TQA), BF16),
                   jax.ShapeDtypeStruct((B, S // LANES, LANES, LANES), BF16),
                   jax.ShapeDtypeStruct((B, S // LANES, LANES, LANES), BF16)],
        scratch_shapes=[pltpu.VMEM((1, LANES), F32)],
        compiler_params=_cparams(("parallel", "arbitrary")),
    )(proj, bias_row, pos3, rope_c, rope_f, rope_e, proj, proj, proj, *place, proj, proj, proj)


def _compress_kernel(kc_ref, vc_ref, pek_ref, w1k_ref, b1k_ref, w2k_ref,
                     pev_ref, w1v_ref, b1v_ref, w2v_ref, ok_ref, ov_ref, *, nc):
    for x_ref, pe_ref, w1_ref, b1_ref, w2_ref, o_ref in (
            (kc_ref, pek_ref, w1k_ref, b1k_ref, w2k_ref, ok_ref),
            (vc_ref, pev_ref, w1v_ref, b1v_ref, w2v_ref, ov_ref)):
        h_first = jnp.zeros((nc, NSA_KV_HEADS * CMP_HIDDEN), F32)
        h_second = jnp.zeros((nc, NSA_KV_HEADS * CMP_HIDDEN), F32)
        for l in range(CMP_STRIDE):
            x = x_ref[pl.ds(l, nc, stride=CMP_STRIDE), :]
            h_first = h_first + _dot((x + pe_ref[l:l + 1, :]).astype(BF16), w1_ref[l])
            h_second = h_second + _dot((x + pe_ref[CMP_STRIDE + l:CMP_STRIDE + l + 1, :]).astype(BF16),
                                       w1_ref[CMP_STRIDE + l])
        h = h_first + pltpu.roll(h_second, nc - 1, axis=0) + b1_ref[...]
        a = h * jax.nn.sigmoid(h)
        out = _dot(a.astype(BF16), w2_ref[...])
        o_ref[0] = (out.T if o_ref is ov_ref else out).astype(o_ref.dtype)


def _compress(proj, pek, w1k, b1k, w2k, pev, w1v, b1v, w2v, B, S):
    nc = S // CMP_STRIDE
    full = lambda a: pl.BlockSpec(a.shape, lambda b: (0,) * a.ndim)
    out_spec = pl.BlockSpec((1, nc, LANES), lambda b: (b, 0, 0))
    out_spec_t = pl.BlockSpec((1, LANES, nc), lambda b: (b, 0, 0))
    return pl.pallas_call(
        functools.partial(_compress_kernel, nc=nc),
        grid=(B,),
        in_specs=[pl.BlockSpec((S, LANES), lambda b: (b, C_NKV // LANES)),
                  pl.BlockSpec((S, LANES), lambda b: (b, C_NKV // LANES + 1)),
                  full(pek), full(w1k), full(b1k), full(w2k),
                  full(pev), full(w1v), full(b1v), full(w2v)],
        out_specs=[out_spec, out_spec_t],
        out_shape=[jax.ShapeDtypeStruct((B, nc, LANES), BF16), jax.ShapeDtypeStruct((B, LANES, nc), BF16)],
        compiler_params=_cparams(("parallel",)),
    )(proj, proj, pek, w1k, b1k, w2k, pev, w1v, b1v, w2v)


def _compress_weights(pe, w1, b1, w2):
    G = NSA_KV_HEADS
    pe2 = jnp.tile(pe, (1, G))
    w1l = w1.astype(BF16).reshape(CMP_LEN, HEAD_DIM, CMP_HIDDEN)
    w2b = w2.astype(BF16)
    z1 = jnp.zeros_like(w1l)
    z2 = jnp.zeros_like(w2b)
    w1_bd = jnp.concatenate([jnp.concatenate([w1l, z1], axis=2), jnp.concatenate([z1, w1l], axis=2)], axis=1)
    w2_bd = jnp.concatenate([jnp.concatenate([w2b, z2], axis=1), jnp.concatenate([z2, w2b], axis=1)], axis=0)
    return pe2, w1_bd, jnp.tile(b1.reshape(1, -1), (1, G)), w2_bd


def _nsa_kernel(qn_ref, qr_ref, kc_ref, vct_ref, ks_ref, vst_ref, kw_ref, vwt_ref, gt_ref, ot_ref,
                o_ref, acc_ref, m_ref, l_ref, s_ref, cmax_ref, *, S):
    i = pl.program_id(1)
    nc = S // CMP_STRIDE
    ns = S // SEL_BLOCK
    k_top = min(SEL_TOPK, ns)
    G = NSA_KV_HEADS
    R = H_NSA // G
    RT = R * TQ
    HD = HEAD_DIM
    low = lax.broadcasted_iota(jnp.int32, (TQ, LANES), 1) < HD
    gates_t = gt_ref[...].T
    qn = qn_ref[...] * SCALE
    qr = qr_ref[...]

    def tiled(mask, s, fill):
        return jnp.concatenate([jnp.where(mask, s[:, r * TQ:(r + 1) * TQ], fill) for r in range(R)], axis=1)

    def stack(qfull, g):
        in_g = low if g == 0 else jnp.logical_not(low)
        parts = []
        for r in range(R):
            a, b = divmod(R * g + r, 2)
            blk = qfull[:, a * LANES:(a + 1) * LANES]
            if b != g:
                blk = pltpu.roll(blk, HD, axis=1)
            parts.append(jnp.where(in_g, blk, 0.0))
        return jnp.concatenate(parts, axis=0).astype(BF16)

    def rows_of(g, x):
        return x[g * HD:(g + 1) * HD, :]


    qn_s = [stack(qn, g) for g in range(G)]
    c_valid = (lax.broadcasted_iota(jnp.int32, (nc, TQ), 0) * CMP_STRIDE + (CMP_LEN - 1)
               <= lax.broadcasted_iota(jnp.int32, (nc, TQ), 1) + i * TQ)
    scs = [tiled(c_valid, _dot_nt(kc_ref[0], qn_s[g]), NEG) for g in range(G)]
    es = [tiled(c_valid, jnp.exp(sc - jnp.max(sc, axis=0, keepdims=True)), 0.0) for sc in scs]
    dens = [jnp.sum(e, axis=0, keepdims=True) for e in es]
    p_cmps = [e * (1.0 / jnp.where(den > 0.0, den, 1.0)) for e, den in zip(es, dens)]
    vct = vct_ref[0]
    o_cmps = [_dot(rows_of(g, vct), p_cmps[g].astype(BF16)) for g in range(G)]

    p_slcs = []
    for p in p_cmps:
        p_sum = p[:, 0:TQ]
        for r in range(1, R):
            p_sum = p_sum + p[:, r * TQ:(r + 1) * TQ]
        p_slcs.append(_dot(ot_ref[...], _split_bf16(p_sum, 0)))
    n_idx = lax.broadcasted_iota(jnp.int32, (ns, TQ), 0)
    t_s = lax.broadcasted_iota(jnp.int32, (ns, TQ), 1) + i * TQ
    forced = jnp.logical_or(n_idx == 0, n_idx == (t_s >> SEL_SHIFT))
    s_valid = n_idx * SEL_BLOCK <= t_s
    scores = [jnp.where(forced, 1e4, jnp.where(s_valid, p_slc, -1.0)) for p_slc in p_slcs]
    ranks = [jnp.zeros((ns, TQ), F32) for _ in range(G)]
    for m_i in range(ns):
        for g in range(G):
            sm = scores[g][m_i:m_i + 1, :]
            ge = jnp.where(sm >= scores[g], 1.0, 0.0)
            gt = jnp.where(sm > scores[g], 1.0, 0.0)
            ranks[g] = ranks[g] + jnp.where(n_idx > m_i, ge, gt)
    sels = []
    for g in range(G):
        sel = jnp.where(ranks[g] < k_top, 1.0, 0.0)
        if ns < LANES:
            sel = jnp.concatenate([sel, jnp.zeros((LANES - ns, TQ), F32)], axis=0)
        sels.append(sel.astype(BF16))

    qr_s = [stack(qr, g) for g in range(G)]
    m_ref[...] = jnp.full(m_ref.shape, NEG, F32)
    l_ref[...] = jnp.zeros_like(l_ref)
    acc_ref[...] = jnp.zeros_like(acc_ref)

    def stash(scored):
        for c, sc in scored:
            s_ref[c, 0:sc.shape[0], :] = sc
            cmax_ref[c] = jnp.max(sc, axis=0, keepdims=True)

    def absorb(items):
        stats = []
        for c, _, cmax, _ in items:
            m_old = m_ref[c]
            stats.append((m_old, jnp.maximum(m_old, cmax)))
        ps = [jnp.exp(sc - m_new) for (_, sc, _, _), (_, m_new) in zip(items, stats)]
        out = []
        for (c, _, _, vt), (m_old, m_new), p in zip(items, stats, ps):
            alpha = jnp.exp(m_old - m_new)
            m_ref[c] = m_new
            l_ref[c] = alpha * l_ref[c] + jnp.sum(p, axis=0, keepdims=True)
            out.append((c, alpha, _dot(vt, p.astype(BF16))))
        return out

    def accumulate(updates):
        for c, alpha, pv in updates:
            acc_ref[c] = alpha * acc_ref[c] + pv

    def chosen(g, off, width):
        blk = (lax.broadcasted_iota(jnp.int32, (width, LANES), 0) + off) >> SEL_SHIFT
        expand = jnp.where(blk == lax.broadcasted_iota(jnp.int32, (width, LANES), 1), 1.0, 0.0).astype(BF16)
        return _dot(expand, sels[g]) > 0.5

    n_win = jnp.minimum(i, (WINDOW + TQ - 2) // LANES) + 1
    n_w = n_win + ((i + 1 - n_win) & 1)
    n_old = (i + 1 - n_w) // 2

    def win_scores(j):
        off = pl.multiple_of((i - j) * LANES, LANES)
        s_pos = lax.broadcasted_iota(jnp.int32, (LANES, TQ), 0) + off
        t_q = lax.broadcasted_iota(jnp.int32, (LANES, TQ), 1) + i * TQ
        causal = s_pos <= t_q
        band = jnp.logical_and(causal, t_q - s_pos < WINDOW)
        ks_t, kw_t = ks_ref[pl.ds(off, LANES), :], kw_ref[pl.ds(off, LANES), :]
        scored = []
        for g in range(G):
            pick = jnp.logical_and(chosen(g, off, LANES), causal)
            scored.append((2 * g, tiled(pick, _dot_nt(ks_t, qr_s[g]), NEG)))
            scored.append((2 * g + 1, tiled(band, _dot_nt(kw_t, qr_s[g]), NEG)))
        return scored

    def win_items(j):
        vs_t, vw_t = vst_ref[0, i - j], vwt_ref[0, i - j]
        items = []
        for g in range(G):
            items.append((2 * g, s_ref[2 * g, 0:LANES, :], cmax_ref[2 * g], rows_of(g, vs_t)))
            items.append((2 * g + 1, s_ref[2 * g + 1, 0:LANES, :], cmax_ref[2 * g + 1], rows_of(g, vw_t)))
        return items

    def old_first_tile(j):
        return i - n_w - 2 * j - 1

    def old_scores(j):
        off = pl.multiple_of(old_first_tile(j) * LANES, LANES)
        ks_t = ks_ref[pl.ds(off, 2 * LANES), :]
        return [(2 * g, tiled(chosen(g, off, 2 * LANES), _dot_nt(ks_t, qr_s[g]), NEG)) for g in range(G)]

    def old_items(j):
        kb = old_first_tile(j)
        vs_t = jnp.concatenate([vst_ref[0, kb], vst_ref[0, kb + 1]], axis=1)
        return [(2 * g, s_ref[2 * g], cmax_ref[2 * g], rows_of(g, vs_t)) for g in range(G)]

    stash(win_scores(0))

    def win_body(j, carry):
        items = win_items(j - 1)
        scored = win_scores(j)
        updates = absorb(items)
        stash(scored)
        accumulate(updates)
        return carry

    lax.fori_loop(1, n_w, win_body, 0)

    @pl.when(n_old == 0)
    def _():
        accumulate(absorb(win_items(n_w - 1)))

    @pl.when(n_old > 0)
    def _():
        items = win_items(n_w - 1)
        scored = old_scores(0)
        updates = absorb(items)
        stash(scored)
        accumulate(updates)

        def old_body(j, carry):
            items = old_items(j - 1)
            scored = old_scores(j)
            updates = absorb(items)
            stash(scored)
            accumulate(updates)
            return carry

        lax.fori_loop(1, n_old, old_body, 0)
        accumulate(absorb(old_items(n_old - 1)))

    heads = [None] * H_NSA
    for g in range(G):
        l_sel, l_win = l_ref[2 * g], l_ref[2 * g + 1]
        o_sel = acc_ref[2 * g] * (1.0 / jnp.where(l_sel > 0.0, l_sel, 1.0))
        o_win = acc_ref[2 * g + 1] * (1.0 / jnp.where(l_win > 0.0, l_win, 1.0))
        for r in range(R):
            h = R * g + r
            cols = slice(r * TQ, (r + 1) * TQ)
            heads[h] = (gates_t[3 * h:3 * h + 1, :] * o_cmps[g][:, cols]
                        + gates_t[3 * h + 1:3 * h + 2, :] * o_sel[:, cols]
                        + gates_t[3 * h + 2:3 * h + 3, :] * o_win[:, cols])
    for a in range(H_NSA // 2):
        pair = jnp.concatenate([heads[2 * a], heads[2 * a + 1]], axis=0)
        o_ref[:, a * LANES:(a + 1) * LANES] = pair.T.astype(o_ref.dtype)


def _nsa_attention(proj, q_rope, k_cmp, v_cmp_t, ks_r, vs_t, kw_r, vw_t, gates, overlap_t, B, S):
    nq = S // TQ
    nc = S // CMP_STRIDE
    nk = S // LANES
    G = NSA_KV_HEADS
    rt = H_NSA // G * TQ
    k_full = pl.BlockSpec((S, LANES), lambda b, i: (b, 0))
    vt_full = pl.BlockSpec((1, nk, LANES, LANES), lambda b, i: (b, 0, 0, 0))
    return pl.pallas_call(
        functools.partial(_nsa_kernel, S=S),
        grid=(B, nq),
        in_specs=[pl.BlockSpec((TQ, W_HEADS), lambda b, i: (b * nq + i, C_NQ // W_HEADS)),
                  pl.BlockSpec((TQ, W_HEADS), lambda b, i: (b * nq + i, 0)),
                  pl.BlockSpec((1, nc, LANES), lambda b, i: (b, 0, 0)),
                  pl.BlockSpec((1, LANES, nc), lambda b, i: (b, 0, 0)),
                  k_full, vt_full, k_full, vt_full,
                  pl.BlockSpec((TQ, LANES), lambda b, i: (b * nq + i, 0)),
                  pl.BlockSpec(overlap_t.shape, lambda b, i: (0, 0))],
        out_specs=pl.BlockSpec((TQ, W_HEADS), lambda b, i: (b * nq + i, 0)),
        out_shape=jax.ShapeDtypeStruct((B * S, W_HEADS), BF16),
        scratch_shapes=[pltpu.VMEM((2 * G, HEAD_DIM, rt), F32), pltpu.VMEM((2 * G, 1, rt), F32),
                        pltpu.VMEM((2 * G, 1, rt), F32), pltpu.VMEM((2 * G, 2 * LANES, rt), F32),
                        pltpu.VMEM((2 * G, 1, rt), F32)],
        compiler_params=_cparams(("parallel", "arbitrary")),
    )(proj, q_rope, k_cmp, v_cmp_t, ks_r, vs_t, kw_r, vw_t, gates, overlap_t)


def _merge_kernel(x_ref, gpre_ref, wm_ref, osb_ref, onsa_ref, ofox_ref,
                  wsb_ref, wnsa_ref, wfox_ref, wout_ref, g_ref, o_ref):
    x = x_ref[...]
    D = x.shape[1]
    h = _rms(x, gpre_ref[...]).astype(BF16)
    y = None
    for c, (b_ref, w_ref) in enumerate(((osb_ref, wsb_ref), (onsa_ref, wnsa_ref), (ofox_ref, wfox_ref))):
        gate = jax.nn.sigmoid(_dot_nt(h, wm_ref[c * D:(c + 1) * D, :]))
        term = gate * _dot(b_ref[...], w_ref[...])
        y = term if y is None else y + term
    z = _dot(y.astype(BF16), wout_ref[...])
    o_ref[...] = x + _rms(z, g_ref[...])


def _merge(x, g_pre, w_merge, o_sb, o_nsa, o_fox, w_sb, w_nsa, w_fox, w_out, g, tm):
    T, D = x.shape
    row = lambda w: pl.BlockSpec((tm, w), lambda i: (i, 0))
    full = lambda a: pl.BlockSpec(a.shape, lambda i: (0, 0))
    return pl.pallas_call(
        _merge_kernel,
        grid=(T // tm,),
        in_specs=[row(D), full(g_pre), full(w_merge), row(W_HEADS), row(W_HEADS), row(W_HEADS),
                  full(w_sb), full(w_nsa), full(w_fox), full(w_out), full(g)],
        out_specs=row(D),
        out_shape=jax.ShapeDtypeStruct((T, D), F32),
        compiler_params=_cparams(("parallel",)),
    )(x, g_pre, w_merge, o_sb, o_nsa, o_fox, w_sb, w_nsa, w_fox, w_out, g)


def _mem_kv_kernel(mem_ref, g_ref, wk_ref, wv_ref, k_ref, vt_ref):
    mn = _rms(mem_ref[0], g_ref[...]).astype(BF16)
    k_ref[0] = _dot(mn, wk_ref[...]).astype(BF16)
    vt_ref[0] = _dot(mn, wv_ref[...]).T.astype(BF16)


def _mem_kv(mem, g, wk, wv):
    B, M, D = mem.shape
    full = lambda a: pl.BlockSpec(a.shape, lambda b: (0, 0))
    return pl.pallas_call(
        _mem_kv_kernel,
        grid=(B,),
        in_specs=[pl.BlockSpec((1, M, D), lambda b: (b, 0, 0)), full(g), full(wk), full(wv)],
        out_specs=[pl.BlockSpec((1, M, W_MEM), lambda b: (b, 0, 0)),
                   pl.BlockSpec((1, W_MEM, M), lambda b: (b, 0, 0))],
        out_shape=[jax.ShapeDtypeStruct((B, M, W_MEM), BF16), jax.ShapeDtypeStruct((B, W_MEM, M), BF16)],
        compiler_params=_cparams(("parallel",)),
    )(mem, g, wk, wv)


def _mem_attn_kernel(x_ref, g_ref, wq_ref, k_ref, vt_ref, wo_ref, gp_ref, o_ref):
    x = x_ref[...]
    q = _dot(_rms(x, g_ref[...]).astype(BF16), wq_ref[...]) * SCALE
    k = k_ref[0]
    vt = vt_ref[0]
    lane = lax.broadcasted_iota(jnp.int32, q.shape, 1)
    heads = range(H_MEM)
    qh = [jnp.where(jnp.logical_and(lane >= h * HEAD_DIM, lane < (h + 1) * HEAD_DIM), q, 0.0).astype(BF16)
          for h in heads]
    ss = [_dot_nt(k, qh[h]) for h in heads]
    es = [jnp.exp(s - jnp.max(s, axis=0, keepdims=True)) for s in ss]
    ps = [(e / jnp.sum(e, axis=0, keepdims=True)).astype(BF16) for e in es]
    o_t = jnp.concatenate([_dot(vt[h * HEAD_DIM:(h + 1) * HEAD_DIM, :], ps[h]) for h in heads], axis=0)
    y = _dot(o_t.T.astype(BF16), wo_ref[...])
    o_ref[...] = x + _rms(y, gp_ref[...])


def _mem_attn(x, g_pre, wq, k, v_t, wo, g_post, B, S, tm):
    T, D = x.shape
    M = k.shape[1]
    nt = S // tm
    full = lambda a: pl.BlockSpec(a.shape, lambda b, i: (0, 0))
    row = pl.BlockSpec((tm, D), lambda b, i: (b * nt + i, 0))
    return pl.pallas_call(
        _mem_attn_kernel,
        grid=(B, nt),
        in_specs=[row, full(g_pre), full(wq), pl.BlockSpec((1, M, W_MEM), lambda b, i: (b, 0, 0)),
                  pl.BlockSpec((1, W_MEM, M), lambda b, i: (b, 0, 0)), full(wo), full(g_post)],
        out_specs=row,
        out_shape=jax.ShapeDtypeStruct((T, D), F32),
        compiler_params=_cparams(("parallel", "parallel")),
    )(x, g_pre, wq, k, v_t, wo, g_post)


def _ffn_kernel(x_ref, g_ref, wg_ref, wu_ref, wd_ref, gp_ref, o_ref, *, tf):
    x = x_ref[...]
    h = _rms(x, g_ref[...]).astype(BF16)
    y = None
    for c in range(wg_ref.shape[1] // tf):
        cols = slice(c * tf, (c + 1) * tf)
        a = _dot(h, wg_ref[:, cols])
        u = _dot(h, wu_ref[:, cols])
        part = _dot((a * jax.nn.sigmoid(a) * u).astype(BF16), wd_ref[cols, :])
        y = part if y is None else y + part
    o_ref[...] = x + _rms(y, gp_ref[...])


def _ffn(x, g_pre, wg, wu, wd, g_post, tm, tf):
    T, D = x.shape
    row = pl.BlockSpec((tm, D), lambda i: (i, 0))
    vec = pl.BlockSpec((1, D), lambda i: (0, 0))
    resident = lambda a: pl.BlockSpec(a.shape, lambda i: (0, 0), pipeline_mode=pl.Buffered(1))
    return pl.pallas_call(
        functools.partial(_ffn_kernel, tf=tf),
        grid=(T // tm,),
        in_specs=[row, vec, resident(wg), resident(wu), resident(wd), vec],
        out_specs=row,
        out_shape=jax.ShapeDtypeStruct((T, D), F32),
        compiler_params=_cparams(("parallel",)),
    )(x, g_pre, wg, wu, wd, g_post)


def _scan_matrix():
    j = np.arange(LANES)
    later = (j[:, None] > j[None, :]).astype(np.float32)
    u = np.concatenate([later, np.ones((LANES, LANES), np.float32)], axis=1)
    return jnp.asarray(np.concatenate([u, u], axis=0), dtype=BF16)


def _overlap_t(S):
    nc, ns = S // CMP_STRIDE, S // SEL_BLOCK
    c0 = np.arange(nc) * CMP_STRIDE
    n0 = np.arange(ns) * SEL_BLOCK
    ov = (c0[None, :] < n0[:, None] + SEL_BLOCK) & (c0[None, :] + CMP_LEN > n0[:, None])
    ov = ov & (np.arange(nc)[None, :] < nc - 1)
    ov = ov.astype(np.float32)
    return jnp.asarray(np.concatenate([ov, ov], axis=1), dtype=BF16)


FOX_F_LANE = 24
N_PARTS = 3


def _fox_bias_placement():
    n_pair = H_FOX // 2
    pa = np.zeros((N_PARTS * LANES, n_pair * 2 * LANES), np.float32)
    pb = np.zeros((N_PARTS * LANES, n_pair * LANES), np.float32)
    oa = np.zeros((1, n_pair * 2 * LANES), np.float32)
    ob = np.zeros((1, n_pair * LANES), np.float32)
    for p in range(n_pair):
        for e in range(2):
            src = FOX_F_LANE + 2 * p + e
            for x in range(N_PARTS):
                pa[x * LANES + src, (2 * p + e) * LANES + 8 * e + x] = 1.0
                oa[0, (2 * p + e) * LANES + 8 * e + N_PARTS + x] = 1.0
                pb[x * LANES + src, p * LANES + 8 * e + N_PARTS + x] = -1.0
                ob[0, p * LANES + 8 * e + x] = 1.0
    return (jnp.asarray(pa, dtype=BF16), jnp.asarray(pb, dtype=BF16), jnp.asarray(oa), jnp.asarray(ob))


def _rope_tables():
    half = ROPE_DIM // 2
    inv_freq = ROPE_THETA ** (-jnp.arange(half, dtype=F32) / half)
    d = np.arange(LANES) % HEAD_DIM
    rot = d < ROPE_DIM
    one_hot = (np.arange(half)[:, None] == (d % half)[None, :]) & rot[None, :]
    to_cos = one_hot.astype(np.float32)
    to_up = (one_hot & (d >= half)[None, :]).astype(np.float32)
    to_dn = -(one_hot & (d < half)[None, :]).astype(np.float32)
    zero = np.zeros_like(to_cos)
    cos_rows = np.concatenate([to_cos, zero, zero], axis=1)
    sin_rows = np.concatenate([zero, to_up, to_dn], axis=1)
    spread = np.concatenate([cos_rows] * N_PARTS + [sin_rows] * N_PARTS, axis=0)
    base = np.zeros((8, LANES), np.float32)
    base[0] = (~rot).astype(np.float32)
    return jnp.asarray(base), inv_freq.reshape(half, 1), jnp.asarray(spread, dtype=BF16)


def _reorder_w_in(wt):
    pad = jnp.zeros((MISC_W - (_O_FOX_Q - _O_NSA_G) - (_O_MERGE - _O_FOX_F), wt.shape[1]), wt.dtype)
    return jnp.concatenate([
        wt[:_O_NSA_G - 6 * LANES],
        wt[_O_FOX_Q:_O_FOX_F],
        wt[_O_NSA_G - 6 * LANES:_O_NSA_G],
        wt[_O_NSA_G:_O_FOX_Q],
        wt[_O_FOX_F:_O_MERGE],
        pad], axis=0)


def _row_tile(n, target):
    t = min(n, target)
    while n % t:
        t //= 2
    return t


def kernel(x, mem, positions, g_pre_mix, g_post_mix, g_pre_mem, g_mem, g_post_mem, g_pre_ffn, g_post_ffn,
           w_in, b_fox_f, cmp_pe_k, cmp_w1_k, cmp_b1_k, cmp_w2_k, cmp_pe_v, cmp_w1_v, cmp_b1_v, cmp_w2_v,
           w_up_sb, w_up_nsa, w_up_fox, w_out, w_mem_q, w_mem_k, w_mem_v, w_mem_o,
           w_ffn_gate, w_ffn_up, w_ffn_down):
    B, S, D = x.shape
    T = B * S
    depth = w_in.shape[0]
    u2 = _scan_matrix()
    overlap_t = _overlap_t(S)
    rope_c, rope_f, rope_e = _rope_tables()
    place = _fox_bias_placement()
    pos3 = positions.reshape(B, 1, S)
    vec = lambda g: g.reshape(1, -1)
    tm_mid = _row_tile(T, 512)
    d_ff = w_ffn_gate.shape[2]
    tf = 2 * LANES if d_ff % (2 * LANES) == 0 else d_ff

    w_in_t = jnp.swapaxes(w_in, 1, 2).astype(BF16)
    xf = x.reshape(T, D)
    for l in range(depth):
        proj = _norm_matmul(xf, vec(g_pre_mix[l]), _reorder_w_in(w_in_t[l]), tm_mid, 4 * LANES)

        o_sb = _sb_attention(proj, u2, B, S)

        bias_row = jnp.zeros((1, LANES), F32).at[0, FOX_F_LANE:FOX_F_LANE + H_FOX].set(b_fox_f[l])
        gates, fox_a, fox_b, q_rope, ks_r, kw_r, v_fox_t, vs_t, vw_t = _prep(
            proj, bias_row, pos3, rope_c, rope_f, rope_e, place, B, S)
        o_fox = _fox_attention(proj, v_fox_t, fox_a, fox_b, B, S)

        k_cmp, v_cmp_t = _compress(
            proj, *_compress_weights(cmp_pe_k[l], cmp_w1_k[l], cmp_b1_k[l], cmp_w2_k[l]),
            *_compress_weights(cmp_pe_v[l], cmp_w1_v[l], cmp_b1_v[l], cmp_w2_v[l]), B, S)
        o_nsa = _nsa_attention(proj, q_rope, k_cmp, v_cmp_t, ks_r, vs_t, kw_r, vw_t, gates, overlap_t, B, S)

        xf = _merge(xf, vec(g_pre_mix[l]), w_in_t[l][_O_MERGE:_O_END], o_sb, o_nsa, o_fox,
                    w_up_sb[l].astype(BF16), w_up_nsa[l].astype(BF16),
                    w_up_fox[l].astype(BF16), w_out[l].astype(BF16), vec(g_post_mix[l]), tm_mid)

        k_mem, v_mem = _mem_kv(mem, vec(g_mem[l]), w_mem_k[l].astype(BF16), w_mem_v[l].astype(BF16))
        xf = _mem_attn(xf, vec(g_pre_mem[l]), w_mem_q[l].astype(BF16), k_mem, v_mem,
                       w_mem_o[l].astype(BF16), vec(g_post_mem[l]), B, S, _row_tile(S, 512))

        xf = _ffn(xf, vec(g_pre_ffn[l]), w_ffn_gate[l].astype(BF16), w_ffn_up[l].astype(BF16),
                  w_ffn_down[l].astype(BF16), vec(g_post_ffn[l]), tm_mid, tf)
    return xf.reshape(B, S, D)
```
